```python
import jax, jax.numpy as jnp
from jax import lax
import numpy as np

D_MODEL = 2048
BATCH = 1
SEQ = 8192
DEPTH = 1

MEM_LEN = 256
D_MIX = 2 * D_MODEL
GM_WIDTH = D_MODEL // 2 * 2 // 2 * 2 // 2
GM_WIDTH = D_MIX // 2
GM_GROUPS = 4
GM_CHUNK = 128
SSM_WIDTH = D_MIX - GM_WIDTH
SSM_HEADDIM = 64
SSM_HEADS = SSM_WIDTH // SSM_HEADDIM
SSM_GROUPS = 8
SSM_STATE = 128
SSM_CONV = 4
SSM_CHUNK = 128
SSM_CONV_CH = SSM_WIDTH + 2 * SSM_GROUPS * SSM_STATE
IN_COLS = 2 * GM_WIDTH + SSM_WIDTH + SSM_CONV_CH + SSM_HEADS
XA_HEADS = 4
XA_HEADDIM = D_MODEL // XA_HEADS
D_FF = 5632
EPS = 1e-6

kernel_name = "hybrid_gmlp_ssd_macaron_memxattn"


def rmsnorm(x, w):
    xf = x.astype(jnp.float32)
    y = xf * lax.rsqrt(jnp.mean(xf * xf, axis=-1, keepdims=True) + EPS)
    return (y * w.astype(jnp.float32)).astype(x.dtype)


def swiglu(x, w_gu, w_down):
    g, u = jnp.split(x @ w_gu, 2, axis=-1)
    return (jax.nn.silu(g) * u) @ w_down


def chunked_sgu(u, v, v_norm_w, w_s, b_s):
    bn, s, _ = u.shape
    nc = s // GM_CHUNK
    cg = GM_WIDTH // GM_GROUPS
    v = rmsnorm(v, v_norm_w).reshape(bn, nc, GM_CHUNK, GM_GROUPS, cg)
    causal = jnp.tril(jnp.ones((GM_CHUNK, GM_CHUNK), dtype=bool))
    w = jnp.where(causal[None], w_s, jnp.zeros_like(w_s)).astype(v.dtype)
    mixed = jnp.einsum('gts,bcsgd->bctgd', w, v) + b_s.T.astype(v.dtype)[None, None, :, :, None]
    return u * mixed.reshape(bn, s, GM_WIDTH)


def causal_dwconv(x, w, b):
    k_w = w.shape[0]
    s = x.shape[1]
    xp = jnp.pad(x, ((0, 0), (k_w - 1, 0), (0, 0)))
    y = b + xp[:, 0:s] * w[0]
    for k in range(1, k_w):
        y = y + xp[:, k:k + s] * w[k]
    return y


def ssd_scan(x, dt, a, bm, cm):
    bn, s, h, p = x.shape
    g, n = bm.shape[-2:]
    r = h // g
    l = SSM_CHUNK
    nc = s // l
    x = x.reshape(bn, nc, l, g, r, p)
    dt = dt.reshape(bn, nc, l, g, r)
    bm = bm.reshape(bn, nc, l, g, n)
    cm = cm.reshape(bn, nc, l, g, n)
    cum = jnp.cumsum(dt * a.reshape(g, r), axis=2)
    xdt = x * dt[..., None]
    seg = cum[:, :, :, None] - cum[:, :, None]
    causal = jnp.tril(jnp.ones((l, l), dtype=bool))[:, :, None, None]
    decay = jnp.exp(jnp.where(causal, seg, -jnp.inf))
    scores = jnp.einsum('bclgn,bcsgn->bclsg', cm, bm)
    y_diag = jnp.einsum('bclsg,bclsgr,bcsgrp->bclgrp', scores, decay, xdt)
    decay_end = jnp.exp(cum[:, :, -1:] - cum)
    states = jnp.einsum('bclgn,bclgr,bclgrp->bcgrpn', bm, decay_end, xdt)
    chunk_decay = jnp.exp(cum[:, :, -1])

    def step(carry, inp):
        st, dec = inp
        return carry * dec[..., None, None] + st, carry

    init = jnp.zeros((bn, g, r, p, n), jnp.float32)
    _, prev = lax.scan(step, init, (jnp.moveaxis(states, 1, 0), jnp.moveaxis(chunk_decay, 1, 0)))
    prev = jnp.moveaxis(prev, 0, 1)
    y_off = jnp.einsum('bclgn,bcgrpn,bclgr->bclgrp', cm, prev, jnp.exp(cum))
    return (y_diag + y_off).reshape(bn, s, h, p)


def mamba2_group(z, xbc, dt_raw, conv_w, conv_b, dt_bias, a_log, d_skip, norm_w):
    f32 = jnp.float32
    xbc = jax.nn.silu(causal_dwconv(xbc, conv_w, conv_b))
    gn = SSM_GROUPS * SSM_STATE
    xs, bm, cm = jnp.split(xbc, [SSM_WIDTH, SSM_WIDTH + gn], axis=-1)
    bn, s, _ = xs.shape
    x_h = xs.reshape(bn, s, SSM_HEADS, SSM_HEADDIM).astype(f32)
    dt = jax.nn.softplus(dt_raw.astype(f32) + dt_bias.astype(f32))
    a = -jnp.exp(a_log.astype(f32))
    y = ssd_scan(x_h, dt, a,
                 bm.reshape(bn, s, SSM_GROUPS, SSM_STATE).astype(f32),
                 cm.reshape(bn, s, SSM_GROUPS, SSM_STATE).astype(f32))
    y = y + d_skip.astype(f32)[:, None] * x_h
    y = y.reshape(bn, s, SSM_WIDTH) * jax.nn.silu(z.astype(f32))
    return rmsnorm(y, norm_w).astype(z.dtype)


def memory_cross_attention(hn, memn, w_q, w_kv, w_o):
    bn, s, _ = hn.shape
    m = memn.shape[1]
    q = (hn @ w_q).reshape(bn, s, XA_HEADS, XA_HEADDIM)
    k, v = jnp.split(memn @ w_kv, 2, axis=-1)
    k = k.reshape(bn, m, XA_HEADS, XA_HEADDIM)
    v = v.reshape(bn, m, XA_HEADS, XA_HEADDIM)
    logits = jnp.einsum('bshd,bmhd->bhsm', q, k).astype(jnp.float32) * (XA_HEADDIM ** -0.5)
    probs = jax.nn.softmax(logits, axis=-1).astype(v.dtype)
    o = jnp.einsum('bhsm,bmhd->bshd', probs, v).reshape(bn, s, D_MODEL)
    return o @ w_o


def setup_inputs(seed: int = 0) -> dict:
    key = jax.random.key(seed)
    ks = jax.random.split(key, 32)
    f32 = jnp.float32

    def nrm(k, shape, scale):
        return jax.random.normal(k, shape, f32) * scale

    def gain(k, shape):
        return 1.0 + 0.1 * jax.random.normal(k, shape, f32)

    L = DEPTH
    dt0 = jnp.exp(jax.random.uniform(ks[12], (L, SSM_HEADS), f32) * (np.log(0.1) - np.log(0.001)) + np.log(0.001))
    dt_bias = dt0 + jnp.log(-jnp.expm1(-dt0))
    return {
        "x": nrm(ks[0], (BATCH, SEQ, D_MODEL), 1.0),
        "mem": nrm(ks[1], (BATCH, MEM_LEN, D_MODEL), 1.0),
        "ffn1_norm": gain(ks[2], (L, D_MODEL)),
        "ffn1_w_gu": nrm(ks[3], (L, D_MODEL, 2 * D_FF), D_MODEL ** -0.5),
        "ffn1_w_down": nrm(ks[4], (L, D_FF, D_MODEL), D_FF ** -0.5),
        "mix_norm": gain(ks[5], (L, D_MODEL)),
        "w_in": nrm(ks[6], (L, D_MODEL, IN_COLS), D_MODEL ** -0.5),
        "gm_v_norm": gain(ks[7], (L, GM_WIDTH)),
        "gm_w_s": nrm(ks[8], (L, GM_GROUPS, GM_CHUNK, GM_CHUNK), GM_CHUNK ** -0.5),
        "gm_b_s": gain(ks[9], (L, GM_GROUPS, GM_CHUNK)),
        "ssm_conv_w": nrm(ks[10], (L, SSM_CONV, SSM_CONV_CH), SSM_CONV ** -0.5),
        "ssm_conv_b": nrm(ks[11], (L, SSM_CONV_CH), 0.02),
        "ssm_dt_bias": dt_bias,
        "ssm_a_log": jnp.log(jax.random.uniform(ks[13], (L, SSM_HEADS), f32, 1.0, 16.0)),
        "ssm_d": gain(ks[14], (L, SSM_HEADS)),
        "ssm_norm": gain(ks[15], (L, SSM_WIDTH)),
        "w_out": nrm(ks[16], (L, D_MIX, D_MODEL), D_MIX ** -0.5),
        "xa_norm": gain(ks[17], (L, D_MODEL)),
        "mem_norm": gain(ks[18], (L, D_MODEL)),
        "xa_w_q": nrm(ks[19], (L, D_MODEL, D_MODEL), D_MODEL ** -0.5),
        "xa_w_kv": nrm(ks[20], (L, D_MODEL, 2 * D_MODEL), D_MODEL ** -0.5),
        "xa_w_o": nrm(ks[21], (L, D_MODEL, D_MODEL), D_MODEL ** -0.5),
        "ffn2_norm": gain(ks[22], (L, D_MODEL)),
        "ffn2_w_gu": nrm(ks[23], (L, D_MODEL, 2 * D_FF), D_MODEL ** -0.5),
        "ffn2_w_down": nrm(ks[24], (L, D_FF, D_MODEL), D_FF ** -0.5),
        "final_norm": gain(ks[25], (D_MODEL,)),
    }


def reference(x, mem, ffn1_norm, ffn1_w_gu, ffn1_w_down, mix_norm, w_in, gm_v_norm, gm_w_s, gm_b_s,
              ssm_conv_w, ssm_conv_b, ssm_dt_bias, ssm_a_log, ssm_d, ssm_norm, w_out,
              xa_norm, mem_norm, xa_w_q, xa_w_kv, xa_w_o, ffn2_norm, ffn2_w_gu, ffn2_w_down, final_norm):
    h = x
    split_at = [GM_WIDTH, 2 * GM_WIDTH, 2 * GM_WIDTH + SSM_WIDTH, 2 * GM_WIDTH + SSM_WIDTH + SSM_CONV_CH]
    for i in range(DEPTH):
        h = h + 0.5 * swiglu(rmsnorm(h, ffn1_norm[i]), ffn1_w_gu[i], ffn1_w_down[i])
        proj = rmsnorm(h, mix_norm[i]) @ w_in[i]
        gu, gv, z, xbc, dt_raw = jnp.split(proj, split_at, axis=-1)
        a_out = chunked_sgu(jax.nn.gelu(gu), jax.nn.gelu(gv), gm_v_norm[i], gm_w_s[i], gm_b_s[i])
        m_out = mamba2_group(z, xbc, dt_raw, ssm_conv_w[i], ssm_conv_b[i], ssm_dt_bias[i],
                             ssm_a_log[i], ssm_d[i], ssm_norm[i])
        h = h + jnp.concatenate([a_out, m_out], axis=-1) @ w_out[i]
        h = h + memory_cross_attention(rmsnorm(h, xa_norm[i]), rmsnorm(mem, mem_norm[i]),
                                       xa_w_q[i], xa_w_kv[i], xa_w_o[i])
        h = h + 0.5 * swiglu(rmsnorm(h, ffn2_norm[i]), ffn2_w_gu[i], ffn2_w_down[i])
    return rmsnorm(h, final_norm)
```

```python
import functools

import jax
import jax.numpy as jnp
from jax import lax
from jax.experimental import pallas as pl
from jax.experimental.pallas import tpu as pltpu

D_MODEL = 2048
SEQ = 8192
MEM_LEN = 256
GM_WIDTH = 2048
GM_GROUPS = 4
GM_GROUP_WIDTH = GM_WIDTH // GM_GROUPS
CHUNK = 128
SSM_WIDTH = 2048
SSM_HEADDIM = 64
SSM_HEADS = SSM_WIDTH // SSM_HEADDIM
SSM_GROUPS = 8
SSM_HEADS_PER_GROUP = SSM_HEADS // SSM_GROUPS
SSM_GROUP_WIDTH = SSM_HEADS_PER_GROUP * SSM_HEADDIM
SSM_STATE = 128
SSM_CONV = 4
SSM_BC_WIDTH = SSM_GROUPS * SSM_STATE
PROJ_MAIN = 2 * GM_WIDTH + SSM_WIDTH + SSM_WIDTH + 2 * SSM_BC_WIDTH
XA_HEADS = 4
XA_HEADDIM = D_MODEL // XA_HEADS
D_FF = 5632
EPS = 1e-6

LANES = 128
HALO_ROWS = 8
VMEM_LIMIT = 56 * 1024 * 1024

BF16 = jnp.bfloat16
F32 = jnp.float32


def _params(n_axes):
    return pltpu.CompilerParams(dimension_semantics=("arbitrary",) * n_axes,
                                vmem_limit_bytes=VMEM_LIMIT)


def _rmsnorm(x, w):
    return x * lax.rsqrt(jnp.mean(x * x, axis=-1, keepdims=True) + EPS) * w


def _sigmoid(x):
    return 1.0 / (1.0 + jnp.exp(-x))


def _silu(x):
    return x * _sigmoid(x)


def _gelu_tanh(x):
    return 0.5 * x * (1.0 + jnp.tanh(0.7978845608028654 * (x + 0.044715 * (x * x * x))))


def _dot(a, b):
    return jnp.dot(a, b, preferred_element_type=F32)


def _dot_nt(a, b):
    return lax.dot_general(a, b, (((1,), (1,)), ((), ())), preferred_element_type=F32)


def _dot_tn(a, b):
    return lax.dot_general(a, b, (((0,), (0,)), ((), ())), preferred_element_type=F32)


def _dot_exact(a, b):
    return jnp.dot(a, b, preferred_element_type=F32, precision=lax.Precision.HIGHEST)


def _ffn_kernel(x_ref, nw_ref, wg_ref, wu_ref, wd_ref, fw_ref, o_ref, xn_ref, acc_ref, *, final_norm):
    j = pl.program_id(1)

    @pl.when(j == 0)
    def _():
        xn_ref[...] = _rmsnorm(x_ref[...], nw_ref[...]).astype(BF16)
        acc_ref[...] = jnp.zeros_like(acc_ref)

    xn = xn_ref[...]
    g = _dot(xn, wg_ref[...])
    u = _dot(xn, wu_ref[...])
    acc_ref[...] += _dot((_silu(g) * u).astype(BF16), wd_ref[...])

    @pl.when(j == pl.num_programs(1) - 1)
    def _():
        h = x_ref[...] + 0.5 * acc_ref[...]
        if final_norm:
            h = _rmsnorm(h, fw_ref[...])
        o_ref[...] = h


def _ffn(x, norm_w, w_gu, w_down, final_w, *, final_norm, tm=512, tf=512):
    m, d = x.shape
    f = w_down.shape[0]
    nf = f // tf
    return pl.pallas_call(
        functools.partial(_ffn_kernel, final_norm=final_norm),
        grid=(m // tm, nf),
        in_specs=[
            pl.BlockSpec((tm, d), lambda i, j: (i, 0)),
            pl.BlockSpec((1, d), lambda i, j: (0, 0)),
            pl.BlockSpec((d, tf), lambda i, j: (0, j)),
            pl.BlockSpec((d, tf), lambda i, j: (0, j + nf)),
            pl.BlockSpec((tf, d), lambda i, j: (j, 0)),
            pl.BlockSpec((1, d), lambda i, j: (0, 0)),
        ],
        out_specs=pl.BlockSpec((tm, d), lambda i, j: (i, 0)),
        out_shape=jax.ShapeDtypeStruct((m, d), F32),
        scratch_shapes=[pltpu.VMEM((tm, d), BF16), pltpu.VMEM((tm, d), F32)],
        compiler_params=_params(2),
        name="ffn",
    )(x, norm_w.reshape(1, d), w_gu, w_gu, w_down, final_w.reshape(1, d))


def _norm_matmul_kernel(x_ref, nw_ref, w_ref, o_ref, xn_ref):
    @pl.when(pl.program_id(1) == 0)
    def _():
        xn_ref[...] = _rmsnorm(x_ref[...], nw_ref[...]).astype(BF16)

    o_ref[...] = _dot(xn_ref[...], w_ref[...]).astype(o_ref.dtype)


def _norm_matmul(x, norm_w, w, *, out_dtype, tm, tn, name):
    m, k = x.shape
    n = w.shape[1]
    return pl.pallas_call(
        _norm_matmul_kernel,
        grid=(m // tm, n // tn),
        in_specs=[
            pl.BlockSpec((tm, k), lambda i, j: (i, 0)),
            pl.BlockSpec((1, k), lambda i, j: (0, 0)),
            pl.BlockSpec((k, tn), lambda i, j: (0, j)),
        ],
        out_specs=pl.BlockSpec((tm, tn), lambda i, j: (i, j)),
        out_shape=jax.ShapeDtypeStruct((m, n), out_dtype),
        scratch_shapes=[pltpu.VMEM((tm, k), BF16)],
        compiler_params=_params(2),
        name=name,
    )(x, norm_w.reshape(1, k), w)


def _matmul_residual_kernel(*refs, n_lhs):
    x_refs = refs[:n_lhs]
    w_refs = refs[n_lhs:2 * n_lhs]
    r_ref, o_ref = refs[2 * n_lhs], refs[2 * n_lhs + 1]
    acc = r_ref[...]
    for x_ref, w_ref in zip(x_refs, w_refs):
        acc = acc + _dot(x_ref[...], w_ref[...])
    o_ref[...] = acc


def _matmul_residual(xs, ws, residual, *, tm, tn, name):
    m, n = residual.shape
    in_specs = ([pl.BlockSpec((tm, x.shape[1]), lambda i, j: (i, 0)) for x in xs]
                + [pl.BlockSpec((w.shape[0], tn), lambda i, j: (0, j)) for w in ws]
                + [pl.BlockSpec((tm, tn), lambda i, j: (i, j))])
    return pl.pallas_call(
        functools.partial(_matmul_residual_kernel, n_lhs=len(xs)),
        grid=(m // tm, n // tn),
        in_specs=in_specs,
        out_specs=pl.BlockSpec((tm, tn), lambda i, j: (i, j)),
        out_shape=jax.ShapeDtypeStruct((m, n), F32),
        compiler_params=_params(2),
        name=name,
    )(*xs, *ws, residual)


def _sgu_kernel(u_ref, v_ref, vw_ref, ws_ref, bs_ref, o_ref):
    vn = _rmsnorm(_gelu_tanh(v_ref[...]), vw_ref[...]).astype(BF16)
    t_idx = lax.broadcasted_iota(jnp.int32, (CHUNK, CHUNK), 0)
    s_idx = lax.broadcasted_iota(jnp.int32, (CHUNK, CHUNK), 1)
    causal = s_idx <= t_idx
    bias = bs_ref[...]
    for g in range(GM_GROUPS):
        cols = slice(g * GM_GROUP_WIDTH, (g + 1) * GM_GROUP_WIDTH)
        w = jnp.where(causal, ws_ref[g], 0.0).astype(BF16)
        mixed = _dot(w, vn[:, cols]) + bias[:, g:g + 1]
        o_ref[:, cols] = (_gelu_tanh(u_ref[:, cols]) * mixed).astype(o_ref.dtype)


def _sgu(proj, v_norm_w, w_s, b_s_t):
    m = proj.shape[0]
    return pl.pallas_call(
        _sgu_kernel,
        grid=(m // CHUNK,),
        in_specs=[
            pl.BlockSpec((CHUNK, GM_WIDTH), lambda i: (i, 0)),
            pl.BlockSpec((CHUNK, GM_WIDTH), lambda i: (i, 1)),
            pl.BlockSpec((1, GM_WIDTH), lambda i: (0, 0)),
            pl.BlockSpec((GM_GROUPS, CHUNK, CHUNK), lambda i: (0, 0, 0)),
            pl.BlockSpec((CHUNK, LANES), lambda i: (0, 0)),
        ],
        out_specs=pl.BlockSpec((CHUNK, GM_WIDTH), lambda i: (i, 0)),
        out_shape=jax.ShapeDtypeStruct((m, GM_WIDTH), BF16),
        compiler_params=_params(1),
        name="sgu",
    )(proj, proj, v_norm_w.reshape(1, GM_WIDTH), w_s, b_s_t)


def _ssd_kernel(z_ref, xs_ref, bc_ref, dt_ref, cw_ref, cb_ref, dtb_ref, alog_ref, dskip_ref, nw_ref,
                o_ref, halo_ref, state_ref, y_ref):
    @pl.when(pl.program_id(0) == 0)
    def _():
        halo_ref[...] = jnp.zeros_like(halo_ref)
        state_ref[...] = jnp.zeros_like(state_ref)

    raw = jnp.concatenate([xs_ref[...], bc_ref[...]], axis=1)
    ext = jnp.concatenate([halo_ref[...], raw], axis=0)
    halo_ref[...] = raw[CHUNK - HALO_ROWS:, :]
    conv = cb_ref[...]
    for k in range(SSM_CONV):
        off = HALO_ROWS - (SSM_CONV - 1) + k
        conv = conv + ext[off:off + CHUNK, :] * cw_ref[k:k + 1, :]
    conv = _silu(conv)
    xs = conv[:, :SSM_WIDTH]
    bmat = conv[:, SSM_WIDTH:SSM_WIDTH + SSM_BC_WIDTH].astype(BF16)
    cmat = conv[:, SSM_WIDTH + SSM_BC_WIDTH:].astype(BF16)

    pre = dt_ref[...] + dtb_ref[...]
    dt = jnp.maximum(pre, 0.0) + jnp.log1p(jnp.exp(-jnp.abs(pre)))
    da = dt * (-jnp.exp(alog_ref[...]))
    row = lax.broadcasted_iota(jnp.int32, (CHUNK, CHUNK), 0)
    col = lax.broadcasted_iota(jnp.int32, (CHUNK, CHUNK), 1)
    causal = col <= row
    cum = _dot_exact(causal.astype(F32), da)
    cum_t = _dot_exact(da.T, (row <= col).astype(F32))
    cum_last = cum[CHUNK - 1:CHUNK, :]

    hrow = lax.broadcasted_iota(jnp.int32, (LANES, SSM_WIDTH), 0)
    hcol = lax.broadcasted_iota(jnp.int32, (LANES, SSM_WIDTH), 1)
    expand = (hcol // SSM_HEADDIM == hrow).astype(F32)
    dt_e = _dot_exact(dt, expand)
    exp_cum_e = _dot_exact(jnp.exp(cum), expand)
    decay_end_e = _dot_exact(jnp.exp(cum_last - cum), expand)
    chunk_decay_e = exp_cum_e[CHUNK - 1:CHUNK, :]

    xdt = xs * dt_e
    xdt_end = (xdt * decay_end_e).astype(BF16)
    lane_head = lax.broadcasted_iota(jnp.int32, (CHUNK, SSM_GROUP_WIDTH), 1) // SSM_HEADDIM

    for g in range(SSM_GROUPS):
        gcols = slice(g * SSM_GROUP_WIDTH, (g + 1) * SSM_GROUP_WIDTH)
        ncols = slice(g * SSM_STATE, (g + 1) * SSM_STATE)
        c_g = cmat[:, ncols]
        b_g = bmat[:, ncols]
        scores = _dot_nt(c_g, b_g)
        xdt_g = xdt[:, gcols]
        y = _dot(c_g, state_ref[g].astype(BF16)) * exp_cum_e[:, gcols]
        for r in range(SSM_HEADS_PER_GROUP):
            h = g * SSM_HEADS_PER_GROUP + r
            seg = cum[:, h:h + 1] - cum_t[h:h + 1, :]
            decay = jnp.exp(jnp.where(causal, seg, -jnp.inf))
            x_r = jnp.where(lane_head == r, xdt_g, 0.0).astype(BF16)
            y = y + _dot((scores * decay).astype(BF16), x_r)
        state_ref[g] = state_ref[g] * chunk_decay_e[:, gcols] + _dot_tn(b_g, xdt_end[:, gcols])
        y_ref[:, gcols] = y

    y = y_ref[...] + dskip_ref[...] * xs
    y = y * _silu(z_ref[...])
    o_ref[...] = _rmsnorm(y, nw_ref[...]).astype(o_ref.dtype)


def _ssd(proj, dt_raw, conv_w, conv_b, dt_bias, a_log, d_skip_e, norm_w):
    m = proj.shape[0]
    row = lambda a: a.reshape(1, -1)
    c0 = 2 * GM_WIDTH // SSM_WIDTH
    full = lambda shape: pl.BlockSpec(shape, lambda i: (0,) * len(shape))
    return pl.pallas_call(
        _ssd_kernel,
        grid=(m // CHUNK,),
        in_specs=[
            pl.BlockSpec((CHUNK, SSM_WIDTH), lambda i: (i, c0)),
            pl.BlockSpec((CHUNK, SSM_WIDTH), lambda i: (i, c0 + 1)),
            pl.BlockSpec((CHUNK, 2 * SSM_BC_WIDTH), lambda i: (i, c0 + 2)),
            pl.BlockSpec((CHUNK, LANES), lambda i: (i, 0)),
            full((SSM_CONV, SSM_WIDTH + 2 * SSM_BC_WIDTH)),
            full((1, SSM_WIDTH + 2 * SSM_BC_WIDTH)),
            full((1, LANES)),
            full((1, LANES)),
            full((1, SSM_WIDTH)),
            full((1, SSM_WIDTH)),
        ],
        out_specs=pl.BlockSpec((CHUNK, SSM_WIDTH), lambda i: (i, 0)),
        out_shape=jax.ShapeDtypeStruct((m, SSM_WIDTH), BF16),
        scratch_shapes=[pltpu.VMEM((HALO_ROWS, SSM_WIDTH + 2 * SSM_BC_WIDTH), F32),
                        pltpu.VMEM((SSM_GROUPS, SSM_STATE, SSM_GROUP_WIDTH), F32),
                        pltpu.VMEM((CHUNK, SSM_WIDTH), F32)],
        compiler_params=_params(1),
        name="ssd",
    )(proj, proj, proj, dt_raw, conv_w, row(conv_b), row(dt_bias), row(a_log), row(d_skip_e), row(norm_w))


def _attn_kernel(q_ref, k_ref, v_ref, o_ref):
    scale = XA_HEADDIM ** -0.5
    for h in range(XA_HEADS):
        cols = slice(h * XA_HEADDIM, (h + 1) * XA_HEADDIM)
        logits = _dot_nt(q_ref[:, cols], k_ref[:, cols]) * scale
        p = jnp.exp(logits - jnp.max(logits, axis=-1, keepdims=True))
        p = p / jnp.sum(p, axis=-1, keepdims=True)
        o_ref[:, cols] = _dot(p.astype(BF16), v_ref[:, cols]).astype(o_ref.dtype)


def _attention(q, kv, *, tm=512):
    m = q.shape[0]
    return pl.pallas_call(
        _attn_kernel,
        grid=(m // tm,),
        in_specs=[
            pl.BlockSpec((tm, D_MODEL), lambda i: (i, 0)),
            pl.BlockSpec((MEM_LEN, D_MODEL), lambda i: (0, 0)),
            pl.BlockSpec((MEM_LEN, D_MODEL), lambda i: (0, 1)),
        ],
        out_specs=pl.BlockSpec((tm, D_MODEL), lambda i: (i, 0)),
        out_shape=jax.ShapeDtypeStruct((m, D_MODEL), BF16),
        compiler_params=_params(1),
        name="xattn",
    )(q, kv, kv)


def kernel(x, mem, ffn1_norm, ffn1_w_gu, ffn1_w_down, mix_norm, w_in, gm_v_norm, gm_w_s, gm_b_s, ssm_conv_w, ssm_conv_b, ssm_dt_bias, ssm_a_log, ssm_d, ssm_norm, w_out, xa_norm, mem_norm, xa_w_q, xa_w_kv, xa_w_o, ffn2_norm, ffn2_w_gu, ffn2_w_down, final_norm):
    bf = lambda w: w.astype(BF16)
    pad_lanes = lambda a: jnp.pad(a, ((0, 0), (0, LANES - a.shape[1])))
    h = x[0]
    for i in range(ffn1_norm.shape[0]):
        h = _ffn(h, ffn1_norm[i], bf(ffn1_w_gu[i]), bf(ffn1_w_down[i]), final_norm, final_norm=False)

        w_in_i = bf(w_in[i])
        proj = _norm_matmul(h, mix_norm[i], w_in_i[:, :PROJ_MAIN], out_dtype=F32, tm=1024, tn=512, name="proj")
        dt_raw = _norm_matmul(h, mix_norm[i], pad_lanes(w_in_i[:, PROJ_MAIN:]), out_dtype=F32,
                              tm=1024, tn=LANES, name="proj_dt")
        a_out = _sgu(proj, gm_v_norm[i], gm_w_s[i], pad_lanes(gm_b_s[i].T))
        m_out = _ssd(proj, dt_raw, ssm_conv_w[i], ssm_conv_b[i],
                     pad_lanes(ssm_dt_bias[i][None])[0], pad_lanes(ssm_a_log[i][None])[0],
                     jnp.repeat(ssm_d[i], SSM_HEADDIM), ssm_norm[i])
        w_out_i = bf(w_out[i])
        h = _matmul_residual([a_out, m_out], [w_out_i[:GM_WIDTH], w_out_i[GM_WIDTH:]], h,
                             tm=1024, tn=512, name="mix_out")

        q = _norm_matmul(h, xa_norm[i], bf(xa_w_q[i]), out_dtype=BF16, tm=1024, tn=512, name="xa_q")
        kv = _norm_matmul(mem[0], mem_norm[i], bf(xa_w_kv[i]), out_dtype=BF16, tm=MEM_LEN, tn=1024, name="xa_kv")
        o = _attention(q, kv)
        h = _matmul_residual([o], [bf(xa_w_o[i])], h, tm=1024, tn=512, name="xa_out")

        last = i == ffn1_norm.shape[0] - 1
        h = _ffn(h, ffn2_norm[i], bf(ffn2_w_gu[i]), bf(ffn2_w_down[i]), final_norm, final_norm=last)
    return h[None]
```

```python
import functools

import jax
import jax.numpy as jnp
from jax import lax
from jax.experimental import pallas as pl
from jax.experimental.pallas import tpu as pltpu

D_MODEL = 2048
SEQ = 8192
MEM_LEN = 256
GM_WIDTH = 2048
GM_GROUPS = 4
GM_GROUP_WIDTH = GM_WIDTH // GM_GROUPS
CHUNK = 128
SSM_WIDTH = 2048
SSM_HEADDIM = 64
SSM_HEADS = SSM_WIDTH // SSM_HEADDIM
SSM_GROUPS = 8
SSM_HEADS_PER_GROUP = SSM_HEADS // SSM_GROUPS
SSM_GROUP_WIDTH = SSM_HEADS_PER_GROUP * SSM_HEADDIM
SSM_STATE = 128
SSM_CONV = 4
SSM_BC_WIDTH = SSM_GROUPS * SSM_STATE
PROJ_MAIN = 2 * GM_WIDTH + SSM_WIDTH + SSM_WIDTH + 2 * SSM_BC_WIDTH
XA_HEADS = 4
XA_HEADDIM = D_MODEL // XA_HEADS
D_FF = 5632
EPS = 1e-6

LANES = 128
HALO_ROWS = 8
VMEM_LIMIT = 56 * 1024 * 1024

BF16 = jnp.bfloat16
F32 = jnp.float32


def _params(n_axes):
    return pltpu.CompilerParams(dimension_semantics=("arbitrary",) * n_axes,
                                vmem_limit_bytes=VMEM_LIMIT)


def _rmsnorm(x, w):
    return x * lax.rsqrt(jnp.mean(x * x, axis=-1, keepdims=True) + EPS) * w


def _sigmoid(x):
    return 1.0 / (1.0 + jnp.exp(-x))


def _silu(x):
    return x * _sigmoid(x)


def _gelu_tanh(x):
    return 0.5 * x * (1.0 + jnp.tanh(0.7978845608028654 * (x + 0.044715 * (x * x * x))))


def _dot(a, b):
    return jnp.dot(a, b, preferred_element_type=F32)


def _dot_nt(a, b):
    return lax.dot_general(a, b, (((1,), (1,)), ((), ())), preferred_element_type=F32)


def _dot_tn(a, b):
    return lax.dot_general(a, b, (((0,), (0,)), ((), ())), preferred_element_type=F32)


def _dot_exact(a, b):
    return jnp.dot(a, b, preferred_element_type=F32, precision=lax.Precision.HIGHEST)


def _ffn_kernel(x_ref, nw_ref, wg_ref, wu_ref, wd_ref, fw_ref, o_ref, xn_ref, acc_ref, *, final_norm):
    j = pl.program_id(1)

    @pl.when(j == 0)
    def _():
        xn_ref[...] = _rmsnorm(x_ref[...], nw_ref[...]).astype(BF16)
        acc_ref[...] = jnp.zeros_like(acc_ref)

    xn = xn_ref[...]
    g = _dot(xn, wg_ref[...])
    u = _dot(xn, wu_ref[...])
    acc_ref[...] += _dot((_silu(g) * u).astype(BF16), wd_ref[...])

    @pl.when(j == pl.num_programs(1) - 1)
    def _():
        h = x_ref[...] + 0.5 * acc_ref[...]
        if final_norm:
            h = _rmsnorm(h, fw_ref[...])
        o_ref[...] = h


def _ffn(x, norm_w, w_gu, w_down, final_w, *, final_norm, tm=512, tf=512):
    m, d = x.shape
    f = w_down.shape[0]
    nf = f // tf
    return pl.pallas_call(
        functools.partial(_ffn_kernel, final_norm=final_norm),
        grid=(m // tm, nf),
        in_specs=[
            pl.BlockSpec((tm, d), lambda i, j: (i, 0)),
            pl.BlockSpec((1, d), lambda i, j: (0, 0)),
            pl.BlockSpec((d, tf), lambda i, j: (0, j)),
            pl.BlockSpec((d, tf), lambda i, j: (0, j + nf)),
            pl.BlockSpec((tf, d), lambda i, j: (j, 0)),
            pl.BlockSpec((1, d), lambda i, j: (0, 0)),
        ],
        out_specs=pl.BlockSpec((tm, d), lambda i, j: (i, 0)),
        out_shape=jax.ShapeDtypeStruct((m, d), F32),
        scratch_shapes=[pltpu.VMEM((tm, d), BF16), pltpu.VMEM((tm, d), F32)],
        compiler_params=_params(2),
        name="ffn",
    )(x, norm_w.reshape(1, d), w_gu, w_gu, w_down, final_w.reshape(1, d))


def _norm_matmul_kernel(x_ref, nw_ref, w_ref, o_ref, xn_ref):
    @pl.when(pl.program_id(1) == 0)
    def _():
        xn_ref[...] = _rmsnorm(x_ref[...], nw_ref[...]).astype(BF16)

    o_ref[...] = _dot(xn_ref[...], w_ref[...]).astype(o_ref.dtype)


def _norm_matmul(x, norm_w, w, *, out_dtype, tm, tn, name):
    m, k = x.shape
    n = w.shape[1]
    return pl.pallas_call(
        _norm_matmul_kernel,
        grid=(m // tm, n // tn),
        in_specs=[
            pl.BlockSpec((tm, k), lambda i, j: (i, 0)),
            pl.BlockSpec((1, k), lambda i, j: (0, 0)),
            pl.BlockSpec((k, tn), lambda i, j: (0, j)),
        ],
        out_specs=pl.BlockSpec((tm, tn), lambda i, j: (i, j)),
        out_shape=jax.ShapeDtypeStruct((m, n), out_dtype),
        scratch_shapes=[pltpu.VMEM((tm, k), BF16)],
        compiler_params=_params(2),
        name=name,
    )(x, norm_w.reshape(1, k), w)


def _matmul_residual_kernel(*refs, n_lhs):
    x_refs = refs[:n_lhs]
    w_refs = refs[n_lhs:2 * n_lhs]
    r_ref, o_ref = refs[2 * n_lhs], refs[2 * n_lhs + 1]
    acc = r_ref[...]
    for x_ref, w_ref in zip(x_refs, w_refs):
        acc = acc + _dot(x_ref[...], w_ref[...])
    o_ref[...] = acc


def _matmul_residual(xs, ws, residual, *, tm, tn, name):
    m, n = residual.shape
    in_specs = ([pl.BlockSpec((tm, x.shape[1]), lambda i, j: (i, 0)) for x in xs]
                + [pl.BlockSpec((w.shape[0], tn), lambda i, j: (0, j)) for w in ws]
                + [pl.BlockSpec((tm, tn), lambda i, j: (i, j))])
    return pl.pallas_call(
        functools.partial(_matmul_residual_kernel, n_lhs=len(xs)),
        grid=(m // tm, n // tn),
        in_specs=in_specs,
        out_specs=pl.BlockSpec((tm, tn), lambda i, j: (i, j)),
        out_shape=jax.ShapeDtypeStruct((m, n), F32),
        compiler_params=_params(2),
        name=name,
    )(*xs, *ws, residual)


def _sgu_kernel(u_ref, v_ref, vw_ref, ws_ref, bs_ref, o_ref):
    vn = _rmsnorm(_gelu_tanh(v_ref[...]), vw_ref[...]).astype(BF16)
    t_idx = lax.broadcasted_iota(jnp.int32, (CHUNK, CHUNK), 0)
    s_idx = lax.broadcasted_iota(jnp.int32, (CHUNK, CHUNK), 1)
    causal = s_idx <= t_idx
    bias = bs_ref[...]
    for g in range(GM_GROUPS):
        cols = slice(g * GM_GROUP_WIDTH, (g + 1) * GM_GROUP_WIDTH)
        w = jnp.where(causal, ws_ref[g], 0.0).astype(BF16)
        mixed = _dot(w, vn[:, cols]) + bias[:, g:g + 1]
        o_ref[:, cols] = (_gelu_tanh(u_ref[:, cols]) * mixed).astype(o_ref.dtype)


def _sgu(proj, v_norm_w, w_s, b_s_t):
    m = proj.shape[0]
    return pl.pallas_call(
        _sgu_kernel,
        grid=(m // CHUNK,),
        in_specs=[
            pl.BlockSpec((CHUNK, GM_WIDTH), lambda i: (i, 0)),
            pl.BlockSpec((CHUNK, GM_WIDTH), lambda i: (i, 1)),
            pl.BlockSpec((1, GM_WIDTH), lambda i: (0, 0)),
            pl.BlockSpec((GM_GROUPS, CHUNK, CHUNK), lambda i: (0, 0, 0)),
            pl.BlockSpec((CHUNK, LANES), lambda i: (0, 0)),
        ],
        out_specs=pl.BlockSpec((CHUNK, GM_WIDTH), lambda i: (i, 0)),
        out_shape=jax.ShapeDtypeStruct((m, GM_WIDTH), BF16),
        compiler_params=_params(1),
        name="sgu",
    )(proj, proj, v_norm_w.reshape(1, GM_WIDTH), w_s, b_s_t)


def _split3(v):
    hi = v.astype(BF16).astype(F32)
    rest = v - hi
    mid = rest.astype(BF16).astype(F32)
    return hi, mid, rest - mid


def _pack3(v):
    hi, mid, lo = _split3(v)
    lane = lax.broadcasted_iota(jnp.int32, v.shape, 1)
    packed = jnp.where(lane < SSM_HEADS, hi,
                       jnp.where(lane < 2 * SSM_HEADS, pltpu.roll(mid, SSM_HEADS, 1),
                                 jnp.where(lane < 3 * SSM_HEADS, pltpu.roll(lo, 2 * SSM_HEADS, 1), 0.0)))
    return packed.astype(BF16)


def _ssd_kernel(z_ref, xs_ref, bc_ref, dt_ref, cw_ref, cb_ref, dtb_ref, alog_ref, dskip_ref, nw_ref,
                o_ref, ext_ref, state_ref, y_ref, xc_ref, bmat_ref, cmat_ref, expand_ref):
    conv_ch = SSM_WIDTH + 2 * SSM_BC_WIDTH

    @pl.when(pl.program_id(0) == 0)
    def _():
        ext_ref[0:HALO_ROWS, :] = jnp.zeros((HALO_ROWS, conv_ch), F32)
        state_ref[...] = jnp.zeros_like(state_ref)
        k_idx = lax.broadcasted_iota(jnp.int32, (LANES, SSM_WIDTH), 0)
        j_idx = lax.broadcasted_iota(jnp.int32, (LANES, SSM_WIDTH), 1)
        hit = (k_idx < 3 * SSM_HEADS) & (j_idx // SSM_HEADDIM == k_idx % SSM_HEADS)
        expand_ref[...] = hit.astype(F32).astype(BF16)

    ext_ref[HALO_ROWS:, 0:SSM_WIDTH] = xs_ref[...]
    ext_ref[HALO_ROWS:, SSM_WIDTH:] = bc_ref[...]

    def conv_silu(c0, c1):
        acc = cb_ref[:, c0:c1]
        for k in range(SSM_CONV):
            off = HALO_ROWS - (SSM_CONV - 1) + k
            acc = acc + ext_ref[off:off + CHUNK, c0:c1] * cw_ref[k:k + 1, c0:c1]
        return _silu(acc)

    blk = 512
    for c0 in range(0, SSM_WIDTH, blk):
        xc_ref[:, c0:c0 + blk] = conv_silu(c0, c0 + blk)
    for c0 in range(0, SSM_BC_WIDTH, blk):
        bmat_ref[:, c0:c0 + blk] = conv_silu(SSM_WIDTH + c0, SSM_WIDTH + c0 + blk).astype(BF16)
        cmat_ref[:, c0:c0 + blk] = conv_silu(SSM_WIDTH + SSM_BC_WIDTH + c0,
                                             SSM_WIDTH + SSM_BC_WIDTH + c0 + blk).astype(BF16)
    ext_ref[0:HALO_ROWS, :] = ext_ref[CHUNK:CHUNK + HALO_ROWS, :]

    pre = dt_ref[...] + dtb_ref[...]
    dt = jnp.maximum(pre, 0.0) + jnp.log1p(jnp.exp(-jnp.abs(pre)))
    da = dt * (-jnp.exp(alog_ref[...]))
    row = lax.broadcasted_iota(jnp.int32, (CHUNK, CHUNK), 0)
    col = lax.broadcasted_iota(jnp.int32, (CHUNK, CHUNK), 1)
    causal = col <= row
    tri = causal.astype(F32).astype(BF16)
    da_hi, da_mid, da_lo = _split3(da)
    parts = _dot(tri, jnp.concatenate([da_hi, da_mid, da_lo], axis=1).astype(BF16))
    cum = parts[:, :LANES] + parts[:, LANES:2 * LANES] + parts[:, 2 * LANES:]
    cum_t = cum.T
    cum_last = cum[CHUNK - 1:CHUNK, :]

    packed = jnp.concatenate([_pack3(dt), _pack3(dt * jnp.exp(cum_last - cum)), _pack3(jnp.exp(cum))], axis=0)
    expanded = _dot(packed, expand_ref[...])
    dt_e = expanded[0:CHUNK]
    dt_decay_end_e = expanded[CHUNK:2 * CHUNK]
    exp_cum_e = expanded[2 * CHUNK:]
    chunk_decay_e = exp_cum_e[CHUNK - 1:CHUNK, :]

    xs = xc_ref[...]
    xdt = (xs * dt_e).astype(BF16)
    xdt_end = (xs * dt_decay_end_e).astype(BF16)
    lane_head = lax.broadcasted_iota(jnp.int32, (1, SSM_GROUP_WIDTH), 1) // SSM_HEADDIM

    for g in range(SSM_GROUPS):
        gcols = slice(g * SSM_GROUP_WIDTH, (g + 1) * SSM_GROUP_WIDTH)
        ncols = slice(g * SSM_STATE, (g + 1) * SSM_STATE)
        c_g = cmat_ref[:, ncols]
        b_g = bmat_ref[:, ncols]
        scores = _dot_nt(c_g, b_g)
        xdt_g = xdt[:, gcols]
        y = _dot(c_g, state_ref[g].astype(BF16)) * exp_cum_e[:, gcols]
        for r in range(SSM_HEADS_PER_GROUP):
            h = g * SSM_HEADS_PER_GROUP + r
            seg = cum[:, h:h + 1] - cum_t[h:h + 1, :]
            decay = jnp.exp(jnp.where(causal, seg, -jnp.inf))
            x_r = xdt_g * (lane_head == r).astype(F32).astype(BF16)
            y = y + _dot((scores * decay).astype(BF16), x_r)
        state_ref[g] = state_ref[g] * chunk_decay_e[:, gcols] + _dot_tn(b_g, xdt_end[:, gcols])
        y_ref[:, gcols] = y

    y = y_ref[...] + dskip_ref[...] * xs
    y = y * _silu(z_ref[...])
    o_ref[...] = _rmsnorm(y, nw_ref[...]).astype(o_ref.dtype)


def _ssd(proj, dt_raw, conv_w, conv_b, dt_bias, a_log, d_skip_e, norm_w):
    m = proj.shape[0]
    row = lambda a: a.reshape(1, -1)
    c0 = 2 * GM_WIDTH // SSM_WIDTH
    full = lambda shape: pl.BlockSpec(shape, lambda i: (0,) * len(shape))
    return pl.pallas_call(
        _ssd_kernel,
        grid=(m // CHUNK,),
        in_specs=[
            pl.BlockSpec((CHUNK, SSM_WIDTH), lambda i: (i, c0)),
            pl.BlockSpec((CHUNK, SSM_WIDTH), lambda i: (i, c0 + 1)),
            pl.BlockSpec((CHUNK, 2 * SSM_BC_WIDTH), lambda i: (i, c0 + 2)),
            pl.BlockSpec((CHUNK, LANES), lambda i: (i, 0)),
            full((SSM_CONV, SSM_WIDTH + 2 * SSM_BC_WIDTH)),
            full((1, SSM_WIDTH + 2 * SSM_BC_WIDTH)),
            full((1, LANES)),
            full((1, LANES)),
            full((1, SSM_WIDTH)),
            full((1, SSM_WIDTH)),
        ],
        out_specs=pl.BlockSpec((CHUNK, SSM_WIDTH), lambda i: (i, 0)),
        out_shape=jax.ShapeDtypeStruct((m, SSM_WIDTH), BF16),
        scratch_shapes=[pltpu.VMEM((HALO_ROWS + CHUNK, SSM_WIDTH + 2 * SSM_BC_WIDTH), F32),
                        pltpu.VMEM((SSM_GROUPS, SSM_STATE, SSM_GROUP_WIDTH), F32),
                        pltpu.VMEM((CHUNK, SSM_WIDTH), F32),
                        pltpu.VMEM((CHUNK, SSM_WIDTH), F32),
                        pltpu.VMEM((CHUNK, SSM_BC_WIDTH), BF16),
                        pltpu.VMEM((CHUNK, SSM_BC_WIDTH), BF16),
                        pltpu.VMEM((LANES, SSM_WIDTH), BF16)],
        compiler_params=_params(1),
        name="ssd",
    )(proj, proj, proj, dt_raw, conv_w, row(conv_b), row(dt_bias), row(a_log), row(d_skip_e), row(norm_w))


def _attn_kernel(q_ref, k_ref, v_ref, o_ref):
    scale = XA_HEADDIM ** -0.5
    for h in range(XA_HEADS):
        cols = slice(h * XA_HEADDIM, (h + 1) * XA_HEADDIM)
        logits = _dot_nt(q_ref[:, cols], k_ref[:, cols]) * scale
        p = jnp.exp(logits - jnp.max(logits, axis=-1, keepdims=True))
        p = p / jnp.sum(p, axis=-1, keepdims=True)
        o_ref[:, cols] = _dot(p.astype(BF16), v_ref[:, cols]).astype(o_ref.dtype)


def _attention(q, kv, *, tm=512):
    m = q.shape[0]
    return pl.pallas_call(
        _attn_kernel,
        grid=(m // tm,),
        in_specs=[
            pl.BlockSpec((tm, D_MODEL), lambda i: (i, 0)),
            pl.BlockSpec((MEM_LEN, D_MODEL), lambda i: (0, 0)),
            pl.BlockSpec((MEM_LEN, D_MODEL), lambda i: (0, 1)),
        ],
        out_specs=pl.BlockSpec((tm, D_MODEL), lambda i: (i, 0)),
        out_shape=jax.ShapeDtypeStruct((m, D_MODEL), BF16),
        compiler_params=_params(1),
        name="xattn",
    )(q, kv, kv)


def kernel(x, mem, ffn1_norm, ffn1_w_gu, ffn1_w_down, mix_norm, w_in, gm_v_norm, gm_w_s, gm_b_s, ssm_conv_w, ssm_conv_b, ssm_dt_bias, ssm_a_log, ssm_d, ssm_norm, w_out, xa_norm, mem_norm, xa_w_q, xa_w_kv, xa_w_o, ffn2_norm, ffn2_w_gu, ffn2_w_down, final_norm):
    bf = lambda w: w.astype(BF16)
    pad_lanes = lambda a: jnp.pad(a, ((0, 0), (0, LANES - a.shape[1])))
    h = x[0]
    for i in range(ffn1_norm.shape[0]):
        h = _ffn(h, ffn1_norm[i], bf(ffn1_w_gu[i]), bf(ffn1_w_down[i]), final_norm, final_norm=False)

        w_in_i = bf(w_in[i])
        proj = _norm_matmul(h, mix_norm[i], w_in_i[:, :PROJ_MAIN], out_dtype=F32, tm=1024, tn=512, name="proj")
        dt_raw = _norm_matmul(h, mix_norm[i], pad_lanes(w_in_i[:, PROJ_MAIN:]), out_dtype=F32,
                              tm=1024, tn=LANES, name="proj_dt")
        a_out = _sgu(proj, gm_v_norm[i], gm_w_s[i], pad_lanes(gm_b_s[i].T))
        m_out = _ssd(proj, dt_raw, ssm_conv_w[i], ssm_conv_b[i],
                     pad_lanes(ssm_dt_bias[i][None])[0], pad_lanes(ssm_a_log[i][None])[0],
                     jnp.repeat(ssm_d[i], SSM_HEADDIM), ssm_norm[i])
        w_out_i = bf(w_out[i])
        h = _matmul_residual([a_out, m_out], [w_out_i[:GM_WIDTH], w_out_i[GM_WIDTH:]], h,
                             tm=1024, tn=512, name="mix_out")

        q = _norm_matmul(h, xa_norm[i], bf(xa_w_q[i]), out_dtype=BF16, tm=1024, tn=512, name="xa_q")
        kv = _norm_matmul(mem[0], mem_norm[i], bf(xa_w_kv[i]), out_dtype=BF16, tm=MEM_LEN, tn=1024, name="xa_kv")
        o = _attention(q, kv)
        h = _matmul_residual([o], [bf(xa_w_o[i])], h, tm=1024, tn=512, name="xa_out")

        last = i == ffn1_norm.shape[0] - 1
        h = _ffn(h, ffn2_norm[i], bf(ffn2_w_gu[i]), bf(ffn2_w_down[i]), final_norm, final_norm=last)
    return h[None]
```

```python
import functools

import jax
import jax.numpy as jnp
from jax import lax
from jax.experimental import pallas as pl
from jax.experimental.pallas import tpu as pltpu

D_MODEL = 2048
SEQ = 8192
MEM_LEN = 256
GM_WIDTH = 2048
GM_GROUPS = 4
GM_GROUP_WIDTH = GM_WIDTH // GM_GROUPS
CHUNK = 128
SSM_WIDTH = 2048
SSM_HEADDIM = 64
SSM_HEADS = SSM_WIDTH // SSM_HEADDIM
SSM_GROUPS = 8
SSM_HEADS_PER_GROUP = SSM_HEADS // SSM_GROUPS
SSM_GROUP_WIDTH = SSM_HEADS_PER_GROUP * SSM_HEADDIM
SSM_STATE = 128
SSM_CONV = 4
SSM_BC_WIDTH = SSM_GROUPS * SSM_STATE
PROJ_MAIN = 2 * GM_WIDTH + SSM_WIDTH + SSM_WIDTH + 2 * SSM_BC_WIDTH
XA_HEADS = 4
XA_HEADDIM = D_MODEL // XA_HEADS
D_FF = 5632
EPS = 1e-6

LANES = 128
HALO_ROWS = 8
VMEM_LIMIT = 60 * 1024 * 1024

BF16 = jnp.bfloat16
F32 = jnp.float32


def _params(n_axes):
    return pltpu.CompilerParams(dimension_semantics=("arbitrary",) * n_axes,
                                vmem_limit_bytes=VMEM_LIMIT)


def _rmsnorm(x, w):
    return x * lax.rsqrt(jnp.mean(x * x, axis=-1, keepdims=True) + EPS) * w


def _sigmoid(x):
    return 1.0 / (1.0 + jnp.exp(-x))


def _silu(x):
    return x * _sigmoid(x)


def _gelu_tanh(x):
    return 0.5 * x * (1.0 + jnp.tanh(0.7978845608028654 * (x + 0.044715 * (x * x * x))))


def _dot(a, b):
    return jnp.dot(a, b, preferred_element_type=F32)


def _dot_nt(a, b):
    return lax.dot_general(a, b, (((1,), (1,)), ((), ())), preferred_element_type=F32)


def _dot_tn(a, b):
    return lax.dot_general(a, b, (((0,), (0,)), ((), ())), preferred_element_type=F32)


def _dot_exact(a, b):
    return jnp.dot(a, b, preferred_element_type=F32, precision=lax.Precision.HIGHEST)


def _ffn_kernel(x_ref, nw_ref, wg_ref, wu_ref, wd_ref, fw_ref, o_ref, xn_ref, *, final_norm):
    j = pl.program_id(1)

    @pl.when(j == 0)
    def _():
        xn_ref[...] = _rmsnorm(x_ref[...], nw_ref[...]).astype(BF16)
        o_ref[...] = jnp.zeros_like(o_ref)

    xn = xn_ref[...]
    g = _dot(xn, wg_ref[...].astype(BF16))
    u = _dot(xn, wu_ref[...].astype(BF16))
    o_ref[...] += _dot((_silu(g) * u).astype(BF16), wd_ref[...].astype(BF16))

    @pl.when(j == pl.num_programs(1) - 1)
    def _():
        h = x_ref[...] + 0.5 * o_ref[...]
        if final_norm:
            h = _rmsnorm(h, fw_ref[...])
        o_ref[...] = h


def _ffn(x, norm_w, w_gu, w_down, final_w, *, final_norm, tm=1024, tf=256):
    m, d = x.shape
    f = w_down.shape[0]
    nf = f // tf
    return pl.pallas_call(
        functools.partial(_ffn_kernel, final_norm=final_norm),
        grid=(m // tm, nf),
        in_specs=[
            pl.BlockSpec((tm, d), lambda i, j: (i, 0)),
            pl.BlockSpec((1, d), lambda i, j: (0, 0)),
            pl.BlockSpec((d, tf), lambda i, j: (0, j)),
            pl.BlockSpec((d, tf), lambda i, j: (0, j + nf)),
            pl.BlockSpec((tf, d), lambda i, j: (j, 0)),
            pl.BlockSpec((1, d), lambda i, j: (0, 0)),
        ],
        out_specs=pl.BlockSpec((tm, d), lambda i, j: (i, 0)),
        out_shape=jax.ShapeDtypeStruct((m, d), F32),
        scratch_shapes=[pltpu.VMEM((tm, d), BF16)],
        compiler_params=_params(2),
        name="ffn",
    )(x, norm_w.reshape(1, d), w_gu, w_gu, w_down, final_w.reshape(1, d))


def _in_proj_kernel(x_ref, nw_ref, w_ref, wdt_ref, o_ref, dt_ref, xn_ref):
    @pl.when(pl.program_id(1) == 0)
    def _():
        xn = _rmsnorm(x_ref[...], nw_ref[...]).astype(BF16)
        xn_ref[...] = xn
        dt_ref[...] = _dot(xn, wdt_ref[...].astype(BF16))

    o_ref[...] = _dot(xn_ref[...], w_ref[...].astype(BF16))


def _in_proj(x, norm_w, w_in, *, tm=1024, tn=512):
    m, k = x.shape
    w_dt = jnp.pad(w_in[:, PROJ_MAIN:], ((0, 0), (0, LANES - (w_in.shape[1] - PROJ_MAIN))))
    return pl.pallas_call(
        _in_proj_kernel,
        grid=(m // tm, PROJ_MAIN // tn),
        in_specs=[
            pl.BlockSpec((tm, k), lambda i, j: (i, 0)),
            pl.BlockSpec((1, k), lambda i, j: (0, 0)),
            pl.BlockSpec((k, tn), lambda i, j: (0, j)),
            pl.BlockSpec((k, LANES), lambda i, j: (0, 0)),
        ],
        out_specs=[pl.BlockSpec((tm, tn), lambda i, j: (i, j)),
                   pl.BlockSpec((tm, LANES), lambda i, j: (i, 0))],
        out_shape=[jax.ShapeDtypeStruct((m, PROJ_MAIN), F32), jax.ShapeDtypeStruct((m, LANES), F32)],
        scratch_shapes=[pltpu.VMEM((tm, k), BF16)],
        compiler_params=_params(2),
        name="in_proj",
    )(x, norm_w.reshape(1, k), w_in, w_dt)


def _norm_matmul_kernel(x_ref, nw_ref, w_ref, o_ref, xn_ref):
    @pl.when(pl.program_id(1) == 0)
    def _():
        xn_ref[...] = _rmsnorm(x_ref[...], nw_ref[...]).astype(BF16)

    o_ref[...] = _dot(xn_ref[...], w_ref[...]).astype(o_ref.dtype)


def _norm_matmul(x, norm_w, w, *, out_dtype, tm, tn, name):
    m, k = x.shape
    n = w.shape[1]
    return pl.pallas_call(
        _norm_matmul_kernel,
        grid=(m // tm, n // tn),
        in_specs=[
            pl.BlockSpec((tm, k), lambda i, j: (i, 0)),
            pl.BlockSpec((1, k), lambda i, j: (0, 0)),
            pl.BlockSpec((k, tn), lambda i, j: (0, j)),
        ],
        out_specs=pl.BlockSpec((tm, tn), lambda i, j: (i, j)),
        out_shape=jax.ShapeDtypeStruct((m, n), out_dtype),
        scratch_shapes=[pltpu.VMEM((tm, k), BF16)],
        compiler_params=_params(2),
        name=name,
    )(x, norm_w.reshape(1, k), w)


def _matmul_residual_kernel(*refs, n_lhs):
    x_refs = refs[:n_lhs]
    w_refs = refs[n_lhs:2 * n_lhs]
    r_ref, o_ref = refs[2 * n_lhs], refs[2 * n_lhs + 1]
    acc = r_ref[...]
    for x_ref, w_ref in zip(x_refs, w_refs):
        acc = acc + _dot(x_ref[...], w_ref[...])
    o_ref[...] = acc


def _matmul_residual(xs, w, residual, *, tm, tn, name):
    m, n = residual.shape
    k = xs[0].shape[1]
    in_specs = ([pl.BlockSpec((tm, k), lambda i, j: (i, 0)) for _ in xs]
                + [pl.BlockSpec((k, tn), functools.partial(lambda i, j, r: (r, j), r=r)) for r in range(len(xs))]
                + [pl.BlockSpec((tm, tn), lambda i, j: (i, j))])
    return pl.pallas_call(
        functools.partial(_matmul_residual_kernel, n_lhs=len(xs)),
        grid=(m // tm, n // tn),
        in_specs=in_specs,
        out_specs=pl.BlockSpec((tm, tn), lambda i, j: (i, j)),
        out_shape=jax.ShapeDtypeStruct((m, n), F32),
        compiler_params=_params(2),
        name=name,
    )(*xs, *([w] * len(xs)), residual)


def _sgu_kernel(u_ref, v_ref, vw_ref, ws_ref, bs_ref, o_ref):
    vn = _rmsnorm(_gelu_tanh(v_ref[...]), vw_ref[...]).astype(BF16)
    t_idx = lax.broadcasted_iota(jnp.int32, (CHUNK, CHUNK), 0)
    s_idx = lax.broadcasted_iota(jnp.int32, (CHUNK, CHUNK), 1)
    causal = s_idx <= t_idx
    bias = bs_ref[...]
    for g in range(GM_GROUPS):
        cols = slice(g * GM_GROUP_WIDTH, (g + 1) * GM_GROUP_WIDTH)
        w = jnp.where(causal, ws_ref[g], 0.0).astype(BF16)
        mixed = _dot(w, vn[:, cols]) + bias[:, g:g + 1]
        o_ref[:, cols] = (_gelu_tanh(u_ref[:, cols]) * mixed).astype(o_ref.dtype)


def _sgu(proj, v_norm_w, w_s, b_s_t):
    m = proj.shape[0]
    return pl.pallas_call(
        _sgu_kernel,
        grid=(m // CHUNK,),
        in_specs=[
            pl.BlockSpec((CHUNK, GM_WIDTH), lambda i: (i, 0)),
            pl.BlockSpec((CHUNK, GM_WIDTH), lambda i: (i, 1)),
            pl.BlockSpec((1, GM_WIDTH), lambda i: (0, 0)),
            pl.BlockSpec((GM_GROUPS, CHUNK, CHUNK), lambda i: (0, 0, 0)),
            pl.BlockSpec((CHUNK, LANES), lambda i: (0, 0)),
        ],
        out_specs=pl.BlockSpec((CHUNK, GM_WIDTH), lambda i: (i, 0)),
        out_shape=jax.ShapeDtypeStruct((m, GM_WIDTH), BF16),
        compiler_params=_params(1),
        name="sgu",
    )(proj, proj, v_norm_w.reshape(1, GM_WIDTH), w_s, b_s_t)


def _split3(v):
    hi = v.astype(BF16).astype(F32)
    rest = v - hi
    mid = rest.astype(BF16).astype(F32)
    return hi, mid, rest - mid


def _pack3(v):
    hi, mid, lo = _split3(v)
    lane = lax.broadcasted_iota(jnp.int32, v.shape, 1)
    packed = jnp.where(lane < SSM_HEADS, hi,
                       jnp.where(lane < 2 * SSM_HEADS, pltpu.roll(mid, SSM_HEADS, 1),
                                 jnp.where(lane < 3 * SSM_HEADS, pltpu.roll(lo, 2 * SSM_HEADS, 1), 0.0)))
    return packed.astype(BF16)


def _ssd_kernel(z_ref, xs_ref, bc_ref, dt_ref, cw_ref, cb_ref, dtb_ref, alog_ref, dskip_ref, nw_ref,
                o_ref, ext_ref, state_ref, y_ref, xc_ref, bmat_ref, cmat_ref, expand_ref):
    conv_ch = SSM_WIDTH + 2 * SSM_BC_WIDTH

    @pl.when(pl.program_id(0) == 0)
    def _():
        ext_ref[0:HALO_ROWS, :] = jnp.zeros((HALO_ROWS, conv_ch), F32)
        state_ref[...] = jnp.zeros_like(state_ref)
        k_idx = lax.broadcasted_iota(jnp.int32, (LANES, SSM_WIDTH), 0)
        j_idx = lax.broadcasted_iota(jnp.int32, (LANES, SSM_WIDTH), 1)
        hit = (k_idx < 3 * SSM_HEADS) & (j_idx // SSM_HEADDIM == k_idx % SSM_HEADS)
        expand_ref[...] = hit.astype(F32).astype(BF16)

    ext_ref[HALO_ROWS:, 0:SSM_WIDTH] = xs_ref[...]
    ext_ref[HALO_ROWS:, SSM_WIDTH:] = bc_ref[...]

    def conv_silu(c0, c1):
        acc = cb_ref[:, c0:c1]
        for k in range(SSM_CONV):
            off = HALO_ROWS - (SSM_CONV - 1) + k
            acc = acc + ext_ref[off:off + CHUNK, c0:c1] * cw_ref[k:k + 1, c0:c1]
        return _silu(acc)

    blk = 512
    for c0 in range(0, SSM_WIDTH, blk):
        xc_ref[:, c0:c0 + blk] = conv_silu(c0, c0 + blk)
    for c0 in range(0, SSM_BC_WIDTH, blk):
        bmat_ref[:, c0:c0 + blk] = conv_silu(SSM_WIDTH + c0, SSM_WIDTH + c0 + blk).astype(BF16)
        cmat_ref[:, c0:c0 + blk] = conv_silu(SSM_WIDTH + SSM_BC_WIDTH + c0,
                                             SSM_WIDTH + SSM_BC_WIDTH + c0 + blk).astype(BF16)
    ext_ref[0:HALO_ROWS, :] = ext_ref[CHUNK:CHUNK + HALO_ROWS, :]

    pre = dt_ref[...] + dtb_ref[...]
    dt = jnp.maximum(pre, 0.0) + jnp.log1p(jnp.exp(-jnp.abs(pre)))
    da = dt * (-jnp.exp(alog_ref[...]))
    row = lax.broadcasted_iota(jnp.int32, (CHUNK, CHUNK), 0)
    col = lax.broadcasted_iota(jnp.int32, (CHUNK, CHUNK), 1)
    causal = col <= row
    tri = causal.astype(F32).astype(BF16)
    da_hi, da_mid, da_lo = _split3(da)
    parts = _dot(tri, jnp.concatenate([da_hi, da_mid, da_lo], axis=1).astype(BF16))
    cum = parts[:, :LANES] + parts[:, LANES:2 * LANES] + parts[:, 2 * LANES:]
    cum_t = cum.T
    cum_last = cum[CHUNK - 1:CHUNK, :]

    packed = jnp.concatenate([_pack3(dt), _pack3(dt * jnp.exp(cum_last - cum)), _pack3(jnp.exp(cum))], axis=0)
    expanded = _dot(packed, expand_ref[...])
    dt_e = expanded[0:CHUNK]
    dt_decay_end_e = expanded[CHUNK:2 * CHUNK]
    exp_cum_e = expanded[2 * CHUNK:]
    chunk_decay_e = exp_cum_e[CHUNK - 1:CHUNK, :]

    xs = xc_ref[...]
    xdt = (xs * dt_e).astype(BF16)
    xdt_end = (xs * dt_decay_end_e).astype(BF16)
    lane_head = lax.broadcasted_iota(jnp.int32, (1, SSM_GROUP_WIDTH), 1) // SSM_HEADDIM

    for g in range(SSM_GROUPS):
        gcols = slice(g * SSM_GROUP_WIDTH, (g + 1) * SSM_GROUP_WIDTH)
        ncols = slice(g * SSM_STATE, (g + 1) * SSM_STATE)
        c_g = cmat_ref[:, ncols]
        b_g = bmat_ref[:, ncols]
        scores = _dot_nt(c_g, b_g)
        xdt_g = xdt[:, gcols]
        y = _dot(c_g, state_ref[g].astype(BF16)) * exp_cum_e[:, gcols]
        for r in range(SSM_HEADS_PER_GROUP):
            h = g * SSM_HEADS_PER_GROUP + r
            seg = cum[:, h:h + 1] - cum_t[h:h + 1, :]
            decay = jnp.exp(jnp.where(causal, seg, -jnp.inf))
            x_r = xdt_g * (lane_head == r).astype(F32).astype(BF16)
            y = y + _dot((scores * decay).astype(BF16), x_r)
        state_ref[g] = state_ref[g] * chunk_decay_e[:, gcols] + _dot_tn(b_g, xdt_end[:, gcols])
        y_ref[:, gcols] = y

    y = y_ref[...] + dskip_ref[...] * xs
    y = y * _silu(z_ref[...])
    o_ref[...] = _rmsnorm(y, nw_ref[...]).astype(o_ref.dtype)


def _ssd(proj, dt_raw, conv_w, conv_b, dt_bias, a_log, d_skip_e, norm_w):
    m = proj.shape[0]
    row = lambda a: a.reshape(1, -1)
    c0 = 2 * GM_WIDTH // SSM_WIDTH
    full = lambda shape: pl.BlockSpec(shape, lambda i: (0,) * len(shape))
    return pl.pallas_call(
        _ssd_kernel,
        grid=(m // CHUNK,),
        in_specs=[
            pl.BlockSpec((CHUNK, SSM_WIDTH), lambda i: (i, c0)),
            pl.BlockSpec((CHUNK, SSM_WIDTH), lambda i: (i, c0 + 1)),
            pl.BlockSpec((CHUNK, 2 * SSM_BC_WIDTH), lambda i: (i, c0 + 2)),
            pl.BlockSpec((CHUNK, LANES), lambda i: (i, 0)),
            full((SSM_CONV, SSM_WIDTH + 2 * SSM_BC_WIDTH)),
            full((1, SSM_WIDTH + 2 * SSM_BC_WIDTH)),
            full((1, LANES)),
            full((1, LANES)),
            full((1, SSM_WIDTH)),
            full((1, SSM_WIDTH)),
        ],
        out_specs=pl.BlockSpec((CHUNK, SSM_WIDTH), lambda i: (i, 0)),
        out_shape=jax.ShapeDtypeStruct((m, SSM_WIDTH), BF16),
        scratch_shapes=[pltpu.VMEM((HALO_ROWS + CHUNK, SSM_WIDTH + 2 * SSM_BC_WIDTH), F32),
                        pltpu.VMEM((SSM_GROUPS, SSM_STATE, SSM_GROUP_WIDTH), F32),
                        pltpu.VMEM((CHUNK, SSM_WIDTH), F32),
                        pltpu.VMEM((CHUNK, SSM_WIDTH), F32),
                        pltpu.VMEM((CHUNK, SSM_BC_WIDTH), BF16),
                        pltpu.VMEM((CHUNK, SSM_BC_WIDTH), BF16),
                        pltpu.VMEM((LANES, SSM_WIDTH), BF16)],
        compiler_params=_params(1),
        name="ssd",
    )(proj, proj, proj, dt_raw, conv_w, row(conv_b), row(dt_bias), row(a_log), row(d_skip_e), row(norm_w))


def _attn_kernel(q_ref, k_ref, v_ref, o_ref):
    scale = XA_HEADDIM ** -0.5
    for h in range(XA_HEADS):
        cols = slice(h * XA_HEADDIM, (h + 1) * XA_HEADDIM)
        logits = _dot_nt(q_ref[:, cols], k_ref[:, cols]) * scale
        p = jnp.exp(logits - jnp.max(logits, axis=-1, keepdims=True))
        p = p / jnp.sum(p, axis=-1, keepdims=True)
        o_ref[:, cols] = _dot(p.astype(BF16), v_ref[:, cols]).astype(o_ref.dtype)


def _attention(q, kv, *, tm=512):
    m = q.shape[0]
    return pl.pallas_call(
        _attn_kernel,
        grid=(m // tm,),
        in_specs=[
            pl.BlockSpec((tm, D_MODEL), lambda i: (i, 0)),
            pl.BlockSpec((MEM_LEN, D_MODEL), lambda i: (0, 0)),
            pl.BlockSpec((MEM_LEN, D_MODEL), lambda i: (0, 1)),
        ],
        out_specs=pl.BlockSpec((tm, D_MODEL), lambda i: (i, 0)),
        out_shape=jax.ShapeDtypeStruct((m, D_MODEL), BF16),
        compiler_params=_params(1),
        name="xattn",
    )(q, kv, kv)


def kernel(x, mem, ffn1_norm, ffn1_w_gu, ffn1_w_down, mix_norm, w_in, gm_v_norm, gm_w_s, gm_b_s, ssm_conv_w, ssm_conv_b, ssm_dt_bias, ssm_a_log, ssm_d, ssm_norm, w_out, xa_norm, mem_norm, xa_w_q, xa_w_kv, xa_w_o, ffn2_norm, ffn2_w_gu, ffn2_w_down, final_norm):
    bf = lambda w: w.astype(BF16)
    pad_lanes = lambda a: jnp.pad(a, ((0, 0), (0, LANES - a.shape[1])))
    h = x[0]
    for i in range(ffn1_norm.shape[0]):
        h = _ffn(h, ffn1_norm[i], ffn1_w_gu[i], ffn1_w_down[i], final_norm, final_norm=False)

        proj, dt_raw = _in_proj(h, mix_norm[i], w_in[i])
        a_out = _sgu(proj, gm_v_norm[i], gm_w_s[i], pad_lanes(gm_b_s[i].T))
        m_out = _ssd(proj, dt_raw, ssm_conv_w[i], ssm_conv_b[i],
                     pad_lanes(ssm_dt_bias[i][None])[0], pad_lanes(ssm_a_log[i][None])[0],
                     jnp.repeat(ssm_d[i], SSM_HEADDIM), ssm_norm[i])
        h = _matmul_residual([a_out, m_out], bf(w_out[i]), h, tm=1024, tn=512, name="mix_out")

        q = _norm_matmul(h, xa_norm[i], bf(xa_w_q[i]), out_dtype=BF16, tm=1024, tn=512, name="xa_q")
        kv = _norm_matmul(mem[0], mem_norm[i], bf(xa_w_kv[i]), out_dtype=BF16, tm=MEM_LEN, tn=1024, name="xa_kv")
        o = _attention(q, kv)
        h = _matmul_residual([o], bf(xa_w_o[i]), h, tm=1024, tn=512, name="xa_out")

        last = i == ffn1_norm.shape[0] - 1
        h = _ffn(h, ffn2_norm[i], ffn2_w_gu[i], ffn2_w_down[i], final_norm, final_norm=last)
    return h[None]
```

```python
import functools

import jax
import jax.numpy as jnp
from jax import lax
from jax.experimental import pallas as pl
from jax.experimental.pallas import tpu as pltpu

D_MODEL = 2048
SEQ = 8192
MEM_LEN = 256
GM_WIDTH = 2048
GM_GROUPS = 4
GM_GROUP_WIDTH = GM_WIDTH // GM_GROUPS
CHUNK = 128
SSM_WIDTH = 2048
SSM_HEADDIM = 64
SSM_HEADS = SSM_WIDTH // SSM_HEADDIM
SSM_GROUPS = 8
SSM_HEADS_PER_GROUP = SSM_HEADS // SSM_GROUPS
SSM_GROUP_WIDTH = SSM_HEADS_PER_GROUP * SSM_HEADDIM
SSM_STATE = 128
SSM_CONV = 4
SSM_BC_WIDTH = SSM_GROUPS * SSM_STATE
PROJ_MAIN = 2 * GM_WIDTH + SSM_WIDTH + SSM_WIDTH + 2 * SSM_BC_WIDTH
XA_HEADS = 4
XA_HEADDIM = D_MODEL // XA_HEADS
D_FF = 5632
EPS = 1e-6

LANES = 128
HALO_ROWS = 8
ROW_BLOCK = 128
VMEM_LIMIT = 60 * 1024 * 1024

BF16 = jnp.bfloat16
F32 = jnp.float32


def _params(n_axes):
    return pltpu.CompilerParams(dimension_semantics=("arbitrary",) * n_axes,
                                vmem_limit_bytes=VMEM_LIMIT)


def _rmsnorm(x, w):
    return x * lax.rsqrt(jnp.mean(x * x, axis=-1, keepdims=True) + EPS) * w


def _sigmoid(x):
    return 1.0 / (1.0 + jnp.exp(-x))


def _silu(x):
    return x * _sigmoid(x)


def _gelu_tanh(x):
    return 0.5 * x * (1.0 + jnp.tanh(0.7978845608028654 * (x + 0.044715 * (x * x * x))))


def _dot(a, b):
    return jnp.dot(a, b, preferred_element_type=F32)


def _dot_nt(a, b):
    return lax.dot_general(a, b, (((1,), (1,)), ((), ())), preferred_element_type=F32)


def _dot_tn(a, b):
    return lax.dot_general(a, b, (((0,), (0,)), ((), ())), preferred_element_type=F32)


def _dot_exact(a, b):
    return jnp.dot(a, b, preferred_element_type=F32, precision=lax.Precision.HIGHEST)


def _ffn_kernel(x_ref, nw_ref, wg_ref, wu_ref, wd_ref, tw_ref, o_ref, *rest, tail):
    xn_ref = rest[-1]
    j = pl.program_id(1)
    n_row_blocks = x_ref.shape[0] // ROW_BLOCK

    def row_block(r):
        return pl.ds(pl.multiple_of(r * ROW_BLOCK, ROW_BLOCK), ROW_BLOCK)

    @pl.when(j == 0)
    def _():
        def body(r, _):
            rows = row_block(r)
            xn_ref[rows, :] = _rmsnorm(x_ref[rows, :], nw_ref[...]).astype(BF16)
            o_ref[rows, :] = jnp.zeros((ROW_BLOCK, o_ref.shape[1]), F32)
        lax.fori_loop(0, n_row_blocks, body, None)

    xn = xn_ref[...]
    g = _dot(xn, wg_ref[...].astype(BF16))
    u = _dot(xn, wu_ref[...].astype(BF16))
    o_ref[...] += _dot((_silu(g) * u).astype(BF16), wd_ref[...].astype(BF16))

    @pl.when(j == pl.num_programs(1) - 1)
    def _():
        def body(r, _):
            rows = row_block(r)
            h = x_ref[rows, :] + 0.5 * o_ref[rows, :]
            if tail == "final":
                h = _rmsnorm(h, tw_ref[...])
            o_ref[rows, :] = h
            if tail == "next":
                rest[0][rows, :] = _rmsnorm(h, tw_ref[...]).astype(BF16)
        lax.fori_loop(0, n_row_blocks, body, None)


def _ffn(x, norm_w, w_gu, w_down, tail_w, *, tail, tm=1024, tf=256):
    m, d = x.shape
    f = w_down.shape[0]
    nf = f // tf
    row_tile = pl.BlockSpec((tm, d), lambda i, j: (i, 0))
    out_specs, out_shape = [row_tile], [jax.ShapeDtypeStruct((m, d), F32)]
    if tail == "next":
        out_specs.append(row_tile)
        out_shape.append(jax.ShapeDtypeStruct((m, d), BF16))
    return pl.pallas_call(
        functools.partial(_ffn_kernel, tail=tail),
        grid=(m // tm, nf),
        in_specs=[
            row_tile,
            pl.BlockSpec((1, d), lambda i, j: (0, 0)),
            pl.BlockSpec((d, tf), lambda i, j: (0, j)),
            pl.BlockSpec((d, tf), lambda i, j: (0, j + nf)),
            pl.BlockSpec((tf, d), lambda i, j: (j, 0)),
            pl.BlockSpec((1, d), lambda i, j: (0, 0)),
        ],
        out_specs=out_specs,
        out_shape=out_shape,
        scratch_shapes=[pltpu.VMEM((tm, d), BF16)],
        compiler_params=_params(2),
        name="ffn",
    )(x, norm_w.reshape(1, d), w_gu, w_gu, w_down, tail_w.reshape(1, d))


def _in_proj_kernel(xn_ref, wt_ref, wdt_ref, o_ref, dt_ref):
    @pl.when(pl.program_id(1) == 0)
    def _():
        dt_ref[...] = _dot_nt(xn_ref[...], wdt_ref[...].astype(BF16))

    o_ref[...] = _dot_nt(xn_ref[...], wt_ref[...].astype(BF16))


def _in_proj(xn, w_in_t, *, tm=2048, tn=512):
    m, k = xn.shape
    w_dt_t = jnp.pad(w_in_t[PROJ_MAIN:], ((0, LANES - (w_in_t.shape[0] - PROJ_MAIN)), (0, 0)))
    return pl.pallas_call(
        _in_proj_kernel,
        grid=(m // tm, PROJ_MAIN // tn),
        in_specs=[
            pl.BlockSpec((tm, k), lambda i, j: (i, 0)),
            pl.BlockSpec((tn, k), lambda i, j: (j, 0)),
            pl.BlockSpec((LANES, k), lambda i, j: (0, 0)),
        ],
        out_specs=[pl.BlockSpec((tm, tn), lambda i, j: (i, j)),
                   pl.BlockSpec((tm, LANES), lambda i, j: (i, 0))],
        out_shape=[jax.ShapeDtypeStruct((m, PROJ_MAIN), F32), jax.ShapeDtypeStruct((m, LANES), F32)],
        compiler_params=_params(2),
        name="in_proj",
    )(xn, w_in_t, w_dt_t)


def _norm_matmul_kernel(x_ref, nw_ref, w_ref, o_ref, xn_ref):
    @pl.when(pl.program_id(1) == 0)
    def _():
        xn_ref[...] = _rmsnorm(x_ref[...], nw_ref[...]).astype(BF16)

    o_ref[...] = _dot(xn_ref[...], w_ref[...]).astype(o_ref.dtype)


def _norm_matmul(x, norm_w, w, *, out_dtype, tm, tn, name):
    m, k = x.shape
    n = w.shape[1]
    return pl.pallas_call(
        _norm_matmul_kernel,
        grid=(m // tm, n // tn),
        in_specs=[
            pl.BlockSpec((tm, k), lambda i, j: (i, 0)),
            pl.BlockSpec((1, k), lambda i, j: (0, 0)),
            pl.BlockSpec((k, tn), lambda i, j: (0, j)),
        ],
        out_specs=pl.BlockSpec((tm, tn), lambda i, j: (i, j)),
        out_shape=jax.ShapeDtypeStruct((m, n), out_dtype),
        scratch_shapes=[pltpu.VMEM((tm, k), BF16)],
        compiler_params=_params(2),
        name=name,
    )(x, norm_w.reshape(1, k), w)


def _matmul_residual_kernel(*refs, n_lhs):
    x_refs = refs[:n_lhs]
    w_refs = refs[n_lhs:2 * n_lhs]
    r_ref, o_ref = refs[2 * n_lhs], refs[2 * n_lhs + 1]
    acc = r_ref[...]
    for x_ref, w_ref in zip(x_refs, w_refs):
        acc = acc + _dot(x_ref[...], w_ref[...])
    o_ref[...] = acc


def _matmul_residual(xs, w, residual, *, tm, tn, name):
    m, n = residual.shape
    k = xs[0].shape[1]
    in_specs = ([pl.BlockSpec((tm, k), lambda i, j: (i, 0)) for _ in xs]
                + [pl.BlockSpec((k, tn), functools.partial(lambda i, j, r: (r, j), r=r)) for r in range(len(xs))]
                + [pl.BlockSpec((tm, tn), lambda i, j: (i, j))])
    return pl.pallas_call(
        functools.partial(_matmul_residual_kernel, n_lhs=len(xs)),
        grid=(m // tm, n // tn),
        in_specs=in_specs,
        out_specs=pl.BlockSpec((tm, tn), lambda i, j: (i, j)),
        out_shape=jax.ShapeDtypeStruct((m, n), F32),
        compiler_params=_params(2),
        name=name,
    )(*xs, *([w] * len(xs)), residual)


def _sgu_kernel(u_ref, v_ref, vw_ref, ws_ref, bs_ref, o_ref):
    vn = _rmsnorm(_gelu_tanh(v_ref[...]), vw_ref[...]).astype(BF16)
    t_idx = lax.broadcasted_iota(jnp.int32, (CHUNK, CHUNK), 0)
    s_idx = lax.broadcasted_iota(jnp.int32, (CHUNK, CHUNK), 1)
    causal = s_idx <= t_idx
    bias = bs_ref[...]
    for g in range(GM_GROUPS):
        cols = slice(g * GM_GROUP_WIDTH, (g + 1) * GM_GROUP_WIDTH)
        w = jnp.where(causal, ws_ref[g], 0.0).astype(BF16)
        mixed = _dot(w, vn[:, cols]) + bias[:, g:g + 1]
        o_ref[:, cols] = (_gelu_tanh(u_ref[:, cols]) * mixed).astype(o_ref.dtype)


def _sgu(proj, v_norm_w, w_s, b_s_t):
    m = proj.shape[0]
    return pl.pallas_call(
        _sgu_kernel,
        grid=(m // CHUNK,),
        in_specs=[
            pl.BlockSpec((CHUNK, GM_WIDTH), lambda i: (i, 0)),
            pl.BlockSpec((CHUNK, GM_WIDTH), lambda i: (i, 1)),
            pl.BlockSpec((1, GM_WIDTH), lambda i: (0, 0)),
            pl.BlockSpec((GM_GROUPS, CHUNK, CHUNK), lambda i: (0, 0, 0)),
            pl.BlockSpec((CHUNK, LANES), lambda i: (0, 0)),
        ],
        out_specs=pl.BlockSpec((CHUNK, GM_WIDTH), lambda i: (i, 0)),
        out_shape=jax.ShapeDtypeStruct((m, GM_WIDTH), BF16),
        compiler_params=_params(1),
        name="sgu",
    )(proj, proj, v_norm_w.reshape(1, GM_WIDTH), w_s, b_s_t)


def _split3(v):
    hi = v.astype(BF16).astype(F32)
    rest = v - hi
    mid = rest.astype(BF16).astype(F32)
    return hi, mid, rest - mid


def _pack3(v):
    hi, mid, lo = _split3(v)
    lane = lax.broadcasted_iota(jnp.int32, v.shape, 1)
    packed = jnp.where(lane < SSM_HEADS, hi,
                       jnp.where(lane < 2 * SSM_HEADS, pltpu.roll(mid, SSM_HEADS, 1),
                                 jnp.where(lane < 3 * SSM_HEADS, pltpu.roll(lo, 2 * SSM_HEADS, 1), 0.0)))
    return packed.astype(BF16)


def _ssd_kernel(z_ref, xs_ref, bc_ref, dt_ref, cw_ref, cb_ref, dtb_ref, alog_ref, dskip_ref, nw_ref,
                o_ref, ext_ref, state_ref, y_ref, xc_ref, bmat_ref, cmat_ref, expand_ref):
    conv_ch = SSM_WIDTH + 2 * SSM_BC_WIDTH

    @pl.when(pl.program_id(0) == 0)
    def _():
        ext_ref[0:HALO_ROWS, :] = jnp.zeros((HALO_ROWS, conv_ch), F32)
        state_ref[...] = jnp.zeros_like(state_ref)
        k_idx = lax.broadcasted_iota(jnp.int32, (LANES, SSM_WIDTH), 0)
        j_idx = lax.broadcasted_iota(jnp.int32, (LANES, SSM_WIDTH), 1)
        hit = (k_idx < 3 * SSM_HEADS) & (j_idx // SSM_HEADDIM == k_idx % SSM_HEADS)
        expand_ref[...] = hit.astype(F32).astype(BF16)

    ext_ref[HALO_ROWS:, 0:SSM_WIDTH] = xs_ref[...]
    ext_ref[HALO_ROWS:, SSM_WIDTH:] = bc_ref[...]

    def conv_silu(c0, c1):
        acc = cb_ref[:, c0:c1]
        for k in range(SSM_CONV):
            off = HALO_ROWS - (SSM_CONV - 1) + k
            acc = acc + ext_ref[off:off + CHUNK, c0:c1] * cw_ref[k:k + 1, c0:c1]
        return _silu(acc)

    blk = 512
    for c0 in range(0, SSM_WIDTH, blk):
        xc_ref[:, c0:c0 + blk] = conv_silu(c0, c0 + blk)
    for c0 in range(0, SSM_BC_WIDTH, blk):
        bmat_ref[:, c0:c0 + blk] = conv_silu(SSM_WIDTH + c0, SSM_WIDTH + c0 + blk).astype(BF16)
        cmat_ref[:, c0:c0 + blk] = conv_silu(SSM_WIDTH + SSM_BC_WIDTH + c0,
                                             SSM_WIDTH + SSM_BC_WIDTH + c0 + blk).astype(BF16)
    ext_ref[0:HALO_ROWS, :] = ext_ref[CHUNK:CHUNK + HALO_ROWS, :]

    pre = dt_ref[...] + dtb_ref[...]
    dt = jnp.maximum(pre, 0.0) + jnp.log1p(jnp.exp(-jnp.abs(pre)))
    da = dt * (-jnp.exp(alog_ref[...]))
    row = lax.broadcasted_iota(jnp.int32, (CHUNK, CHUNK), 0)
    col = lax.broadcasted_iota(jnp.int32, (CHUNK, CHUNK), 1)
    causal = col <= row
    tri = causal.astype(F32).astype(BF16)
    da_hi, da_mid, da_lo = _split3(da)
    parts = _dot(tri, jnp.concatenate([da_hi, da_mid, da_lo], axis=1).astype(BF16))
    cum = parts[:, :LANES] + parts[:, LANES:2 * LANES] + parts[:, 2 * LANES:]
    cum_t = cum.T
    cum_last = cum[CHUNK - 1:CHUNK, :]

    packed = jnp.concatenate([_pack3(dt), _pack3(dt * jnp.exp(cum_last - cum)), _pack3(jnp.exp(cum))], axis=0)
    expanded = _dot(packed, expand_ref[...])
    dt_e = expanded[0:CHUNK]
    dt_decay_end_e = expanded[CHUNK:2 * CHUNK]
    exp_cum_e = expanded[2 * CHUNK:]
    chunk_decay_e = exp_cum_e[CHUNK - 1:CHUNK, :]

    xs = xc_ref[...]
    xdt = (xs * dt_e).astype(BF16)
    xdt_end = (xs * dt_decay_end_e).astype(BF16)
    lane_head = lax.broadcasted_iota(jnp.int32, (1, SSM_GROUP_WIDTH), 1) // SSM_HEADDIM

    for g in range(SSM_GROUPS):
        gcols = slice(g * SSM_GROUP_WIDTH, (g + 1) * SSM_GROUP_WIDTH)
        ncols = slice(g * SSM_STATE, (g + 1) * SSM_STATE)
        c_g = cmat_ref[:, ncols]
        b_g = bmat_ref[:, ncols]
        scores = _dot_nt(c_g, b_g)
        xdt_g = xdt[:, gcols]
        y = _dot(c_g, state_ref[g].astype(BF16)) * exp_cum_e[:, gcols]
        for r in range(SSM_HEADS_PER_GROUP):
            h = g * SSM_HEADS_PER_GROUP + r
            seg = cum[:, h:h + 1] - cum_t[h:h + 1, :]
            decay = jnp.exp(jnp.where(causal, seg, -jnp.inf))
            x_r = xdt_g * (lane_head == r).astype(F32).astype(BF16)
            y = y + _dot((scores * decay).astype(BF16), x_r)
        state_ref[g] = state_ref[g] * chunk_decay_e[:, gcols] + _dot_tn(b_g, xdt_end[:, gcols])
        y_ref[:, gcols] = y

    y = y_ref[...] + dskip_ref[...] * xs
    y = y * _silu(z_ref[...])
    o_ref[...] = _rmsnorm(y, nw_ref[...]).astype(o_ref.dtype)


def _ssd(proj, dt_raw, conv_w, conv_b, dt_bias, a_log, d_skip_e, norm_w):
    m = proj.shape[0]
    row = lambda a: a.reshape(1, -1)
    c0 = 2 * GM_WIDTH // SSM_WIDTH
    full = lambda shape: pl.BlockSpec(shape, lambda i: (0,) * len(shape))
    return pl.pallas_call(
        _ssd_kernel,
        grid=(m // CHUNK,),
        in_specs=[
            pl.BlockSpec((CHUNK, SSM_WIDTH), lambda i: (i, c0)),
            pl.BlockSpec((CHUNK, SSM_WIDTH), lambda i: (i, c0 + 1)),
            pl.BlockSpec((CHUNK, 2 * SSM_BC_WIDTH), lambda i: (i, c0 + 2)),
            pl.BlockSpec((CHUNK, LANES), lambda i: (i, 0)),
            full((SSM_CONV, SSM_WIDTH + 2 * SSM_BC_WIDTH)),
            full((1, SSM_WIDTH + 2 * SSM_BC_WIDTH)),
            full((1, LANES)),
            full((1, LANES)),
            full((1, SSM_WIDTH)),
            full((1, SSM_WIDTH)),
        ],
        out_specs=pl.BlockSpec((CHUNK, SSM_WIDTH), lambda i: (i, 0)),
        out_shape=jax.ShapeDtypeStruct((m, SSM_WIDTH), BF16),
        scratch_shapes=[pltpu.VMEM((HALO_ROWS + CHUNK, SSM_WIDTH + 2 * SSM_BC_WIDTH), F32),
                        pltpu.VMEM((SSM_GROUPS, SSM_STATE, SSM_GROUP_WIDTH), F32),
                        pltpu.VMEM((CHUNK, SSM_WIDTH), F32),
                        pltpu.VMEM((CHUNK, SSM_WIDTH), F32),
                        pltpu.VMEM((CHUNK, SSM_BC_WIDTH), BF16),
                        pltpu.VMEM((CHUNK, SSM_BC_WIDTH), BF16),
                        pltpu.VMEM((LANES, SSM_WIDTH), BF16)],
        compiler_params=_params(1),
        name="ssd",
    )(proj, proj, proj, dt_raw, conv_w, row(conv_b), row(dt_bias), row(a_log), row(d_skip_e), row(norm_w))


def _attn_kernel(q_ref, k_ref, v_ref, o_ref):
    scale = XA_HEADDIM ** -0.5
    for h in range(XA_HEADS):
        cols = slice(h * XA_HEADDIM, (h + 1) * XA_HEADDIM)
        logits = _dot_nt(q_ref[:, cols], k_ref[:, cols]) * scale
        p = jnp.exp(logits - jnp.max(logits, axis=-1, keepdims=True))
        p = p / jnp.sum(p, axis=-1, keepdims=True)
        o_ref[:, cols] = _dot(p.astype(BF16), v_ref[:, cols]).astype(o_ref.dtype)


def _attention(q, kv, *, tm=512):
    m = q.shape[0]
    return pl.pallas_call(
        _attn_kernel,
        grid=(m // tm,),
        in_specs=[
            pl.BlockSpec((tm, D_MODEL), lambda i: (i, 0)),
            pl.BlockSpec((MEM_LEN, D_MODEL), lambda i: (0, 0)),
            pl.BlockSpec((MEM_LEN, D_MODEL), lambda i: (0, 1)),
        ],
        out_specs=pl.BlockSpec((tm, D_MODEL), lambda i: (i, 0)),
        out_shape=jax.ShapeDtypeStruct((m, D_MODEL), BF16),
        compiler_params=_params(1),
        name="xattn",
    )(q, kv, kv)


def kernel(x, mem, ffn1_norm, ffn1_w_gu, ffn1_w_down, mix_norm, w_in, gm_v_norm, gm_w_s, gm_b_s, ssm_conv_w, ssm_conv_b, ssm_dt_bias, ssm_a_log, ssm_d, ssm_norm, w_out, xa_norm, mem_norm, xa_w_q, xa_w_kv, xa_w_o, ffn2_norm, ffn2_w_gu, ffn2_w_down, final_norm):
    bf = lambda w: w.astype(BF16)
    pad_lanes = lambda a: jnp.pad(a, ((0, 0), (0, LANES - a.shape[1])))
    h = x[0]
    for i in range(ffn1_norm.shape[0]):
        h, xn = _ffn(h, ffn1_norm[i], ffn1_w_gu[i], ffn1_w_down[i], mix_norm[i], tail="next")

        proj, dt_raw = _in_proj(xn, jnp.swapaxes(w_in[i], 0, 1))
        a_out = _sgu(proj, gm_v_norm[i], gm_w_s[i], pad_lanes(gm_b_s[i].T))
        m_out = _ssd(proj, dt_raw, ssm_conv_w[i], ssm_conv_b[i],
                     pad_lanes(ssm_dt_bias[i][None])[0], pad_lanes(ssm_a_log[i][None])[0],
                     jnp.repeat(ssm_d[i], SSM_HEADDIM), ssm_norm[i])
        h = _matmul_residual([a_out, m_out], bf(w_out[i]), h, tm=1024, tn=512, name="mix_out")

        q = _norm_matmul(h, xa_norm[i], bf(xa_w_q[i]), out_dtype=BF16, tm=1024, tn=512, name="xa_q")
        kv = _norm_matmul(mem[0], mem_norm[i], bf(xa_w_kv[i]), out_dtype=BF16, tm=MEM_LEN, tn=1024, name="xa_kv")
        o = _attention(q, kv)
        h = _matmul_residual([o], bf(xa_w_o[i]), h, tm=1024, tn=512, name="xa_out")

        last = i == ffn1_norm.shape[0] - 1
        h = _ffn(h, ffn2_norm[i], ffn2_w_gu[i], ffn2_w_down[i], final_norm, tail="final" if last else "none")[0]
    return h[None]
```

```python
import functools

import jax
import jax.numpy as jnp
from jax import lax
from jax.experimental import pallas as pl
from jax.experimental.pallas import tpu as pltpu

D_MODEL = 2048
SEQ = 8192
MEM_LEN = 256
GM_WIDTH = 2048
GM_GROUPS = 4
GM_GROUP_WIDTH = GM_WIDTH // GM_GROUPS
CHUNK = 128
SSM_WIDTH = 2048
SSM_HEADDIM = 64
SSM_HEADS = SSM_WIDTH // SSM_HEADDIM
SSM_GROUPS = 8
SSM_HEADS_PER_GROUP = SSM_HEADS // SSM_GROUPS
SSM_GROUP_WIDTH = SSM_HEADS_PER_GROUP * SSM_HEADDIM
SSM_STATE = 128
SSM_CONV = 4
SSM_BC_WIDTH = SSM_GROUPS * SSM_STATE
PROJ_MAIN = 2 * GM_WIDTH + SSM_WIDTH + SSM_WIDTH + 2 * SSM_BC_WIDTH
XA_HEADS = 4
XA_HEADDIM = D_MODEL // XA_HEADS
D_FF = 5632
EPS = 1e-6

LANES = 128
HALO_ROWS = 8
ROW_BLOCK = 128
VMEM_LIMIT = 60 * 1024 * 1024

BF16 = jnp.bfloat16
F32 = jnp.float32


def _params(n_axes):
    return pltpu.CompilerParams(dimension_semantics=("arbitrary",) * n_axes,
                                vmem_limit_bytes=VMEM_LIMIT)


def _rmsnorm(x, w):
    return x * lax.rsqrt(jnp.mean(x * x, axis=-1, keepdims=True) + EPS) * w


def _sigmoid(x):
    return 1.0 / (1.0 + jnp.exp(-x))


def _silu(x):
    return x * _sigmoid(x)


def _gelu_tanh(x):
    return 0.5 * x * (1.0 + jnp.tanh(0.7978845608028654 * (x + 0.044715 * (x * x * x))))


def _dot(a, b):
    return jnp.dot(a, b, preferred_element_type=F32)


def _dot_nt(a, b):
    return lax.dot_general(a, b, (((1,), (1,)), ((), ())), preferred_element_type=F32)


def _dot_tn(a, b):
    return lax.dot_general(a, b, (((0,), (0,)), ((), ())), preferred_element_type=F32)


def _dot_exact(a, b):
    return jnp.dot(a, b, preferred_element_type=F32, precision=lax.Precision.HIGHEST)


def _ffn_kernel(x_ref, nw_ref, wg_ref, wu_ref, wd_ref, tw_ref, o_ref, *rest, tail):
    xn_ref = rest[-1]
    j = pl.program_id(1)
    n_row_blocks = x_ref.shape[0] // ROW_BLOCK

    def row_block(r):
        return pl.ds(pl.multiple_of(r * ROW_BLOCK, ROW_BLOCK), ROW_BLOCK)

    @pl.when(j == 0)
    def _():
        def body(r, _):
            rows = row_block(r)
            xn_ref[rows, :] = _rmsnorm(x_ref[rows, :], nw_ref[...]).astype(BF16)
            o_ref[rows, :] = jnp.zeros((ROW_BLOCK, o_ref.shape[1]), F32)
        lax.fori_loop(0, n_row_blocks, body, None)

    xn = xn_ref[...]
    g = _dot(xn, wg_ref[...].astype(BF16))
    u = _dot(xn, wu_ref[...].astype(BF16))
    o_ref[...] += _dot((_silu(g) * u).astype(BF16), wd_ref[...].astype(BF16))

    @pl.when(j == pl.num_programs(1) - 1)
    def _():
        def body(r, _):
            rows = row_block(r)
            h = x_ref[rows, :] + 0.5 * o_ref[rows, :]
            if tail == "final":
                h = _rmsnorm(h, tw_ref[...])
            o_ref[rows, :] = h
            if tail == "next":
                rest[0][rows, :] = _rmsnorm(h, tw_ref[...]).astype(BF16)
        lax.fori_loop(0, n_row_blocks, body, None)


def _ffn(x, norm_w, w_gu, w_down, tail_w, *, tail, tm=1024, tf=256):
    m, d = x.shape
    f = w_down.shape[0]
    nf = f // tf
    row_tile = pl.BlockSpec((tm, d), lambda i, j: (i, 0))
    out_specs, out_shape = [row_tile], [jax.ShapeDtypeStruct((m, d), F32)]
    if tail == "next":
        out_specs.append(row_tile)
        out_shape.append(jax.ShapeDtypeStruct((m, d), BF16))
    return pl.pallas_call(
        functools.partial(_ffn_kernel, tail=tail),
        grid=(m // tm, nf),
        in_specs=[
            row_tile,
            pl.BlockSpec((1, d), lambda i, j: (0, 0)),
            pl.BlockSpec((d, tf), lambda i, j: (0, j)),
            pl.BlockSpec((d, tf), lambda i, j: (0, j + nf)),
            pl.BlockSpec((tf, d), lambda i, j: (j, 0)),
            pl.BlockSpec((1, d), lambda i, j: (0, 0)),
        ],
        out_specs=out_specs,
        out_shape=out_shape,
        scratch_shapes=[pltpu.VMEM((tm, d), BF16)],
        compiler_params=_params(2),
        name="ffn",
    )(x, norm_w.reshape(1, d), w_gu, w_gu, w_down, tail_w.reshape(1, d))


def _in_proj_kernel(xn_ref, wt_ref, wdt_ref, o_ref, dt_ref):
    @pl.when(pl.program_id(1) == 0)
    def _():
        dt_ref[...] = _dot_nt(xn_ref[...], wdt_ref[...].astype(BF16))

    o_ref[...] = _dot_nt(xn_ref[...], wt_ref[...].astype(BF16))


def _in_proj(xn, w_in_t, *, tm=2048, tn=512):
    m, k = xn.shape
    w_dt_t = jnp.pad(w_in_t[PROJ_MAIN:], ((0, LANES - (w_in_t.shape[0] - PROJ_MAIN)), (0, 0)))
    return pl.pallas_call(
        _in_proj_kernel,
        grid=(m // tm, PROJ_MAIN // tn),
        in_specs=[
            pl.BlockSpec((tm, k), lambda i, j: (i, 0)),
            pl.BlockSpec((tn, k), lambda i, j: (j, 0)),
            pl.BlockSpec((LANES, k), lambda i, j: (0, 0)),
        ],
        out_specs=[pl.BlockSpec((tm, tn), lambda i, j: (i, j)),
                   pl.BlockSpec((tm, LANES), lambda i, j: (i, 0))],
        out_shape=[jax.ShapeDtypeStruct((m, PROJ_MAIN), F32), jax.ShapeDtypeStruct((m, LANES), F32)],
        compiler_params=_params(2),
        name="in_proj",
    )(xn, w_in_t, w_dt_t)


def _norm_matmul_kernel(x_ref, nw_ref, w_ref, o_ref, xn_ref):
    @pl.when(pl.program_id(1) == 0)
    def _():
        xn_ref[...] = _rmsnorm(x_ref[...], nw_ref[...]).astype(BF16)

    o_ref[...] = _dot(xn_ref[...], w_ref[...]).astype(o_ref.dtype)


def _norm_matmul(x, norm_w, w, *, out_dtype, tm, tn, name):
    m, k = x.shape
    n = w.shape[1]
    return pl.pallas_call(
        _norm_matmul_kernel,
        grid=(m // tm, n // tn),
        in_specs=[
            pl.BlockSpec((tm, k), lambda i, j: (i, 0)),
            pl.BlockSpec((1, k), lambda i, j: (0, 0)),
            pl.BlockSpec((k, tn), lambda i, j: (0, j)),
        ],
        out_specs=pl.BlockSpec((tm, tn), lambda i, j: (i, j)),
        out_shape=jax.ShapeDtypeStruct((m, n), out_dtype),
        scratch_shapes=[pltpu.VMEM((tm, k), BF16)],
        compiler_params=_params(2),
        name=name,
    )(x, norm_w.reshape(1, k), w)


def _matmul_residual_kernel(*refs, n_lhs):
    x_refs = refs[:n_lhs]
    w_refs = refs[n_lhs:2 * n_lhs]
    r_ref, o_ref = refs[2 * n_lhs], refs[2 * n_lhs + 1]
    acc = r_ref[...]
    for x_ref, w_ref in zip(x_refs, w_refs):
        acc = acc + _dot(x_ref[...], w_ref[...])
    o_ref[...] = acc


def _matmul_residual(xs, w, residual, *, tm, tn, name):
    m, n = residual.shape
    k = xs[0].shape[1]
    in_specs = ([pl.BlockSpec((tm, k), lambda i, j: (i, 0)) for _ in xs]
                + [pl.BlockSpec((k, tn), functools.partial(lambda i, j, r: (r, j), r=r)) for r in range(len(xs))]
                + [pl.BlockSpec((tm, tn), lambda i, j: (i, j))])
    return pl.pallas_call(
        functools.partial(_matmul_residual_kernel, n_lhs=len(xs)),
        grid=(m // tm, n // tn),
        in_specs=in_specs,
        out_specs=pl.BlockSpec((tm, tn), lambda i, j: (i, j)),
        out_shape=jax.ShapeDtypeStruct((m, n), F32),
        compiler_params=_params(2),
        name=name,
    )(*xs, *([w] * len(xs)), residual)


def _sgu_kernel(u_ref, v_ref, vw_ref, ws_ref, bs_ref, o_ref):
    vn = _rmsnorm(_gelu_tanh(v_ref[...]), vw_ref[...]).astype(BF16)
    t_idx = lax.broadcasted_iota(jnp.int32, (CHUNK, CHUNK), 0)
    s_idx = lax.broadcasted_iota(jnp.int32, (CHUNK, CHUNK), 1)
    causal = s_idx <= t_idx
    bias = bs_ref[...]
    for g in range(GM_GROUPS):
        cols = slice(g * GM_GROUP_WIDTH, (g + 1) * GM_GROUP_WIDTH)
        w = jnp.where(causal, ws_ref[g], 0.0).astype(BF16)
        mixed = _dot(w, vn[:, cols]) + bias[:, g:g + 1]
        o_ref[:, cols] = (_gelu_tanh(u_ref[:, cols]) * mixed).astype(o_ref.dtype)


def _sgu(proj, v_norm_w, w_s, b_s_t):
    m = proj.shape[0]
    return pl.pallas_call(
        _sgu_kernel,
        grid=(m // CHUNK,),
        in_specs=[
            pl.BlockSpec((CHUNK, GM_WIDTH), lambda i: (i, 0)),
            pl.BlockSpec((CHUNK, GM_WIDTH), lambda i: (i, 1)),
            pl.BlockSpec((1, GM_WIDTH), lambda i: (0, 0)),
            pl.BlockSpec((GM_GROUPS, CHUNK, CHUNK), lambda i: (0, 0, 0)),
            pl.BlockSpec((CHUNK, LANES), lambda i: (0, 0)),
        ],
        out_specs=pl.BlockSpec((CHUNK, GM_WIDTH), lambda i: (i, 0)),
        out_shape=jax.ShapeDtypeStruct((m, GM_WIDTH), BF16),
        compiler_params=_params(1),
        name="sgu",
    )(proj, proj, v_norm_w.reshape(1, GM_WIDTH), w_s, b_s_t)


def _split3(v):
    hi = v.astype(BF16).astype(F32)
    rest = v - hi
    mid = rest.astype(BF16).astype(F32)
    return hi, mid, rest - mid


def _pack3(v):
    hi, mid, lo = _split3(v)
    lane = lax.broadcasted_iota(jnp.int32, v.shape, 1)
    packed = jnp.where(lane < SSM_HEADS, hi,
                       jnp.where(lane < 2 * SSM_HEADS, pltpu.roll(mid, SSM_HEADS, 1),
                                 jnp.where(lane < 3 * SSM_HEADS, pltpu.roll(lo, 2 * SSM_HEADS, 1), 0.0)))
    return packed.astype(BF16)


def _ssd_kernel(z_ref, xs_ref, bc_ref, dt_ref, cw_ref, cb_ref, dtb_ref, alog_ref, dskip_ref, nw_ref,
                o_ref, ext_ref, state_ref, y_ref, xc_ref, bmat_ref, cmat_ref, expand_ref):
    conv_ch = SSM_WIDTH + 2 * SSM_BC_WIDTH

    @pl.when(pl.program_id(0) == 0)
    def _():
        ext_ref[0:HALO_ROWS, :] = jnp.zeros((HALO_ROWS, conv_ch), F32)
        state_ref[...] = jnp.zeros_like(state_ref)
        k_idx = lax.broadcasted_iota(jnp.int32, (LANES, SSM_WIDTH), 0)
        j_idx = lax.broadcasted_iota(jnp.int32, (LANES, SSM_WIDTH), 1)
        hit = (k_idx < 3 * SSM_HEADS) & (j_idx // SSM_HEADDIM == k_idx % SSM_HEADS)
        expand_ref[...] = hit.astype(F32).astype(BF16)

    ext_ref[HALO_ROWS:, 0:SSM_WIDTH] = xs_ref[...]
    ext_ref[HALO_ROWS:, SSM_WIDTH:] = bc_ref[...]

    def conv_silu(c0, c1):
        acc = cb_ref[:, c0:c1]
        for k in range(SSM_CONV):
            off = HALO_ROWS - (SSM_CONV - 1) + k
            acc = acc + ext_ref[off:off + CHUNK, c0:c1] * cw_ref[k:k + 1, c0:c1]
        return _silu(acc)

    blk = 512
    for c0 in range(0, SSM_WIDTH, blk):
        xc_ref[:, c0:c0 + blk] = conv_silu(c0, c0 + blk)
    for c0 in range(0, SSM_BC_WIDTH, blk):
        bmat_ref[:, c0:c0 + blk] = conv_silu(SSM_WIDTH + c0, SSM_WIDTH + c0 + blk).astype(BF16)
        cmat_ref[:, c0:c0 + blk] = conv_silu(SSM_WIDTH + SSM_BC_WIDTH + c0,
                                             SSM_WIDTH + SSM_BC_WIDTH + c0 + blk).astype(BF16)
    ext_ref[0:HALO_ROWS, :] = ext_ref[CHUNK:CHUNK + HALO_ROWS, :]

    pre = dt_ref[...] + dtb_ref[...]
    dt = jnp.maximum(pre, 0.0) + jnp.log1p(jnp.exp(-jnp.abs(pre)))
    da = dt * (-jnp.exp(alog_ref[...]))
    row = lax.broadcasted_iota(jnp.int32, (CHUNK, CHUNK), 0)
    col = lax.broadcasted_iota(jnp.int32, (CHUNK, CHUNK), 1)
    causal = col <= row
    tri = causal.astype(F32).astype(BF16)
    da_hi, da_mid, da_lo = _split3(da)
    parts = _dot(tri, jnp.concatenate([da_hi, da_mid, da_lo], axis=1).astype(BF16))
    cum = parts[:, :LANES] + parts[:, LANES:2 * LANES] + parts[:, 2 * LANES:]
    cum_t = cum.T
    cum_last = cum[CHUNK - 1:CHUNK, :]

    packed = jnp.concatenate([_pack3(dt), _pack3(dt * jnp.exp(cum_last - cum)), _pack3(jnp.exp(cum))], axis=0)
    expanded = _dot(packed, expand_ref[...])
    dt_e = expanded[0:CHUNK]
    dt_decay_end_e = expanded[CHUNK:2 * CHUNK]
    exp_cum_e = expanded[2 * CHUNK:]
    chunk_decay_e = exp_cum_e[CHUNK - 1:CHUNK, :]

    xs = xc_ref[...]
    xdt = (xs * dt_e).astype(BF16)
    xdt_end = (xs * dt_decay_end_e).astype(BF16)
    lane_head = lax.broadcasted_iota(jnp.int32, (1, SSM_GROUP_WIDTH), 1) // SSM_HEADDIM

    for g in range(SSM_GROUPS):
        gcols = slice(g * SSM_GROUP_WIDTH, (g + 1) * SSM_GROUP_WIDTH)
        ncols = slice(g * SSM_STATE, (g + 1) * SSM_STATE)
        c_g = cmat_ref[:, ncols]
        b_g = bmat_ref[:, ncols]
        scores = _dot_nt(c_g, b_g)
        xdt_g = xdt[:, gcols]
        y = _dot(c_g, state_ref[g].astype(BF16)) * exp_cum_e[:, gcols]
        for r in range(SSM_HEADS_PER_GROUP):
            h = g * SSM_HEADS_PER_GROUP + r
            seg = cum[:, h:h + 1] - cum_t[h:h + 1, :]
            decay = jnp.exp(jnp.where(causal, seg, -jnp.inf))
            x_r = xdt_g * (lane_head == r).astype(F32).astype(BF16)
            y = y + _dot((scores * decay).astype(BF16), x_r)
        state_ref[g] = state_ref[g] * chunk_decay_e[:, gcols] + _dot_tn(b_g, xdt_end[:, gcols])
        y_ref[:, gcols] = y

    y = y_ref[...] + dskip_ref[...] * xs
    y = y * _silu(z_ref[...])
    o_ref[...] = _rmsnorm(y, nw_ref[...]).astype(o_ref.dtype)


def _ssd(proj, dt_raw, conv_w, conv_b, dt_bias, a_log, d_skip_e, norm_w):
    m = proj.shape[0]
    row = lambda a: a.reshape(1, -1)
    c0 = 2 * GM_WIDTH // SSM_WIDTH
    full = lambda shape: pl.BlockSpec(shape, lambda i: (0,) * len(shape))
    return pl.pallas_call(
        _ssd_kernel,
        grid=(m // CHUNK,),
        in_specs=[
            pl.BlockSpec((CHUNK, SSM_WIDTH), lambda i: (i, c0)),
            pl.BlockSpec((CHUNK, SSM_WIDTH), lambda i: (i, c0 + 1)),
            pl.BlockSpec((CHUNK, 2 * SSM_BC_WIDTH), lambda i: (i, c0 + 2)),
            pl.BlockSpec((CHUNK, LANES), lambda i: (i, 0)),
            full((SSM_CONV, SSM_WIDTH + 2 * SSM_BC_WIDTH)),
            full((1, SSM_WIDTH + 2 * SSM_BC_WIDTH)),
            full((1, LANES)),
            full((1, LANES)),
            full((1, SSM_WIDTH)),
            full((1, SSM_WIDTH)),
        ],
        out_specs=pl.BlockSpec((CHUNK, SSM_WIDTH), lambda i: (i, 0)),
        out_shape=jax.ShapeDtypeStruct((m, SSM_WIDTH), BF16),
        scratch_shapes=[pltpu.VMEM((HALO_ROWS + CHUNK, SSM_WIDTH + 2 * SSM_BC_WIDTH), F32),
                        pltpu.VMEM((SSM_GROUPS, SSM_STATE, SSM_GROUP_WIDTH), F32),
                        pltpu.VMEM((CHUNK, SSM_WIDTH), F32),
                        pltpu.VMEM((CHUNK, SSM_WIDTH), F32),
                        pltpu.VMEM((CHUNK, SSM_BC_WIDTH), BF16),
                        pltpu.VMEM((CHUNK, SSM_BC_WIDTH), BF16),
                        pltpu.VMEM((LANES, SSM_WIDTH), BF16)],
        compiler_params=_params(1),
        name="ssd",
    )(proj, proj, proj, dt_raw, conv_w, row(conv_b), row(dt_bias), row(a_log), row(d_skip_e), row(norm_w))


def _xattn_kernel(h_ref, nw_ref, wq_ref, k_ref, v_ref, wo_ref, o_ref, q_ref, att_ref):
    q_ref[...] = _dot(_rmsnorm(h_ref[...], nw_ref[...]).astype(BF16), wq_ref[...]).astype(BF16)
    scale = XA_HEADDIM ** -0.5
    for h in range(XA_HEADS):
        cols = slice(h * XA_HEADDIM, (h + 1) * XA_HEADDIM)
        logits = _dot_nt(q_ref[:, cols], k_ref[:, cols]) * scale
        p = jnp.exp(logits - jnp.max(logits, axis=-1, keepdims=True))
        p = p / jnp.sum(p, axis=-1, keepdims=True)
        att_ref[:, cols] = _dot(p.astype(BF16), v_ref[:, cols]).astype(BF16)
    o_ref[...] = h_ref[...] + _dot(att_ref[...], wo_ref[...])


def _xattn(h, norm_w, w_q, kv, w_o, *, tm=512):
    m, d = h.shape
    once = pl.Buffered(1)
    return pl.pallas_call(
        _xattn_kernel,
        grid=(m // tm,),
        in_specs=[
            pl.BlockSpec((tm, d), lambda i: (i, 0)),
            pl.BlockSpec((1, d), lambda i: (0, 0)),
            pl.BlockSpec((d, d), lambda i: (0, 0), pipeline_mode=once),
            pl.BlockSpec((MEM_LEN, d), lambda i: (0, 0), pipeline_mode=once),
            pl.BlockSpec((MEM_LEN, d), lambda i: (0, 1), pipeline_mode=once),
            pl.BlockSpec((d, d), lambda i: (0, 0), pipeline_mode=once),
        ],
        out_specs=pl.BlockSpec((tm, d), lambda i: (i, 0)),
        out_shape=jax.ShapeDtypeStruct((m, d), F32),
        scratch_shapes=[pltpu.VMEM((tm, d), BF16), pltpu.VMEM((tm, d), BF16)],
        compiler_params=_params(1),
        name="xattn",
    )(h, norm_w.reshape(1, d), w_q, kv, kv, w_o)


def kernel(x, mem, ffn1_norm, ffn1_w_gu, ffn1_w_down, mix_norm, w_in, gm_v_norm, gm_w_s, gm_b_s, ssm_conv_w, ssm_conv_b, ssm_dt_bias, ssm_a_log, ssm_d, ssm_norm, w_out, xa_norm, mem_norm, xa_w_q, xa_w_kv, xa_w_o, ffn2_norm, ffn2_w_gu, ffn2_w_down, final_norm):
    bf = lambda w: w.astype(BF16)
    pad_lanes = lambda a: jnp.pad(a, ((0, 0), (0, LANES - a.shape[1])))
    h = x[0]
    for i in range(ffn1_norm.shape[0]):
        h, xn = _ffn(h, ffn1_norm[i], ffn1_w_gu[i], ffn1_w_down[i], mix_norm[i], tail="next")

        proj, dt_raw = _in_proj(xn, jnp.swapaxes(w_in[i], 0, 1))
        a_out = _sgu(proj, gm_v_norm[i], gm_w_s[i], pad_lanes(gm_b_s[i].T))
        m_out = _ssd(proj, dt_raw, ssm_conv_w[i], ssm_conv_b[i],
                     pad_lanes(ssm_dt_bias[i][None])[0], pad_lanes(ssm_a_log[i][None])[0],
                     jnp.repeat(ssm_d[i], SSM_HEADDIM), ssm_norm[i])
        h = _matmul_residual([a_out, m_out], bf(w_out[i]), h, tm=1024, tn=512, name="mix_out")

        kv = _norm_matmul(mem[0], mem_norm[i], bf(xa_w_kv[i]), out_dtype=BF16, tm=MEM_LEN, tn=1024, name="xa_kv")
        h = _xattn(h, xa_norm[i], bf(xa_w_q[i]), kv, bf(xa_w_o[i]))

        last = i == ffn1_norm.shape[0] - 1
        h = _ffn(h, ffn2_norm[i], ffn2_w_gu[i], ffn2_w_down[i], final_norm, tail="final" if last else "none")[0]
    return h[None]
```

```python
import functools

import jax
import jax.numpy as jnp
from jax import lax
from jax.experimental import pallas as pl
from jax.experimental.pallas import tpu as pltpu

D_MODEL = 2048
SEQ = 8192
MEM_LEN = 256
GM_WIDTH = 2048
GM_GROUPS = 4
GM_GROUP_WIDTH = GM_WIDTH // GM_GROUPS
CHUNK = 128
SSM_WIDTH = 2048
SSM_HEADDIM = 64
SSM_HEADS = SSM_WIDTH // SSM_HEADDIM
SSM_GROUPS = 8
SSM_HEADS_PER_GROUP = SSM_HEADS // SSM_GROUPS
SSM_GROUP_WIDTH = SSM_HEADS_PER_GROUP * SSM_HEADDIM
SSM_STATE = 128
SSM_CONV = 4
SSM_BC_WIDTH = SSM_GROUPS * SSM_STATE
PROJ_MAIN = 2 * GM_WIDTH + SSM_WIDTH + SSM_WIDTH + 2 * SSM_BC_WIDTH
XA_HEADS = 4
XA_HEADDIM = D_MODEL // XA_HEADS
D_FF = 5632
EPS = 1e-6

LANES = 128
HALO_ROWS = 8
ROW_BLOCK = 128
IN_PROJ_ROW_BLOCK = 256
VMEM_LIMIT = 60 * 1024 * 1024

BF16 = jnp.bfloat16
F32 = jnp.float32


def _params(n_axes):
    return pltpu.CompilerParams(dimension_semantics=("arbitrary",) * n_axes,
                                vmem_limit_bytes=VMEM_LIMIT)


def _rmsnorm(x, w):
    return x * lax.rsqrt(jnp.mean(x * x, axis=-1, keepdims=True) + EPS) * w


def _sigmoid(x):
    return 1.0 / (1.0 + jnp.exp(-x))


def _silu(x):
    return x * _sigmoid(x)


def _gelu_tanh(x):
    c = 0.7978845608028654
    return 0.5 * x * (1.0 + jnp.tanh(x * (c + (0.044715 * c) * (x * x))))


def _dot(a, b):
    return jnp.dot(a, b, preferred_element_type=F32)


def _dot_nt(a, b):
    return lax.dot_general(a, b, (((1,), (1,)), ((), ())), preferred_element_type=F32)


def _dot_tn(a, b):
    return lax.dot_general(a, b, (((0,), (0,)), ((), ())), preferred_element_type=F32)


def _ffn_kernel(x_ref, nw_ref, wg_ref, wu_ref, wd_ref, tw_ref, o_ref, *rest, tail):
    xn_ref = rest[-1]
    j = pl.program_id(1)
    n_row_blocks = x_ref.shape[0] // ROW_BLOCK

    def row_block(r):
        return pl.ds(pl.multiple_of(r * ROW_BLOCK, ROW_BLOCK), ROW_BLOCK)

    @pl.when(j == 0)
    def _():
        def body(r, _):
            rows = row_block(r)
            xn_ref[rows, :] = _rmsnorm(x_ref[rows, :], nw_ref[...]).astype(BF16)
            o_ref[rows, :] = jnp.zeros((ROW_BLOCK, o_ref.shape[1]), F32)
        lax.fori_loop(0, n_row_blocks, body, None)

    xn = xn_ref[...]
    g = _dot(xn, wg_ref[...].astype(BF16))
    u = _dot(xn, wu_ref[...].astype(BF16))
    o_ref[...] += _dot((_silu(g) * u).astype(BF16), wd_ref[...].astype(BF16))

    @pl.when(j == pl.num_programs(1) - 1)
    def _():
        def body(r, _):
            rows = row_block(r)
            h = x_ref[rows, :] + 0.5 * o_ref[rows, :]
            if tail == "final":
                h = _rmsnorm(h, tw_ref[...])
            o_ref[rows, :] = h
            if tail == "next":
                rest[0][rows, :] = _rmsnorm(h, tw_ref[...]).astype(BF16)
        lax.fori_loop(0, n_row_blocks, body, None)


def _ffn(x, norm_w, w_gu, w_down, tail_w, *, tail, tm=1024, tf=256):
    m, d = x.shape
    f = w_down.shape[0]
    nf = f // tf
    row_tile = pl.BlockSpec((tm, d), lambda i, j: (i, 0))
    out_specs, out_shape = [row_tile], [jax.ShapeDtypeStruct((m, d), F32)]
    if tail == "next":
        out_specs.append(row_tile)
        out_shape.append(jax.ShapeDtypeStruct((m, d), BF16))
    return pl.pallas_call(
        functools.partial(_ffn_kernel, tail=tail),
        grid=(m // tm, nf),
        in_specs=[
            row_tile,
            pl.BlockSpec((1, d), lambda i, j: (0, 0)),
            pl.BlockSpec((d, tf), lambda i, j: (0, j)),
            pl.BlockSpec((d, tf), lambda i, j: (0, j + nf)),
            pl.BlockSpec((tf, d), lambda i, j: (j, 0)),
            pl.BlockSpec((1, d), lambda i, j: (0, 0)),
        ],
        out_specs=out_specs,
        out_shape=out_shape,
        scratch_shapes=[pltpu.VMEM((tm, d), BF16)],
        compiler_params=_params(2),
        name="ffn",
    )(x, norm_w.reshape(1, d), w_gu, w_gu, w_down, tail_w.reshape(1, d))


def _in_proj_kernel(xn_ref, wt_ref, wdt_ref, o_ref, dt_ref, *, gelu_tiles, silu_tiles):
    j = pl.program_id(1)

    @pl.when(j == 0)
    def _():
        dt_ref[...] = _dot_nt(xn_ref[...], wdt_ref[...].astype(BF16))

    def project(activation):
        w = wt_ref[...].astype(BF16)
        for r0 in range(0, o_ref.shape[0], IN_PROJ_ROW_BLOCK):
            rows = slice(r0, r0 + IN_PROJ_ROW_BLOCK)
            o_ref[rows, :] = activation(_dot_nt(xn_ref[rows, :], w))

    @pl.when(j < gelu_tiles)
    def _():
        project(_gelu_tanh)

    @pl.when((j >= gelu_tiles) & (j < gelu_tiles + silu_tiles))
    def _():
        project(_silu)

    @pl.when(j >= gelu_tiles + silu_tiles)
    def _():
        project(lambda acc: acc)


def _in_proj(xn, w_in_t, *, tm=2048, tn=512):
    m, k = xn.shape
    kernel_fn = functools.partial(_in_proj_kernel, gelu_tiles=2 * GM_WIDTH // tn, silu_tiles=SSM_WIDTH // tn)
    w_dt_t = jnp.pad(w_in_t[PROJ_MAIN:], ((0, LANES - (w_in_t.shape[0] - PROJ_MAIN)), (0, 0)))
    return pl.pallas_call(
        kernel_fn,
        grid=(m // tm, PROJ_MAIN // tn),
        in_specs=[
            pl.BlockSpec((tm, k), lambda i, j: (i, 0)),
            pl.BlockSpec((tn, k), lambda i, j: (j, 0)),
            pl.BlockSpec((LANES, k), lambda i, j: (0, 0)),
        ],
        out_specs=[pl.BlockSpec((tm, tn), lambda i, j: (i, j)),
                   pl.BlockSpec((tm, LANES), lambda i, j: (i, 0))],
        out_shape=[jax.ShapeDtypeStruct((m, PROJ_MAIN), F32), jax.ShapeDtypeStruct((m, LANES), F32)],
        compiler_params=_params(2),
        name="in_proj",
    )(xn, w_in_t, w_dt_t)


def _sgu_mix(gu_ref, gv_ref, vw_ref, ws_ref, bs_ref, o_ref):
    vn = _rmsnorm(gv_ref[...], vw_ref[...]).astype(BF16)
    t_idx = lax.broadcasted_iota(jnp.int32, (CHUNK, CHUNK), 0)
    s_idx = lax.broadcasted_iota(jnp.int32, (CHUNK, CHUNK), 1)
    causal = s_idx <= t_idx
    bias = bs_ref[...]
    for g in range(GM_GROUPS):
        cols = slice(g * GM_GROUP_WIDTH, (g + 1) * GM_GROUP_WIDTH)
        w = jnp.where(causal, ws_ref[g], 0.0).astype(BF16)
        mixed = _dot(w, vn[:, cols]) + bias[:, g:g + 1]
        o_ref[:, cols] = (gu_ref[:, cols] * mixed).astype(o_ref.dtype)


def _split3(v):
    hi = v.astype(BF16).astype(F32)
    rest = v - hi
    mid = rest.astype(BF16).astype(F32)
    return hi, mid, rest - mid


def _pack3(v):
    hi, mid, lo = _split3(v)
    lane = lax.broadcasted_iota(jnp.int32, v.shape, 1)
    packed = jnp.where(lane < SSM_HEADS, hi,
                       jnp.where(lane < 2 * SSM_HEADS, pltpu.roll(mid, SSM_HEADS, 1),
                                 jnp.where(lane < 3 * SSM_HEADS, pltpu.roll(lo, 2 * SSM_HEADS, 1), 0.0)))
    return packed.astype(BF16)


def _ssd_init(ext_ref, state_ref, expand_ref):
    ext_ref[0:HALO_ROWS, :] = jnp.zeros((HALO_ROWS, ext_ref.shape[1]), F32)
    state_ref[...] = jnp.zeros_like(state_ref)
    k_idx = lax.broadcasted_iota(jnp.int32, (LANES, SSM_WIDTH), 0)
    j_idx = lax.broadcasted_iota(jnp.int32, (LANES, SSM_WIDTH), 1)
    hit = (k_idx < 3 * SSM_HEADS) & (j_idx // SSM_HEADDIM == k_idx % SSM_HEADS)
    expand_ref[...] = hit.astype(F32).astype(BF16)


def _ssd_conv(xs_ref, bc_ref, cw_ref, cb_ref, ext_ref, conv_ref):
    ext_ref[HALO_ROWS:, 0:SSM_WIDTH] = xs_ref[...]
    ext_ref[HALO_ROWS:, SSM_WIDTH:] = bc_ref[...]
    for c0 in range(0, conv_ref.shape[1], CONV_BLOCK):
        ch = slice(c0, c0 + CONV_BLOCK)
        acc = cb_ref[:, ch]
        for tap in range(SSM_CONV):
            off = HALO_ROWS - (SSM_CONV - 1) + tap
            acc = acc + ext_ref[off:off + CHUNK, ch] * cw_ref[tap:tap + 1, ch]
        conv_ref[:, ch] = _silu(acc)
    ext_ref[0:HALO_ROWS, :] = ext_ref[CHUNK:CHUNK + HALO_ROWS, :]


def _ssd_scan(gate_ref, dt_ref, dtb_ref, alog_ref, dskip_ref, nw_ref, o_ref,
              state_ref, y_ref, conv_ref, expand_ref):
    pre = dt_ref[...] + dtb_ref[...]
    dt = jnp.maximum(pre, 0.0) + jnp.log1p(jnp.exp(-jnp.abs(pre)))
    da = dt * (-jnp.exp(alog_ref[...]))
    row = lax.broadcasted_iota(jnp.int32, (CHUNK, CHUNK), 0)
    col = lax.broadcasted_iota(jnp.int32, (CHUNK, CHUNK), 1)
    causal = col <= row
    tri = causal.astype(F32).astype(BF16)
    da_hi, da_mid, da_lo = _split3(da)
    parts = _dot(tri, jnp.concatenate([da_hi, da_mid, da_lo], axis=1).astype(BF16))
    cum = parts[:, :LANES] + parts[:, LANES:2 * LANES] + parts[:, 2 * LANES:]
    cum_t = cum.T
    cum_last = cum[CHUNK - 1:CHUNK, :]

    packed = jnp.concatenate([_pack3(dt), _pack3(dt * jnp.exp(cum_last - cum)), _pack3(jnp.exp(cum))], axis=0)
    expanded = _dot(packed, expand_ref[...])
    dt_e = expanded[0:CHUNK]
    dt_decay_end_e = expanded[CHUNK:2 * CHUNK]
    exp_cum_e = expanded[2 * CHUNK:]
    chunk_decay_e = exp_cum_e[CHUNK - 1:CHUNK, :]

    xs = conv_ref[:, 0:SSM_WIDTH]
    xdt = (xs * dt_e).astype(BF16)
    xdt_end = (xs * dt_decay_end_e).astype(BF16)
    lane_head = lax.broadcasted_iota(jnp.int32, (1, SSM_GROUP_WIDTH), 1) // SSM_HEADDIM

    for g in range(SSM_GROUPS):
        gcols = slice(g * SSM_GROUP_WIDTH, (g + 1) * SSM_GROUP_WIDTH)
        b_g = conv_ref[:, pl.ds(SSM_WIDTH + g * SSM_STATE, SSM_STATE)].astype(BF16)
        c_g = conv_ref[:, pl.ds(SSM_WIDTH + SSM_BC_WIDTH + g * SSM_STATE, SSM_STATE)].astype(BF16)
        scores = _dot_nt(c_g, b_g)
        xdt_g = xdt[:, gcols]
        y = _dot(c_g, state_ref[g].astype(BF16)) * exp_cum_e[:, gcols]
        for r in range(SSM_HEADS_PER_GROUP):
            h = g * SSM_HEADS_PER_GROUP + r
            seg = cum[:, h:h + 1] - cum_t[h:h + 1, :]
            decay = jnp.exp(jnp.where(causal, seg, -jnp.inf))
            x_r = xdt_g * (lane_head == r).astype(F32).astype(BF16)
            y = y + _dot((scores * decay).astype(BF16), x_r)
        state_ref[g] = state_ref[g] * chunk_decay_e[:, gcols] + _dot_tn(b_g, xdt_end[:, gcols])
        y_ref[:, gcols] = y

    y = y_ref[...] + dskip_ref[...] * xs
    y = y * gate_ref[...]
    o_ref[...] = _rmsnorm(y, nw_ref[...]).astype(o_ref.dtype)


CONV_BLOCK = 512


def _mixer_kernel(gu_ref, gv_ref, vw_ref, ws_ref, bs_ref,
                  gate_ref, xs_ref, bc_ref, dt_ref, cw_ref, cb_ref, dtb_ref, alog_ref, dskip_ref, nw_ref,
                  o_ref,
                  ext_ref, state_ref, y_ref, conv_ref, expand_ref):
    @pl.when(pl.program_id(0) == 0)
    def _():
        _ssd_init(ext_ref, state_ref, expand_ref)

    _ssd_conv(xs_ref, bc_ref, cw_ref, cb_ref, ext_ref, conv_ref)
    _sgu_mix(gu_ref, gv_ref, vw_ref, ws_ref, bs_ref, o_ref.at[:, 0:GM_WIDTH])
    _ssd_scan(gate_ref, dt_ref, dtb_ref, alog_ref, dskip_ref, nw_ref, o_ref.at[:, GM_WIDTH:],
              state_ref, y_ref, conv_ref, expand_ref)


def _mixer(proj, dt_raw, v_norm_w, w_s, b_s_t, conv_w, conv_b, dt_bias, a_log, d_skip_e, norm_w):
    m = proj.shape[0]
    row = lambda a: a.reshape(1, -1)
    c0 = 2 * GM_WIDTH // SSM_WIDTH
    full = lambda shape: pl.BlockSpec(shape, lambda i: (0,) * len(shape))
    chunk = lambda width, col: pl.BlockSpec((CHUNK, width), lambda i: (i, col))
    return pl.pallas_call(
        _mixer_kernel,
        grid=(m // CHUNK,),
        in_specs=[
            chunk(GM_WIDTH, 0),
            chunk(GM_WIDTH, 1),
            full((1, GM_WIDTH)),
            full((GM_GROUPS, CHUNK, CHUNK)),
            full((CHUNK, LANES)),
            chunk(SSM_WIDTH, c0),
            chunk(SSM_WIDTH, c0 + 1),
            chunk(2 * SSM_BC_WIDTH, c0 + 2),
            chunk(LANES, 0),
            full((SSM_CONV, SSM_WIDTH + 2 * SSM_BC_WIDTH)),
            full((1, SSM_WIDTH + 2 * SSM_BC_WIDTH)),
            full((1, LANES)),
            full((1, LANES)),
            full((1, SSM_WIDTH)),
            full((1, SSM_WIDTH)),
        ],
        out_specs=chunk(GM_WIDTH + SSM_WIDTH, 0),
        out_shape=jax.ShapeDtypeStruct((m, GM_WIDTH + SSM_WIDTH), BF16),
        scratch_shapes=[pltpu.VMEM((HALO_ROWS + CHUNK, SSM_WIDTH + 2 * SSM_BC_WIDTH), F32),
                        pltpu.VMEM((SSM_GROUPS, SSM_STATE, SSM_GROUP_WIDTH), F32),
                        pltpu.VMEM((CHUNK, SSM_WIDTH), F32),
                        pltpu.VMEM((CHUNK, SSM_WIDTH + 2 * SSM_BC_WIDTH), F32),
                        pltpu.VMEM((LANES, SSM_WIDTH), BF16)],
        compiler_params=_params(1),
        name="mixer",
    )(proj, proj, row(v_norm_w), w_s, b_s_t,
      proj, proj, proj, dt_raw, conv_w, row(conv_b), row(dt_bias), row(a_log), row(d_skip_e), row(norm_w))


WEIGHT_STAGE_ROWS = 256


def _for_each_row_block(w_hbm, cols, stage_ref, sem, use):
    def body(r, _):
        rows = pl.ds(pl.multiple_of(r * WEIGHT_STAGE_ROWS, WEIGHT_STAGE_ROWS), WEIGHT_STAGE_ROWS)
        copy = pltpu.make_async_copy(w_hbm.at[rows, cols], stage_ref, sem)
        copy.start()
        copy.wait()
        use(rows)

    lax.fori_loop(0, w_hbm.shape[0] // WEIGHT_STAGE_ROWS, body, None)


def _load_as_bf16(w_hbm, w_vmem, stage_ref, sem):
    def use(rows):
        w_vmem[rows, :] = stage_ref[...].astype(BF16)

    _for_each_row_block(w_hbm, slice(None), stage_ref, sem, use)


def _project_memory(memn_ref, w_hbm, cols, out_ref, acc_ref, stage_ref, sem):
    acc_ref[...] = jnp.zeros_like(acc_ref)

    def use(rows):
        acc_ref[...] += _dot(memn_ref[:, rows], stage_ref[...].astype(BF16))

    _for_each_row_block(w_hbm, cols, stage_ref, sem, use)
    out_ref[...] = acc_ref[...].astype(BF16)


def _xattn_kernel(h_ref, mixed_ref, nw_ref, mem_ref, memw_ref, wout_hbm, wq_hbm, wkv_hbm, wo_hbm,
                  o_ref,
                  wout_ref, wq_ref, wo_ref, k_ref, v_ref, stage_ref, sem, h2_ref, q_ref, att_ref):
    d = h_ref.shape[1]

    @pl.when(pl.program_id(0) == 0)
    def _():
        _load_as_bf16(wout_hbm, wout_ref, stage_ref, sem)
        _load_as_bf16(wq_hbm, wq_ref, stage_ref, sem)
        _load_as_bf16(wo_hbm, wo_ref, stage_ref, sem)
        q_ref[...] = _rmsnorm(mem_ref[...], memw_ref[...]).astype(BF16)
        _project_memory(q_ref, wkv_hbm, pl.ds(0, d), k_ref, h2_ref, stage_ref, sem)
        _project_memory(q_ref, wkv_hbm, pl.ds(d, d), v_ref, h2_ref, stage_ref, sem)

    h2_ref[...] = h_ref[...] + _dot(mixed_ref[...], wout_ref[...])
    q_ref[...] = _dot(_rmsnorm(h2_ref[...], nw_ref[...]).astype(BF16), wq_ref[...]).astype(BF16)
    scale = XA_HEADDIM ** -0.5
    for h in range(XA_HEADS):
        cols = slice(h * XA_HEADDIM, (h + 1) * XA_HEADDIM)
        logits = _dot_nt(q_ref[:, cols], k_ref[:, cols]) * scale
        p = jnp.exp(logits - jnp.max(logits, axis=-1, keepdims=True))
        p = p / jnp.sum(p, axis=-1, keepdims=True)
        att_ref[:, cols] = _dot(p.astype(BF16), v_ref[:, cols]).astype(BF16)
    o_ref[...] = h2_ref[...] + _dot(att_ref[...], wo_ref[...])


def _xattn(h, mixed, norm_w, mem, mem_norm_w, w_out, w_q, w_kv, w_o):
    m, d = h.shape
    tm = MEM_LEN
    hbm = pl.BlockSpec(memory_space=pl.ANY)
    return pl.pallas_call(
        _xattn_kernel,
        grid=(m // tm,),
        in_specs=[
            pl.BlockSpec((tm, d), lambda i: (i, 0)),
            pl.BlockSpec((tm, mixed.shape[1]), lambda i: (i, 0)),
            pl.BlockSpec((1, d), lambda i: (0, 0)),
            pl.BlockSpec((MEM_LEN, d), lambda i: (0, 0), pipeline_mode=pl.Buffered(1)),
            pl.BlockSpec((1, d), lambda i: (0, 0)),
            hbm, hbm, hbm, hbm,
        ],
        out_specs=pl.BlockSpec((tm, d), lambda i: (i, 0)),
        out_shape=jax.ShapeDtypeStruct((m, d), F32),
        scratch_shapes=[pltpu.VMEM(w_out.shape, BF16),
                        pltpu.VMEM(w_q.shape, BF16),
                        pltpu.VMEM(w_o.shape, BF16),
                        pltpu.VMEM((MEM_LEN, d), BF16),
                        pltpu.VMEM((MEM_LEN, d), BF16),
                        pltpu.VMEM((WEIGHT_STAGE_ROWS, d), F32),
                        pltpu.SemaphoreType.DMA(()),
                        pltpu.VMEM((tm, d), F32),
                        pltpu.VMEM((tm, d), BF16),
                        pltpu.VMEM((tm, d), BF16)],
        compiler_params=_params(1),
        name="xattn",
    )(h, mixed, norm_w.reshape(1, d), mem, mem_norm_w.reshape(1, d), w_out, w_q, w_kv, w_o)


def kernel(x, mem, ffn1_norm, ffn1_w_gu, ffn1_w_down, mix_norm, w_in, gm_v_norm, gm_w_s, gm_b_s, ssm_conv_w, ssm_conv_b, ssm_dt_bias, ssm_a_log, ssm_d, ssm_norm, w_out, xa_norm, mem_norm, xa_w_q, xa_w_kv, xa_w_o, ffn2_norm, ffn2_w_gu, ffn2_w_down, final_norm):
    pad_lanes =lambda a: jnp.pad(a, ((0, 0), (0, LANES - a.shape[1])))
    h = x[0]
    for i in range(ffn1_norm.shape[0]):
        h, xn = _ffn(h, ffn1_norm[i], ffn1_w_gu[i], ffn1_w_down[i], mix_norm[i], tail="next")

        proj, dt_raw = _in_proj(xn, jnp.swapaxes(w_in[i], 0, 1))
        mixed = _mixer(proj, dt_raw, gm_v_norm[i], gm_w_s[i], pad_lanes(gm_b_s[i].T),
                       ssm_conv_w[i], ssm_conv_b[i],
                       pad_lanes(ssm_dt_bias[i][None])[0], pad_lanes(ssm_a_log[i][None])[0],
                       jnp.repeat(ssm_d[i], SSM_HEADDIM), ssm_norm[i])
        h = _xattn(h, mixed, xa_norm[i], mem[0], mem_norm[i], w_out[i], xa_w_q[i], xa_w_kv[i], xa_w_o[i])

        last = i == ffn1_norm.shape[0] - 1
        h = _ffn(h, ffn2_norm[i], ffn2_w_gu[i], ffn2_w_down[i], final_norm, tail="final" if last else "none")[0]
    return h[None]
```

```python
import functools

import jax
import jax.numpy as jnp
from jax import lax
from jax.experimental import pallas as pl
from jax.experimental.pallas import tpu as pltpu

D_MODEL = 2048
SEQ = 8192
MEM_LEN = 256
GM_WIDTH = 2048
GM_GROUPS = 4
GM_GROUP_WIDTH = GM_WIDTH // GM_GROUPS
CHUNK = 128
SSM_WIDTH = 2048
SSM_HEADDIM = 64
SSM_HEADS = SSM_WIDTH // SSM_HEADDIM
SSM_GROUPS = 8
SSM_HEADS_PER_GROUP = SSM_HEADS // SSM_GROUPS
SSM_GROUP_WIDTH = SSM_HEADS_PER_GROUP * SSM_HEADDIM
SSM_STATE = 128
SSM_CONV = 4
SSM_BC_WIDTH = SSM_GROUPS * SSM_STATE
PROJ_MAIN = 2 * GM_WIDTH + SSM_WIDTH + SSM_WIDTH + 2 * SSM_BC_WIDTH
XA_HEADS = 4
XA_HEADDIM = D_MODEL // XA_HEADS
D_FF = 5632
EPS = 1e-6

LANES = 128
HALO_ROWS = 8
ROW_BLOCK = 128
IN_PROJ_ROW_BLOCK = 256
VMEM_LIMIT = 60 * 1024 * 1024

BF16 = jnp.bfloat16
F32 = jnp.float32


def _params(n_axes):
    return pltpu.CompilerParams(dimension_semantics=("arbitrary",) * n_axes,
                                vmem_limit_bytes=VMEM_LIMIT)


def _rmsnorm(x, w):
    return x * lax.rsqrt(jnp.mean(x * x, axis=-1, keepdims=True) + EPS) * w


def _sigmoid(x):
    return 1.0 / (1.0 + jnp.exp(-x))


def _silu(x):
    return x * _sigmoid(x)


def _gelu_tanh(x):
    c = 0.7978845608028654
    return 0.5 * x * (1.0 + jnp.tanh(x * (c + (0.044715 * c) * (x * x))))


def _dot(a, b):
    return jnp.dot(a, b, preferred_element_type=F32)


def _dot_nt(a, b):
    return lax.dot_general(a, b, (((1,), (1,)), ((), ())), preferred_element_type=F32)


def _dot_tn(a, b):
    return lax.dot_general(a, b, (((0,), (0,)), ((), ())), preferred_element_type=F32)


def _ffn_kernel(x_ref, nw_ref, wg_ref, wu_ref, wd_ref, tw_ref, o_ref, *rest, tail):
    xn_ref = rest[-1]
    j = pl.program_id(1)
    n_row_blocks = x_ref.shape[0] // ROW_BLOCK

    def row_block(r):
        return pl.ds(pl.multiple_of(r * ROW_BLOCK, ROW_BLOCK), ROW_BLOCK)

    @pl.when(j == 0)
    def _():
        def body(r, _):
            rows = row_block(r)
            xn_ref[rows, :] = _rmsnorm(x_ref[rows, :], nw_ref[...]).astype(BF16)
            o_ref[rows, :] = jnp.zeros((ROW_BLOCK, o_ref.shape[1]), F32)
        lax.fori_loop(0, n_row_blocks, body, None)

    xn = xn_ref[...]
    g = _dot(xn, wg_ref[...].astype(BF16))
    u = _dot(xn, wu_ref[...].astype(BF16))
    o_ref[...] += _dot((_silu(g) * u).astype(BF16), wd_ref[...].astype(BF16))

    @pl.when(j == pl.num_programs(1) - 1)
    def _():
        def body(r, _):
            rows = row_block(r)
            h = x_ref[rows, :] + 0.5 * o_ref[rows, :]
            if tail == "final":
                h = _rmsnorm(h, tw_ref[...])
            o_ref[rows, :] = h
            if tail == "next":
                rest[0][rows, :] = _rmsnorm(h, tw_ref[...]).astype(BF16)
        lax.fori_loop(0, n_row_blocks, body, None)


def _ffn(x, norm_w, w_gu, w_down, tail_w, *, tail, tm=1024, tf=256):
    m, d = x.shape
    f = w_down.shape[0]
    nf = f // tf
    row_tile = pl.BlockSpec((tm, d), lambda i, j: (i, 0))
    out_specs, out_shape = [row_tile], [jax.ShapeDtypeStruct((m, d), F32)]
    if tail == "next":
        out_specs.append(row_tile)
        out_shape.append(jax.ShapeDtypeStruct((m, d), BF16))
    return pl.pallas_call(
        functools.partial(_ffn_kernel, tail=tail),
        grid=(m // tm, nf),
        in_specs=[
            row_tile,
            pl.BlockSpec((1, d), lambda i, j: (0, 0)),
            pl.BlockSpec((d, tf), lambda i, j: (0, j)),
            pl.BlockSpec((d, tf), lambda i, j: (0, j + nf)),
            pl.BlockSpec((tf, d), lambda i, j: (j, 0)),
            pl.BlockSpec((1, d), lambda i, j: (0, 0)),
        ],
        out_specs=out_specs,
        out_shape=out_shape,
        scratch_shapes=[pltpu.VMEM((tm, d), BF16)],
        compiler_params=_params(2),
        name="ffn",
    )(x, norm_w.reshape(1, d), w_gu, w_gu, w_down, tail_w.reshape(1, d))


def _in_proj_kernel(xn_ref, wt_ref, wdt_ref, o_ref, dt_ref, *, gelu_tiles, silu_tiles):
    j = pl.program_id(1)

    @pl.when(j == 0)
    def _():
        dt_ref[...] = _dot_nt(xn_ref[...], wdt_ref[...].astype(BF16))

    def project(activation):
        w = wt_ref[...].astype(BF16)
        for r0 in range(0, o_ref.shape[0], IN_PROJ_ROW_BLOCK):
            rows = slice(r0, r0 + IN_PROJ_ROW_BLOCK)
            o_ref[rows, :] = activation(_dot_nt(xn_ref[rows, :], w))

    @pl.when(j < gelu_tiles)
    def _():
        project(_gelu_tanh)

    @pl.when((j >= gelu_tiles) & (j < gelu_tiles + silu_tiles))
    def _():
        project(_silu)

    @pl.when(j >= gelu_tiles + silu_tiles)
    def _():
        project(lambda acc: acc)


def _in_proj(xn, w_in_t, *, tm=2048, tn=512):
    m, k = xn.shape
    kernel_fn = functools.partial(_in_proj_kernel, gelu_tiles=2 * GM_WIDTH // tn, silu_tiles=SSM_WIDTH // tn)
    w_dt_t = jnp.pad(w_in_t[PROJ_MAIN:], ((0, LANES - (w_in_t.shape[0] - PROJ_MAIN)), (0, 0)))
    return pl.pallas_call(
        kernel_fn,
        grid=(m // tm, PROJ_MAIN // tn),
        in_specs=[
            pl.BlockSpec((tm, k), lambda i, j: (i, 0)),
            pl.BlockSpec((tn, k), lambda i, j: (j, 0)),
            pl.BlockSpec((LANES, k), lambda i, j: (0, 0)),
        ],
        out_specs=[pl.BlockSpec((tm, tn), lambda i, j: (i, j)),
                   pl.BlockSpec((tm, LANES), lambda i, j: (i, 0))],
        out_shape=[jax.ShapeDtypeStruct((m, PROJ_MAIN), F32), jax.ShapeDtypeStruct((m, LANES), F32)],
        compiler_params=_params(2),
        name="in_proj",
    )(xn, w_in_t, w_dt_t)


def _sgu_mix(gu_ref, gv_ref, vw_ref, ws_ref, bs_ref, o_ref):
    vn = _rmsnorm(gv_ref[...], vw_ref[...]).astype(BF16)
    t_idx = lax.broadcasted_iota(jnp.int32, (CHUNK, CHUNK), 0)
    s_idx = lax.broadcasted_iota(jnp.int32, (CHUNK, CHUNK), 1)
    causal = s_idx <= t_idx
    bias = bs_ref[...]
    for g in range(GM_GROUPS):
        cols = slice(g * GM_GROUP_WIDTH, (g + 1) * GM_GROUP_WIDTH)
        w = jnp.where(causal, ws_ref[g], 0.0).astype(BF16)
        mixed = _dot(w, vn[:, cols]) + bias[:, g:g + 1]
        o_ref[:, cols] = (gu_ref[:, cols] * mixed).astype(o_ref.dtype)


def _split3(v):
    hi = v.astype(BF16).astype(F32)
    rest = v - hi
    mid = rest.astype(BF16).astype(F32)
    return hi, mid, rest - mid


def _pack3(v):
    hi, mid, lo = _split3(v)
    lane = lax.broadcasted_iota(jnp.int32, v.shape, 1)
    packed = jnp.where(lane < SSM_HEADS, hi,
                       jnp.where(lane < 2 * SSM_HEADS, pltpu.roll(mid, SSM_HEADS, 1),
                                 jnp.where(lane < 3 * SSM_HEADS, pltpu.roll(lo, 2 * SSM_HEADS, 1), 0.0)))
    return packed.astype(BF16)


def _ssd_init(ext_ref, state_ref, expand_ref):
    ext_ref[0:HALO_ROWS, :] = jnp.zeros((HALO_ROWS, ext_ref.shape[1]), F32)
    state_ref[...] = jnp.zeros_like(state_ref)
    k_idx = lax.broadcasted_iota(jnp.int32, (LANES, SSM_WIDTH), 0)
    j_idx = lax.broadcasted_iota(jnp.int32, (LANES, SSM_WIDTH), 1)
    hit = (k_idx < 3 * SSM_HEADS) & (j_idx // SSM_HEADDIM == k_idx % SSM_HEADS)
    expand_ref[...] = hit.astype(F32).astype(BF16)


def _ssd_conv(xs_ref, bc_ref, cw_ref, cb_ref, ext_ref, conv_ref):
    ext_ref[HALO_ROWS:, 0:SSM_WIDTH] = xs_ref[...]
    ext_ref[HALO_ROWS:, SSM_WIDTH:] = bc_ref[...]
    for c0 in range(0, conv_ref.shape[1], CONV_BLOCK):
        ch = slice(c0, c0 + CONV_BLOCK)
        acc = cb_ref[:, ch]
        for tap in range(SSM_CONV):
            off = HALO_ROWS - (SSM_CONV - 1) + tap
            acc = acc + ext_ref[off:off + CHUNK, ch] * cw_ref[tap:tap + 1, ch]
        conv_ref[:, ch] = _silu(acc)
    ext_ref[0:HALO_ROWS, :] = ext_ref[CHUNK:CHUNK + HALO_ROWS, :]


def _ssd_scan(gate_ref, dt_ref, dtb_ref, alog_ref, dskip_ref, nw_ref, o_ref,
              state_ref, y_ref, conv_ref, expand_ref):
    pre = dt_ref[...] + dtb_ref[...]
    dt = jnp.maximum(pre, 0.0) + jnp.log1p(jnp.exp(-jnp.abs(pre)))
    da = dt * (-jnp.exp(alog_ref[...]))
    row = lax.broadcasted_iota(jnp.int32, (CHUNK, CHUNK), 0)
    col = lax.broadcasted_iota(jnp.int32, (CHUNK, CHUNK), 1)
    causal = col <= row
    tri = causal.astype(F32).astype(BF16)
    da_hi, da_mid, da_lo = _split3(da)
    parts = _dot(tri, jnp.concatenate([da_hi, da_mid, da_lo], axis=1).astype(BF16))
    cum = parts[:, :LANES] + parts[:, LANES:2 * LANES] + parts[:, 2 * LANES:]
    cum_t = cum.T
    cum_last = cum[CHUNK - 1:CHUNK, :]

    packed = jnp.concatenate([_pack3(dt), _pack3(dt * jnp.exp(cum_last - cum)), _pack3(jnp.exp(cum))], axis=0)
    expanded = _dot(packed, expand_ref[...])
    dt_e = expanded[0:CHUNK]
    dt_decay_end_e = expanded[CHUNK:2 * CHUNK]
    exp_cum_e = expanded[2 * CHUNK:]
    chunk_decay_e = exp_cum_e[CHUNK - 1:CHUNK, :]

    xs = conv_ref[:, 0:SSM_WIDTH]
    xdt = (xs * dt_e).astype(BF16)
    xdt_end = (xs * dt_decay_end_e).astype(BF16)
    lane_head = lax.broadcasted_iota(jnp.int32, (1, SSM_GROUP_WIDTH), 1) // SSM_HEADDIM

    for g in range(SSM_GROUPS):
        gcols = slice(g * SSM_GROUP_WIDTH, (g + 1) * SSM_GROUP_WIDTH)
        b_g = conv_ref[:, pl.ds(SSM_WIDTH + g * SSM_STATE, SSM_STATE)].astype(BF16)
        c_g = conv_ref[:, pl.ds(SSM_WIDTH + SSM_BC_WIDTH + g * SSM_STATE, SSM_STATE)].astype(BF16)
        scores = _dot_nt(c_g, b_g)
        xdt_g = xdt[:, gcols]
        y = _dot(c_g, state_ref[g].astype(BF16)) * exp_cum_e[:, gcols]
        for r in range(SSM_HEADS_PER_GROUP):
            h = g * SSM_HEADS_PER_GROUP + r
            seg = cum[:, h:h + 1] - cum_t[h:h + 1, :]
            decay = jnp.exp(jnp.where(causal, seg, -jnp.inf))
            x_r = xdt_g * (lane_head == r).astype(F32).astype(BF16)
            y = y + _dot((scores * decay).astype(BF16), x_r)
        state_ref[g] = state_ref[g] * chunk_decay_e[:, gcols] + _dot_tn(b_g, xdt_end[:, gcols])
        y_ref[:, gcols] = y

    y = y_ref[...] + dskip_ref[...] * xs
    y = y * gate_ref[...]
    o_ref[...] = _rmsnorm(y, nw_ref[...]).astype(o_ref.dtype)


CONV_BLOCK = 512


def _mixer_kernel(gu_ref, gv_ref, vw_ref, ws_ref, bs_ref,
                  gate_ref, xs_ref, bc_ref, dt_ref, cw_ref, cb_ref, dtb_ref, alog_ref, dskip_ref, nw_ref,
                  o_ref,
                  ext_ref, state_ref, y_ref, conv_ref, expand_ref):
    @pl.when(pl.program_id(0) == 0)
    def _():
        _ssd_init(ext_ref, state_ref, expand_ref)

    _ssd_conv(xs_ref, bc_ref, cw_ref, cb_ref, ext_ref, conv_ref)
    _sgu_mix(gu_ref, gv_ref, vw_ref, ws_ref, bs_ref, o_ref.at[:, 0:GM_WIDTH])
    _ssd_scan(gate_ref, dt_ref, dtb_ref, alog_ref, dskip_ref, nw_ref, o_ref.at[:, GM_WIDTH:],
              state_ref, y_ref, conv_ref, expand_ref)


def _mixer(proj, dt_raw, v_norm_w, w_s, b_s_t, conv_w, conv_b, dt_bias, a_log, d_skip_e, norm_w):
    m = proj.shape[0]
    row = lambda a: a.reshape(1, -1)
    c0 = 2 * GM_WIDTH // SSM_WIDTH
    full = lambda shape: pl.BlockSpec(shape, lambda i: (0,) * len(shape))
    chunk = lambda width, col: pl.BlockSpec((CHUNK, width), lambda i: (i, col))
    return pl.pallas_call(
        _mixer_kernel,
        grid=(m // CHUNK,),
        in_specs=[
            chunk(GM_WIDTH, 0),
            chunk(GM_WIDTH, 1),
            full((1, GM_WIDTH)),
            full((GM_GROUPS, CHUNK, CHUNK)),
            full((CHUNK, LANES)),
            chunk(SSM_WIDTH, c0),
            chunk(SSM_WIDTH, c0 + 1),
            chunk(2 * SSM_BC_WIDTH, c0 + 2),
            chunk(LANES, 0),
            full((SSM_CONV, SSM_WIDTH + 2 * SSM_BC_WIDTH)),
            full((1, SSM_WIDTH + 2 * SSM_BC_WIDTH)),
            full((1, LANES)),
            full((1, LANES)),
            full((1, SSM_WIDTH)),
            full((1, SSM_WIDTH)),
        ],
        out_specs=chunk(GM_WIDTH + SSM_WIDTH, 0),
        out_shape=jax.ShapeDtypeStruct((m, GM_WIDTH + SSM_WIDTH), BF16),
        scratch_shapes=[pltpu.VMEM((HALO_ROWS + CHUNK, SSM_WIDTH + 2 * SSM_BC_WIDTH), F32),
                        pltpu.VMEM((SSM_GROUPS, SSM_STATE, SSM_GROUP_WIDTH), F32),
                        pltpu.VMEM((CHUNK, SSM_WIDTH), F32),
                        pltpu.VMEM((CHUNK, SSM_WIDTH + 2 * SSM_BC_WIDTH), F32),
                        pltpu.VMEM((LANES, SSM_WIDTH), BF16)],
        compiler_params=_params(1),
        name="mixer",
    )(proj, proj, row(v_norm_w), w_s, b_s_t,
      proj, proj, proj, dt_raw, conv_w, row(conv_b), row(dt_bias), row(a_log), row(d_skip_e), row(norm_w))


WEIGHT_STAGE_ROWS = 256


def _for_each_row_block(w_hbm, cols, stage_ref, sems, use):
    n_blocks = w_hbm.shape[0] // WEIGHT_STAGE_ROWS

    def block_rows(r):
        return pl.ds(pl.multiple_of(r * WEIGHT_STAGE_ROWS, WEIGHT_STAGE_ROWS), WEIGHT_STAGE_ROWS)

    def copy(r, slot):
        return pltpu.make_async_copy(w_hbm.at[block_rows(r), cols], stage_ref.at[slot], sems.at[slot])

    copy(0, 0).start()

    def body(r, _):
        slot = r % 2

        @pl.when(r + 1 < n_blocks)
        def _():
            copy(r + 1, 1 - slot).start()

        copy(r, slot).wait()
        use(block_rows(r), stage_ref.at[slot])

    lax.fori_loop(0, n_blocks, body, None)


def _load_as_bf16(w_hbm, w_vmem, stage_ref, sems):
    def use(rows, block_ref):
        w_vmem[rows, :] = block_ref[...].astype(BF16)

    _for_each_row_block(w_hbm, slice(None), stage_ref, sems, use)


def _project_memory(memn_ref, w_hbm, cols, out_ref, acc_ref, stage_ref, sems):
    acc_ref[...] = jnp.zeros_like(acc_ref)

    def use(rows, block_ref):
        acc_ref[...] += _dot(memn_ref[:, rows], block_ref[...].astype(BF16))

    _for_each_row_block(w_hbm, cols, stage_ref, sems, use)
    out_ref[...] = acc_ref[...].astype(BF16)


def _xattn_kernel(h_ref, mixed_ref, nw_ref, mem_ref, memw_ref, wout_hbm, wq_hbm, wkv_hbm, wo_hbm,
                  o_ref,
                  wout_ref, wq_ref, wo_ref, k_ref, v_ref, stage_ref, sems, h2_ref, q_ref, att_ref):
    d = h_ref.shape[1]

    @pl.when(pl.program_id(0) == 0)
    def _():
        _load_as_bf16(wout_hbm, wout_ref, stage_ref, sems)
        _load_as_bf16(wq_hbm, wq_ref, stage_ref, sems)
        _load_as_bf16(wo_hbm, wo_ref, stage_ref, sems)
        q_ref[...] = _rmsnorm(mem_ref[...], memw_ref[...]).astype(BF16)
        _project_memory(q_ref, wkv_hbm, pl.ds(0, d), k_ref, h2_ref, stage_ref, sems)
        _project_memory(q_ref, wkv_hbm, pl.ds(d, d), v_ref, h2_ref, stage_ref, sems)

    h2_ref[...] = h_ref[...] + _dot(mixed_ref[...], wout_ref[...])
    q_ref[...] = _dot(_rmsnorm(h2_ref[...], nw_ref[...]).astype(BF16), wq_ref[...]).astype(BF16)
    scale = XA_HEADDIM ** -0.5
    for h in range(XA_HEADS):
        cols = slice(h * XA_HEADDIM, (h + 1) * XA_HEADDIM)
        logits = _dot_nt(q_ref[:, cols], k_ref[:, cols]) * scale
        p = jnp.exp(logits - jnp.max(logits, axis=-1, keepdims=True))
        p = p / jnp.sum(p, axis=-1, keepdims=True)
        att_ref[:, cols] = _dot(p.astype(BF16), v_ref[:, cols]).astype(BF16)
    o_ref[...] = h2_ref[...] + _dot(att_ref[...], wo_ref[...])


def _xattn(h, mixed, norm_w, mem, mem_norm_w, w_out, w_q, w_kv, w_o):
    m, d = h.shape
    tm = MEM_LEN
    hbm = pl.BlockSpec(memory_space=pl.ANY)
    return pl.pallas_call(
        _xattn_kernel,
        grid=(m // tm,),
        in_specs=[
            pl.BlockSpec((tm, d), lambda i: (i, 0)),
            pl.BlockSpec((tm, mixed.shape[1]), lambda i: (i, 0)),
            pl.BlockSpec((1, d), lambda i: (0, 0)),
            pl.BlockSpec((MEM_LEN, d), lambda i: (0, 0), pipeline_mode=pl.Buffered(1)),
            pl.BlockSpec((1, d), lambda i: (0, 0)),
            hbm, hbm, hbm, hbm,
        ],
        out_specs=pl.BlockSpec((tm, d), lambda i: (i, 0)),
        out_shape=jax.ShapeDtypeStruct((m, d), F32),
        scratch_shapes=[pltpu.VMEM(w_out.shape, BF16),
                        pltpu.VMEM(w_q.shape, BF16),
                        pltpu.VMEM(w_o.shape, BF16),
                        pltpu.VMEM((MEM_LEN, d), BF16),
                        pltpu.VMEM((MEM_LEN, d), BF16),
                        pltpu.VMEM((2, WEIGHT_STAGE_ROWS, d), F32),
                        pltpu.SemaphoreType.DMA((2,)),
                        pltpu.VMEM((tm, d), F32),
                        pltpu.VMEM((tm, d), BF16),
                        pltpu.VMEM((tm, d), BF16)],
        compiler_params=_params(1),
        name="xattn",
    )(h, mixed, norm_w.reshape(1, d), mem, mem_norm_w.reshape(1, d), w_out, w_q, w_kv, w_o)


def kernel(x, mem, ffn1_norm, ffn1_w_gu, ffn1_w_down, mix_norm, w_in, gm_v_norm, gm_w_s, gm_b_s, ssm_conv_w, ssm_conv_b, ssm_dt_bias, ssm_a_log, ssm_d, ssm_norm, w_out, xa_norm, mem_norm, xa_w_q, xa_w_kv, xa_w_o, ffn2_norm, ffn2_w_gu, ffn2_w_down, final_norm):
    pad_lanes =lambda a: jnp.pad(a, ((0, 0), (0, LANES - a.shape[1])))
    h = x[0]
    for i in range(ffn1_norm.shape[0]):
        h, xn = _ffn(h, ffn1_norm[i], ffn1_w_gu[i], ffn1_w_down[i], mix_norm[i], tail="next")

        proj, dt_raw = _in_proj(xn, jnp.swapaxes(w_in[i], 0, 1))
        mixed = _mixer(proj, dt_raw, gm_v_norm[i], gm_w_s[i], pad_lanes(gm_b_s[i].T),
                       ssm_conv_w[i], ssm_conv_b[i],
                       pad_lanes(ssm_dt_bias[i][None])[0], pad_lanes(ssm_a_log[i][None])[0],
                       jnp.repeat(ssm_d[i], SSM_HEADDIM), ssm_norm[i])
        h = _xattn(h, mixed, xa_norm[i], mem[0], mem_norm[i], w_out[i], xa_w_q[i], xa_w_kv[i], xa_w_o[i])

        last = i == ffn1_norm.shape[0] - 1
        h = _ffn(h, ffn2_norm[i], ffn2_w_gu[i], ffn2_w_down[i], final_norm, tail="final" if last else "none")[0]
    return h[None]
```

```python
import functools

import jax
import jax.numpy as jnp
from jax import lax
from jax.experimental import pallas as pl
from jax.experimental.pallas import tpu as pltpu

D_MODEL = 2048
SEQ = 8192
MEM_LEN = 256
GM_WIDTH = 2048
GM_GROUPS = 4
GM_GROUP_WIDTH = GM_WIDTH // GM_GROUPS
CHUNK = 128
SSM_WIDTH = 2048
SSM_HEADDIM = 64
SSM_HEADS = SSM_WIDTH // SSM_HEADDIM
SSM_GROUPS = 8
SSM_HEADS_PER_GROUP = SSM_HEADS // SSM_GROUPS
SSM_GROUP_WIDTH = SSM_HEADS_PER_GROUP * SSM_HEADDIM
SSM_STATE = 128
SSM_CONV = 4
SSM_BC_WIDTH = SSM_GROUPS * SSM_STATE
PROJ_MAIN = 2 * GM_WIDTH + SSM_WIDTH + SSM_WIDTH + 2 * SSM_BC_WIDTH
XA_HEADS = 4
XA_HEADDIM = D_MODEL // XA_HEADS
D_FF = 5632
EPS = 1e-6

LANES = 128
HALO_ROWS = 8
ROW_BLOCK = 128
IN_PROJ_ROW_BLOCK = 256
VMEM_LIMIT = 60 * 1024 * 1024

BF16 = jnp.bfloat16
F32 = jnp.float32


def _params(n_axes):
    return pltpu.CompilerParams(dimension_semantics=("arbitrary",) * n_axes,
                                vmem_limit_bytes=VMEM_LIMIT)


def _rmsnorm(x, w):
    return x * lax.rsqrt(jnp.mean(x * x, axis=-1, keepdims=True) + EPS) * w


def _sigmoid(x):
    return 1.0 / (1.0 + jnp.exp(-x))


def _silu(x):
    return x * _sigmoid(x)


def _gelu_tanh(x):
    c = 0.7978845608028654
    return 0.5 * x * (1.0 + jnp.tanh(x * (c + (0.044715 * c) * (x * x))))


def _dot(a, b):
    return jnp.dot(a, b, preferred_element_type=F32)


def _dot_nt(a, b):
    return lax.dot_general(a, b, (((1,), (1,)), ((), ())), preferred_element_type=F32)


def _dot_tn(a, b):
    return lax.dot_general(a, b, (((0,), (0,)), ((), ())), preferred_element_type=F32)


def _ffn_kernel(x_ref, nw_ref, wg_ref, wu_ref, wd_ref, tw_ref, o_ref, *rest, tail):
    xn_ref = rest[-1]
    j = pl.program_id(1)
    n_row_blocks = x_ref.shape[0] // ROW_BLOCK

    def row_block(r):
        return pl.ds(pl.multiple_of(r * ROW_BLOCK, ROW_BLOCK), ROW_BLOCK)

    @pl.when(j == 0)
    def _():
        def body(r, _):
            rows = row_block(r)
            xn_ref[rows, :] = _rmsnorm(x_ref[rows, :], nw_ref[...]).astype(BF16)
            o_ref[rows, :] = jnp.zeros((ROW_BLOCK, o_ref.shape[1]), F32)
        lax.fori_loop(0, n_row_blocks, body, None)

    xn = xn_ref[...]
    g = _dot(xn, wg_ref[...].astype(BF16))
    u = _dot(xn, wu_ref[...].astype(BF16))
    o_ref[...] += _dot((_silu(g) * u).astype(BF16), wd_ref[...].astype(BF16))

    @pl.when(j == pl.num_programs(1) - 1)
    def _():
        def body(r, _):
            rows = row_block(r)
            h = x_ref[rows, :] + 0.5 * o_ref[rows, :]
            if tail == "final":
                h = _rmsnorm(h, tw_ref[...])
            o_ref[rows, :] = h
            if tail == "next":
                rest[0][rows, :] = _rmsnorm(h, tw_ref[...]).astype(BF16)
        lax.fori_loop(0, n_row_blocks, body, None)


def _ffn(x, norm_w, w_gu, w_down, tail_w, *, tail, tm=1024, tf=256):
    m, d = x.shape
    f = w_down.shape[0]
    nf = f // tf
    row_tile = pl.BlockSpec((tm, d), lambda i, j: (i, 0))
    out_specs, out_shape = [row_tile], [jax.ShapeDtypeStruct((m, d), F32)]
    if tail == "next":
        out_specs.append(row_tile)
        out_shape.append(jax.ShapeDtypeStruct((m, d), BF16))
    return pl.pallas_call(
        functools.partial(_ffn_kernel, tail=tail),
        grid=(m // tm, nf),
        in_specs=[
            row_tile,
            pl.BlockSpec((1, d), lambda i, j: (0, 0)),
            pl.BlockSpec((d, tf), lambda i, j: (0, j)),
            pl.BlockSpec((d, tf), lambda i, j: (0, j + nf)),
            pl.BlockSpec((tf, d), lambda i, j: (j, 0)),
            pl.BlockSpec((1, d), lambda i, j: (0, 0)),
        ],
        out_specs=out_specs,
        out_shape=out_shape,
        scratch_shapes=[pltpu.VMEM((tm, d), BF16)],
        compiler_params=_params(2),
        name="ffn",
    )(x, norm_w.reshape(1, d), w_gu, w_gu, w_down, tail_w.reshape(1, d))


def _in_proj_kernel(xn_ref, wt_ref, wdt_ref, cw_ref, cb_ref, o_ref, dt_ref, raw_even_ref, raw_odd_ref, halo_ref,
                    *, gelu_tiles, silu_tiles):
    i = pl.program_id(0)
    j = pl.program_id(1)
    conv_first = gelu_tiles + silu_tiles
    raw_refs = (raw_even_ref, raw_odd_ref)

    @pl.when(j == 0)
    def _():
        dt_ref[...] = _dot_nt(xn_ref[...], wdt_ref[...].astype(BF16))

    tm = o_ref.shape[0]

    def project(consume):
        w = wt_ref[...].astype(BF16)
        product = lambda r0: _dot_nt(xn_ref[r0:r0 + IN_PROJ_ROW_BLOCK, :], w)
        ahead = product(0)
        for r0 in range(0, tm, IN_PROJ_ROW_BLOCK):
            current = ahead
            if r0 + IN_PROJ_ROW_BLOCK < tm:
                ahead = product(r0 + IN_PROJ_ROW_BLOCK)
            consume(r0, current)

    def store_with(activation):
        def consume(r0, acc):
            o_ref[r0:r0 + IN_PROJ_ROW_BLOCK, :] = activation(acc)
        return consume

    @pl.when(j < gelu_tiles)
    def _():
        project(store_with(_gelu_tanh))

    @pl.when((j >= gelu_tiles) & (j < gelu_tiles + silu_tiles))
    def _():
        project(store_with(_silu))

    @pl.when(j >= conv_first)
    def _():
        c = j - conv_first
        tail = slice(IN_PROJ_ROW_BLOCK, IN_PROJ_ROW_BLOCK + HALO_ROWS)
        n_blocks = tm // IN_PROJ_ROW_BLOCK

        @pl.when(i == 0)
        def _():
            raw_refs[0][0:HALO_ROWS, :] = jnp.zeros((HALO_ROWS, o_ref.shape[1]), F32)

        @pl.when(i > 0)
        def _():
            raw_refs[0][0:HALO_ROWS, :] = halo_ref[c]

        def conv_silu(r0, acc):
            block = r0 // IN_PROJ_ROW_BLOCK
            raw, previous = raw_refs[block % 2], raw_refs[(block - 1) % 2]
            if block > 0:
                raw[0:HALO_ROWS, :] = previous[tail, :]
            raw[HALO_ROWS:, :] = acc
            out = cb_ref[...]
            for tap in range(SSM_CONV):
                off = HALO_ROWS - (SSM_CONV - 1) + tap
                out = out + raw[off:off + IN_PROJ_ROW_BLOCK, :] * cw_ref[tap:tap + 1, :]
            o_ref[r0:r0 + IN_PROJ_ROW_BLOCK, :] = _silu(out)

        project(conv_silu)
        halo_ref[c] = raw_refs[(n_blocks - 1) % 2][tail, :]


def _in_proj(xn, w_in_t, conv_w, conv_b, *, tm=2048, tn=512):
    m, k = xn.shape
    gelu_tiles, silu_tiles = 2 * GM_WIDTH // tn, SSM_WIDTH // tn
    conv_tiles = (SSM_WIDTH + 2 * SSM_BC_WIDTH) // tn
    kernel_fn = functools.partial(_in_proj_kernel, gelu_tiles=gelu_tiles, silu_tiles=silu_tiles)
    conv_tile = lambda i, j: (0, jnp.maximum(j - (gelu_tiles + silu_tiles), 0))
    w_dt_t = jnp.pad(w_in_t[PROJ_MAIN:], ((0, LANES - (w_in_t.shape[0] - PROJ_MAIN)), (0, 0)))
    return pl.pallas_call(
        kernel_fn,
        grid=(m // tm, PROJ_MAIN // tn),
        in_specs=[
            pl.BlockSpec((tm, k), lambda i, j: (i, 0)),
            pl.BlockSpec((tn, k), lambda i, j: (j, 0)),
            pl.BlockSpec((LANES, k), lambda i, j: (0, 0)),
            pl.BlockSpec((SSM_CONV, tn), conv_tile),
            pl.BlockSpec((1, tn), conv_tile),
        ],
        out_specs=[pl.BlockSpec((tm, tn), lambda i, j: (i, j)),
                   pl.BlockSpec((tm, LANES), lambda i, j: (i, 0))],
        out_shape=[jax.ShapeDtypeStruct((m, PROJ_MAIN), F32), jax.ShapeDtypeStruct((m, LANES), F32)],
        scratch_shapes=[pltpu.VMEM((HALO_ROWS + IN_PROJ_ROW_BLOCK, tn), F32),
                        pltpu.VMEM((HALO_ROWS + IN_PROJ_ROW_BLOCK, tn), F32),
                        pltpu.VMEM((conv_tiles, HALO_ROWS, tn), F32)],
        compiler_params=_params(2),
        name="in_proj",
    )(xn, w_in_t, w_dt_t, conv_w, conv_b.reshape(1, -1))


def _sgu_mix(gu_ref, gv_ref, vw_ref, ws_ref, bs_ref, o_ref):
    vn = _rmsnorm(gv_ref[...], vw_ref[...]).astype(BF16)
    t_idx = lax.broadcasted_iota(jnp.int32, (CHUNK, CHUNK), 0)
    s_idx = lax.broadcasted_iota(jnp.int32, (CHUNK, CHUNK), 1)
    causal = s_idx <= t_idx
    bias = bs_ref[...]
    for g in range(GM_GROUPS):
        cols = slice(g * GM_GROUP_WIDTH, (g + 1) * GM_GROUP_WIDTH)
        w = jnp.where(causal, ws_ref[g], 0.0).astype(BF16)
        mixed = _dot(w, vn[:, cols]) + bias[:, g:g + 1]
        o_ref[:, cols] = (gu_ref[:, cols] * mixed).astype(o_ref.dtype)


def _split3(v):
    hi = v.astype(BF16).astype(F32)
    rest = v - hi
    mid = rest.astype(BF16).astype(F32)
    return hi, mid, rest - mid


def _pack3(v):
    hi, mid, lo = _split3(v)
    lane = lax.broadcasted_iota(jnp.int32, v.shape, 1)
    packed = jnp.where(lane < SSM_HEADS, hi,
                       jnp.where(lane < 2 * SSM_HEADS, pltpu.roll(mid, SSM_HEADS, 1),
                                 jnp.where(lane < 3 * SSM_HEADS, pltpu.roll(lo, 2 * SSM_HEADS, 1), 0.0)))
    return packed.astype(BF16)


def _ssd_init(state_ref, expand_ref):
    state_ref[...] = jnp.zeros_like(state_ref)
    k_idx = lax.broadcasted_iota(jnp.int32, (LANES, SSM_WIDTH), 0)
    j_idx = lax.broadcasted_iota(jnp.int32, (LANES, SSM_WIDTH), 1)
    hit = (k_idx < 3 * SSM_HEADS) & (j_idx // SSM_HEADDIM == k_idx % SSM_HEADS)
    expand_ref[...] = hit.astype(F32).astype(BF16)


def _ssd_scan(gate_ref, xs_ref, bc_ref, dt_ref, dtb_ref, alog_ref, dskip_ref, nw_ref, o_ref,
              state_ref, y_ref, expand_ref):
    pre = dt_ref[...] + dtb_ref[...]
    dt = jnp.maximum(pre, 0.0) + jnp.log1p(jnp.exp(-jnp.abs(pre)))
    da = dt * (-jnp.exp(alog_ref[...]))
    row = lax.broadcasted_iota(jnp.int32, (CHUNK, CHUNK), 0)
    col = lax.broadcasted_iota(jnp.int32, (CHUNK, CHUNK), 1)
    causal = col <= row
    tri = causal.astype(F32).astype(BF16)
    da_hi, da_mid, da_lo = _split3(da)
    parts = _dot(tri, jnp.concatenate([da_hi, da_mid, da_lo], axis=1).astype(BF16))
    cum = parts[:, :LANES] + parts[:, LANES:2 * LANES] + parts[:, 2 * LANES:]
    cum_t = cum.T
    cum_last = cum[CHUNK - 1:CHUNK, :]

    packed = jnp.concatenate([_pack3(dt), _pack3(dt * jnp.exp(cum_last - cum)), _pack3(jnp.exp(cum))], axis=0)
    expanded = _dot(packed, expand_ref[...])
    dt_e = expanded[0:CHUNK]
    dt_decay_end_e = expanded[CHUNK:2 * CHUNK]
    exp_cum_e = expanded[2 * CHUNK:]
    chunk_decay_e = exp_cum_e[CHUNK - 1:CHUNK, :]

    xs = xs_ref[...]
    xdt = (xs * dt_e).astype(BF16)
    xdt_end = (xs * dt_decay_end_e).astype(BF16)
    lane_head = lax.broadcasted_iota(jnp.int32, (1, SSM_GROUP_WIDTH), 1) // SSM_HEADDIM

    for g in range(SSM_GROUPS):
        gcols = slice(g * SSM_GROUP_WIDTH, (g + 1) * SSM_GROUP_WIDTH)
        b_g = bc_ref[:, g * SSM_STATE:(g + 1) * SSM_STATE].astype(BF16)
        c_g = bc_ref[:, SSM_BC_WIDTH + g * SSM_STATE:SSM_BC_WIDTH + (g + 1) * SSM_STATE].astype(BF16)
        scores = _dot_nt(c_g, b_g)
        xdt_g = xdt[:, gcols]
        y = _dot(c_g, state_ref[g].astype(BF16)) * exp_cum_e[:, gcols]
        for r in range(SSM_HEADS_PER_GROUP):
            h = g * SSM_HEADS_PER_GROUP + r
            seg = cum[:, h:h + 1] - cum_t[h:h + 1, :]
            decay = jnp.exp(jnp.where(causal, seg, -jnp.inf))
            x_r = xdt_g * (lane_head == r).astype(F32).astype(BF16)
            y = y + _dot((scores * decay).astype(BF16), x_r)
        state_ref[g] = state_ref[g] * chunk_decay_e[:, gcols] + _dot_tn(b_g, xdt_end[:, gcols])
        y_ref[:, gcols] = y

    y = y_ref[...] + dskip_ref[...] * xs
    y = y * gate_ref[...]
    o_ref[...] = _rmsnorm(y, nw_ref[...]).astype(o_ref.dtype)


def _mixer_kernel(gu_ref, gv_ref, vw_ref, ws_ref, bs_ref,
                  gate_ref, xs_ref, bc_ref, dt_ref, dtb_ref, alog_ref, dskip_ref, nw_ref,
                  o_ref,
                  state_ref, y_ref, expand_ref):
    @pl.when(pl.program_id(0) == 0)
    def _():
        _ssd_init(state_ref, expand_ref)

    _sgu_mix(gu_ref, gv_ref, vw_ref, ws_ref, bs_ref, o_ref.at[:, 0:GM_WIDTH])
    _ssd_scan(gate_ref, xs_ref, bc_ref, dt_ref, dtb_ref, alog_ref, dskip_ref, nw_ref, o_ref.at[:, GM_WIDTH:],
              state_ref, y_ref, expand_ref)


def _mixer(proj, dt_raw, v_norm_w, w_s, b_s_t, dt_bias, a_log, d_skip_e, norm_w):
    m = proj.shape[0]
    row = lambda a: a.reshape(1, -1)
    c0 = 2 * GM_WIDTH // SSM_WIDTH
    full = lambda shape: pl.BlockSpec(shape, lambda i: (0,) * len(shape))
    chunk = lambda width, col: pl.BlockSpec((CHUNK, width), lambda i: (i, col))
    return pl.pallas_call(
        _mixer_kernel,
        grid=(m // CHUNK,),
        in_specs=[
            chunk(GM_WIDTH, 0),
            chunk(GM_WIDTH, 1),
            full((1, GM_WIDTH)),
            full((GM_GROUPS, CHUNK, CHUNK)),
            full((CHUNK, LANES)),
            chunk(SSM_WIDTH, c0),
            chunk(SSM_WIDTH, c0 + 1),
            chunk(2 * SSM_BC_WIDTH, c0 + 2),
            chunk(LANES, 0),
            full((1, LANES)),
            full((1, LANES)),
            full((1, SSM_WIDTH)),
            full((1, SSM_WIDTH)),
        ],
        out_specs=chunk(GM_WIDTH + SSM_WIDTH, 0),
        out_shape=jax.ShapeDtypeStruct((m, GM_WIDTH + SSM_WIDTH), BF16),
        scratch_shapes=[pltpu.VMEM((SSM_GROUPS, SSM_STATE, SSM_GROUP_WIDTH), F32),
                        pltpu.VMEM((CHUNK, SSM_WIDTH), F32),
                        pltpu.VMEM((LANES, SSM_WIDTH), BF16)],
        compiler_params=_params(1),
        name="mixer",
    )(proj, proj, row(v_norm_w), w_s, b_s_t,
      proj, proj, proj, dt_raw, row(dt_bias), row(a_log), row(d_skip_e), row(norm_w))


WEIGHT_STAGE_ROWS = 256


def _for_each_row_block(w_hbm, cols, stage_ref, sems, use):
    n_blocks = w_hbm.shape[0] // WEIGHT_STAGE_ROWS

    def block_rows(r):
        return pl.ds(pl.multiple_of(r * WEIGHT_STAGE_ROWS, WEIGHT_STAGE_ROWS), WEIGHT_STAGE_ROWS)

    def copy(r, slot):
        return pltpu.make_async_copy(w_hbm.at[block_rows(r), cols], stage_ref.at[slot], sems.at[slot])

    copy(0, 0).start()

    def body(r, _):
        slot = r % 2

        @pl.when(r + 1 < n_blocks)
        def _():
            copy(r + 1, 1 - slot).start()

        copy(r, slot).wait()
        use(block_rows(r), stage_ref.at[slot])

    lax.fori_loop(0, n_blocks, body, None)


def _load_as_bf16(w_hbm, w_vmem, stage_ref, sems):
    def use(rows, block_ref):
        w_vmem[rows, :] = block_ref[...].astype(BF16)

    _for_each_row_block(w_hbm, slice(None), stage_ref, sems, use)


def _project_memory(memn_ref, w_hbm, cols, out_ref, acc_ref, stage_ref, sems):
    acc_ref[...] = jnp.zeros_like(acc_ref)

    def use(rows, block_ref):
        acc_ref[...] += _dot(memn_ref[:, rows], block_ref[...].astype(BF16))

    _for_each_row_block(w_hbm, cols, stage_ref, sems, use)
    out_ref[...] = acc_ref[...].astype(BF16)


def _xattn_kernel(h_ref, mixed_ref, nw_ref, mem_ref, memw_ref, wout_hbm, wq_hbm, wkv_hbm, wo_hbm,
                  o_ref,
                  wout_ref, wq_ref, wo_ref, k_ref, v_ref, stage_ref, sems, h2_ref, q_ref, att_ref):
    d = h_ref.shape[1]

    @pl.when(pl.program_id(0) == 0)
    def _():
        _load_as_bf16(wout_hbm, wout_ref, stage_ref, sems)
        _load_as_bf16(wq_hbm, wq_ref, stage_ref, sems)
        _load_as_bf16(wo_hbm, wo_ref, stage_ref, sems)
        q_ref[...] = _rmsnorm(mem_ref[...], memw_ref[...]).astype(BF16)
        _project_memory(q_ref, wkv_hbm, pl.ds(0, d), k_ref, h2_ref, stage_ref, sems)
        _project_memory(q_ref, wkv_hbm, pl.ds(d, d), v_ref, h2_ref, stage_ref, sems)

    h2_ref[...] = h_ref[...] + _dot(mixed_ref[...], wout_ref[...])
    q_ref[...] = _dot(_rmsnorm(h2_ref[...], nw_ref[...]).astype(BF16), wq_ref[...]).astype(BF16)
    scale = XA_HEADDIM ** -0.5
    for h in range(XA_HEADS):
        cols = slice(h * XA_HEADDIM, (h + 1) * XA_HEADDIM)
        logits = _dot_nt(q_ref[:, cols], k_ref[:, cols]) * scale
        p = jnp.exp(logits - jnp.max(logits, axis=-1, keepdims=True))
        p = p / jnp.sum(p, axis=-1, keepdims=True)
        att_ref[:, cols] = _dot(p.astype(BF16), v_ref[:, cols]).astype(BF16)
    o_ref[...] = h2_ref[...] + _dot(att_ref[...], wo_ref[...])


def _xattn(h, mixed, norm_w, mem, mem_norm_w, w_out, w_q, w_kv, w_o):
    m, d = h.shape
    tm = MEM_LEN
    hbm = pl.BlockSpec(memory_space=pl.ANY)
    return pl.pallas_call(
        _xattn_kernel,
        grid=(m // tm,),
        in_specs=[
            pl.BlockSpec((tm, d), lambda i: (i, 0)),
            pl.BlockSpec((tm, mixed.shape[1]), lambda i: (i, 0)),
            pl.BlockSpec((1, d), lambda i: (0, 0)),
            pl.BlockSpec((MEM_LEN, d), lambda i: (0, 0), pipeline_mode=pl.Buffered(1)),
            pl.BlockSpec((1, d), lambda i: (0, 0)),
            hbm, hbm, hbm, hbm,
        ],
        out_specs=pl.BlockSpec((tm, d), lambda i: (i, 0)),
        out_shape=jax.ShapeDtypeStruct((m, d), F32),
        scratch_shapes=[pltpu.VMEM(w_out.shape, BF16),
                        pltpu.VMEM(w_q.shape, BF16),
                        pltpu.VMEM(w_o.shape, BF16),
                        pltpu.VMEM((MEM_LEN, d), BF16),
                        pltpu.VMEM((MEM_LEN, d), BF16),
                        pltpu.VMEM((2, WEIGHT_STAGE_ROWS, d), F32),
                        pltpu.SemaphoreType.DMA((2,)),
                        pltpu.VMEM((tm, d), F32),
                        pltpu.VMEM((tm, d), BF16),
                        pltpu.VMEM((tm, d), BF16)],
        compiler_params=_params(1),
        name="xattn",
    )(h, mixed, norm_w.reshape(1, d), mem, mem_norm_w.reshape(1, d), w_out, w_q, w_kv, w_o)


def kernel(x, mem, ffn1_norm, ffn1_w_gu, ffn1_w_down, mix_norm, w_in, gm_v_norm, gm_w_s, gm_b_s, ssm_conv_w, ssm_conv_b, ssm_dt_bias, ssm_a_log, ssm_d, ssm_norm, w_out, xa_norm, mem_norm, xa_w_q, xa_w_kv, xa_w_o, ffn2_norm, ffn2_w_gu, ffn2_w_down, final_norm):
    pad_lanes =lambda a: jnp.pad(a, ((0, 0), (0, LANES - a.shape[1])))
    h = x[0]
    for i in range(ffn1_norm.shape[0]):
        h, xn = _ffn(h, ffn1_norm[i], ffn1_w_gu[i], ffn1_w_down[i], mix_norm[i], tail="next")

        proj, dt_raw = _in_proj(xn, jnp.swapaxes(w_in[i], 0, 1), ssm_conv_w[i], ssm_conv_b[i])
        mixed = _mixer(proj, dt_raw, gm_v_norm[i], gm_w_s[i], pad_lanes(gm_b_s[i].T),
                       pad_lanes(ssm_dt_bias[i][None])[0], pad_lanes(ssm_a_log[i][None])[0],
                       jnp.repeat(ssm_d[i], SSM_HEADDIM), ssm_norm[i])
        h = _xattn(h, mixed, xa_norm[i], mem[0], mem_norm[i], w_out[i], xa_w_q[i], xa_w_kv[i], xa_w_o[i])

        last = i == ffn1_norm.shape[0] - 1
        h = _ffn(h, ffn2_norm[i], ffn2_w_gu[i], ffn2_w_down[i], final_norm, tail="final" if last else "none")[0]
    return h[None]
```

```python
import functools

import jax
import jax.numpy as jnp
from jax import lax
from jax.experimental import pallas as pl
from jax.experimental.pallas import tpu as pltpu

D_MODEL = 2048
SEQ = 8192
MEM_LEN = 256
GM_WIDTH = 2048
GM_GROUPS = 4
GM_GROUP_WIDTH = GM_WIDTH // GM_GROUPS
CHUNK = 128
SSM_WIDTH = 2048
SSM_HEADDIM = 64
SSM_HEADS = SSM_WIDTH // SSM_HEADDIM
SSM_GROUPS = 8
SSM_HEADS_PER_GROUP = SSM_HEADS // SSM_GROUPS
SSM_GROUP_WIDTH = SSM_HEADS_PER_GROUP * SSM_HEADDIM
SSM_STATE = 128
SSM_CONV = 4
SSM_BC_WIDTH = SSM_GROUPS * SSM_STATE
PROJ_MAIN = 2 * GM_WIDTH + SSM_WIDTH + SSM_WIDTH + 2 * SSM_BC_WIDTH
XA_HEADS = 4
XA_HEADDIM = D_MODEL // XA_HEADS
D_FF = 5632
EPS = 1e-6

LANES = 128
HALO_ROWS = 8
ROW_BLOCK = 128
IN_PROJ_ROW_BLOCK = 256
VMEM_LIMIT = 60 * 1024 * 1024

BF16 = jnp.bfloat16
F32 = jnp.float32


def _params(n_axes):
    return pltpu.CompilerParams(dimension_semantics=("arbitrary",) * n_axes,
                                vmem_limit_bytes=VMEM_LIMIT)


def _rmsnorm(x, w):
    return x * lax.rsqrt(jnp.mean(x * x, axis=-1, keepdims=True) + EPS) * w


def _silu(x):
    half = 0.5 * x
    return half + half * jnp.tanh(half)


def _gelu_tanh(x):
    c = 0.7978845608028654
    return 0.5 * x * (1.0 + jnp.tanh(x * (c + (0.044715 * c) * (x * x))))


def _dot(a, b):
    return jnp.dot(a, b, preferred_element_type=F32)


def _dot_nt(a, b):
    return lax.dot_general(a, b, (((1,), (1,)), ((), ())), preferred_element_type=F32)


def _dot_tn(a, b):
    return lax.dot_general(a, b, (((0,), (0,)), ((), ())), preferred_element_type=F32)


def _ffn_kernel(x_ref, nw_ref, wg_ref, wu_ref, wd_ref, tw_ref, o_ref, *rest, tail):
    xn_ref = rest[-1]
    j = pl.program_id(1)
    n_row_blocks = x_ref.shape[0] // ROW_BLOCK

    def row_block(r):
        return pl.ds(pl.multiple_of(r * ROW_BLOCK, ROW_BLOCK), ROW_BLOCK)

    @pl.when(j == 0)
    def _():
        def body(r, _):
            rows = row_block(r)
            x = x_ref[rows, :]
            xn_ref[rows, :] = _rmsnorm(x, nw_ref[...]).astype(BF16)
            o_ref[rows, :] = x
        lax.fori_loop(0, n_row_blocks, body, None)

    xn = xn_ref[...]
    g = _dot(xn, wg_ref[...].astype(BF16))
    u = _dot(xn, wu_ref[...].astype(BF16))
    o_ref[...] += _dot((_silu(g) * u).astype(BF16), (0.5 * wd_ref[...]).astype(BF16))

    if tail != "none":
        @pl.when(j == pl.num_programs(1) - 1)
        def _():
            def body(r, _):
                rows = row_block(r)
                hn = _rmsnorm(o_ref[rows, :], tw_ref[...])
                if tail == "final":
                    o_ref[rows, :] = hn
                else:
                    rest[0][rows, :] = hn.astype(BF16)
            lax.fori_loop(0, n_row_blocks, body, None)


def _ffn(x, norm_w, w_gu, w_down, tail_w, *, tail, tm=1024, tf=256):
    m, d = x.shape
    f = w_down.shape[0]
    nf = f // tf
    row_tile = pl.BlockSpec((tm, d), lambda i, j: (i, 0))
    out_specs, out_shape = [row_tile], [jax.ShapeDtypeStruct((m, d), F32)]
    if tail == "next":
        out_specs.append(row_tile)
        out_shape.append(jax.ShapeDtypeStruct((m, d), BF16))
    return pl.pallas_call(
        functools.partial(_ffn_kernel, tail=tail),
        grid=(m // tm, nf),
        in_specs=[
            row_tile,
            pl.BlockSpec((1, d), lambda i, j: (0, 0)),
            pl.BlockSpec((d, tf), lambda i, j: (0, j)),
            pl.BlockSpec((d, tf), lambda i, j: (0, j + nf)),
            pl.BlockSpec((tf, d), lambda i, j: (j, 0)),
            pl.BlockSpec((1, d), lambda i, j: (0, 0)),
        ],
        out_specs=out_specs,
        out_shape=out_shape,
        scratch_shapes=[pltpu.VMEM((tm, d), BF16)],
        compiler_params=_params(2),
        name="ffn",
    )(x, norm_w.reshape(1, d), w_gu, w_gu, w_down, tail_w.reshape(1, d))


def _in_proj_kernel(xn_ref, wt_ref, wdt_ref, cw_ref, cb_ref, o_ref, dt_ref, raw_even_ref, raw_odd_ref, halo_ref,
                    *, gelu_tiles, silu_tiles):
    i = pl.program_id(0)
    j = pl.program_id(1)
    conv_first = gelu_tiles + silu_tiles
    raw_refs = (raw_even_ref, raw_odd_ref)

    @pl.when(j == 0)
    def _():
        dt_ref[...] = _dot_nt(xn_ref[...], wdt_ref[...].astype(BF16))

    tm = o_ref.shape[0]

    def project(consume):
        w = wt_ref[...].astype(BF16)
        product = lambda r0: _dot_nt(xn_ref[r0:r0 + IN_PROJ_ROW_BLOCK, :], w)
        ahead = product(0)
        for r0 in range(0, tm, IN_PROJ_ROW_BLOCK):
            current = ahead
            if r0 + IN_PROJ_ROW_BLOCK < tm:
                ahead = product(r0 + IN_PROJ_ROW_BLOCK)
            consume(r0, current)

    def store_with(activation):
        def consume(r0, acc):
            o_ref[r0:r0 + IN_PROJ_ROW_BLOCK, :] = activation(acc)
        return consume

    @pl.when(j < gelu_tiles)
    def _():
        project(store_with(_gelu_tanh))

    @pl.when((j >= gelu_tiles) & (j < gelu_tiles + silu_tiles))
    def _():
        project(store_with(_silu))

    @pl.when(j >= conv_first)
    def _():
        c = j - conv_first
        tail = slice(IN_PROJ_ROW_BLOCK, IN_PROJ_ROW_BLOCK + HALO_ROWS)
        n_blocks = tm // IN_PROJ_ROW_BLOCK

        @pl.when(i == 0)
        def _():
            raw_refs[0][0:HALO_ROWS, :] = jnp.zeros((HALO_ROWS, o_ref.shape[1]), F32)

        @pl.when(i > 0)
        def _():
            raw_refs[0][0:HALO_ROWS, :] = halo_ref[c]

        def conv_silu(r0, acc):
            block = r0 // IN_PROJ_ROW_BLOCK
            raw, previous = raw_refs[block % 2], raw_refs[(block - 1) % 2]
            if block > 0:
                raw[0:HALO_ROWS, :] = previous[tail, :]
            raw[HALO_ROWS:, :] = acc
            out = cb_ref[...]
            for tap in range(SSM_CONV):
                off = HALO_ROWS - (SSM_CONV - 1) + tap
                out = out + raw[off:off + IN_PROJ_ROW_BLOCK, :] * cw_ref[tap:tap + 1, :]
            o_ref[r0:r0 + IN_PROJ_ROW_BLOCK, :] = _silu(out)

        project(conv_silu)
        halo_ref[c] = raw_refs[(n_blocks - 1) % 2][tail, :]


def _in_proj(xn, w_in_t, conv_w, conv_b, *, tm=2048, tn=512):
    m, k = xn.shape
    gelu_tiles, silu_tiles = 2 * GM_WIDTH // tn, SSM_WIDTH // tn
    conv_tiles = (SSM_WIDTH + 2 * SSM_BC_WIDTH) // tn
    kernel_fn = functools.partial(_in_proj_kernel, gelu_tiles=gelu_tiles, silu_tiles=silu_tiles)
    conv_tile = lambda i, j: (0, jnp.maximum(j - (gelu_tiles + silu_tiles), 0))
    w_dt_t = jnp.pad(w_in_t[PROJ_MAIN:], ((0, LANES - (w_in_t.shape[0] - PROJ_MAIN)), (0, 0)))
    return pl.pallas_call(
        kernel_fn,
        grid=(m // tm, PROJ_MAIN // tn),
        in_specs=[
            pl.BlockSpec((tm, k), lambda i, j: (i, 0)),
            pl.BlockSpec((tn, k), lambda i, j: (j, 0)),
            pl.BlockSpec((LANES, k), lambda i, j: (0, 0)),
            pl.BlockSpec((SSM_CONV, tn), conv_tile),
            pl.BlockSpec((1, tn), conv_tile),
        ],
        out_specs=[pl.BlockSpec((tm, tn), lambda i, j: (i, j)),
                   pl.BlockSpec((tm, LANES), lambda i, j: (i, 0))],
        out_shape=[jax.ShapeDtypeStruct((m, PROJ_MAIN), F32), jax.ShapeDtypeStruct((m, LANES), F32)],
        scratch_shapes=[pltpu.VMEM((HALO_ROWS + IN_PROJ_ROW_BLOCK, tn), F32),
                        pltpu.VMEM((HALO_ROWS + IN_PROJ_ROW_BLOCK, tn), F32),
                        pltpu.VMEM((conv_tiles, HALO_ROWS, tn), F32)],
        compiler_params=_params(2),
        name="in_proj",
    )(xn, w_in_t, w_dt_t, conv_w, conv_b.reshape(1, -1))


def _sgu_mix(gu_ref, gv_ref, vw_ref, ws_ref, bs_ref, o_ref):
    vn = _rmsnorm(gv_ref[...], vw_ref[...]).astype(BF16)
    t_idx = lax.broadcasted_iota(jnp.int32, (CHUNK, CHUNK), 0)
    s_idx = lax.broadcasted_iota(jnp.int32, (CHUNK, CHUNK), 1)
    causal = s_idx <= t_idx
    bias = bs_ref[...]
    for g in range(GM_GROUPS):
        cols = slice(g * GM_GROUP_WIDTH, (g + 1) * GM_GROUP_WIDTH)
        w = jnp.where(causal, ws_ref[g], 0.0).astype(BF16)
        mixed = _dot(w, vn[:, cols]) + bias[:, g:g + 1]
        o_ref[:, cols] = (gu_ref[:, cols] * mixed).astype(o_ref.dtype)


def _split3(v):
    hi = v.astype(BF16).astype(F32)
    rest = v - hi
    mid = rest.astype(BF16).astype(F32)
    return hi, mid, rest - mid


def _pack3(v):
    hi, mid, lo = _split3(v)
    lane = lax.broadcasted_iota(jnp.int32, v.shape, 1)
    packed = jnp.where(lane < SSM_HEADS, hi,
                       jnp.where(lane < 2 * SSM_HEADS, pltpu.roll(mid, SSM_HEADS, 1),
                                 jnp.where(lane < 3 * SSM_HEADS, pltpu.roll(lo, 2 * SSM_HEADS, 1), 0.0)))
    return packed.astype(BF16)


def _ssd_init(state_ref, expand_ref):
    state_ref[...] = jnp.zeros_like(state_ref)
    k_idx = lax.broadcasted_iota(jnp.int32, (LANES, SSM_WIDTH), 0)
    j_idx = lax.broadcasted_iota(jnp.int32, (LANES, SSM_WIDTH), 1)
    hit = (k_idx < 3 * SSM_HEADS) & (j_idx // SSM_HEADDIM == k_idx % SSM_HEADS)
    expand_ref[...] = hit.astype(F32).astype(BF16)


def _ssd_scan(gate_ref, xs_ref, bc_ref, dt_ref, dtb_ref, alog_ref, dskip_ref, nw_ref, o_ref,
              state_ref, y_ref, expand_ref):
    pre = dt_ref[...] + dtb_ref[...]
    dt = jnp.maximum(pre, 0.0) + jnp.log1p(jnp.exp(-jnp.abs(pre)))
    da = dt * (-jnp.exp(alog_ref[...]))
    row = lax.broadcasted_iota(jnp.int32, (CHUNK, CHUNK), 0)
    col = lax.broadcasted_iota(jnp.int32, (CHUNK, CHUNK), 1)
    causal = col <= row
    tri = causal.astype(F32).astype(BF16)
    da_hi, da_mid, da_lo = _split3(da)
    parts = _dot(tri, jnp.concatenate([da_hi, da_mid, da_lo], axis=1).astype(BF16))
    cum = parts[:, :LANES] + parts[:, LANES:2 * LANES] + parts[:, 2 * LANES:]
    cum_t = cum.T
    cum_last = cum[CHUNK - 1:CHUNK, :]

    packed = jnp.concatenate([_pack3(dt), _pack3(dt * jnp.exp(cum_last - cum)), _pack3(jnp.exp(cum))], axis=0)
    expanded = _dot(packed, expand_ref[...])
    dt_e = expanded[0:CHUNK]
    dt_decay_end_e = expanded[CHUNK:2 * CHUNK]
    exp_cum_e = expanded[2 * CHUNK:]
    chunk_decay_e = exp_cum_e[CHUNK - 1:CHUNK, :]

    xs = xs_ref[...]
    xdt = (xs * dt_e).astype(BF16)
    xdt_end = (xs * dt_decay_end_e).astype(BF16)
    lane_head = lax.broadcasted_iota(jnp.int32, (1, SSM_GROUP_WIDTH), 1) // SSM_HEADDIM

    for g in range(SSM_GROUPS):
        gcols = slice(g * SSM_GROUP_WIDTH, (g + 1) * SSM_GROUP_WIDTH)
        b_g = bc_ref[:, g * SSM_STATE:(g + 1) * SSM_STATE].astype(BF16)
        c_g = bc_ref[:, SSM_BC_WIDTH + g * SSM_STATE:SSM_BC_WIDTH + (g + 1) * SSM_STATE].astype(BF16)
        scores = _dot_nt(c_g, b_g)
        xdt_g = xdt[:, gcols]
        y = _dot(c_g, state_ref[g].astype(BF16)) * exp_cum_e[:, gcols]
        for r in range(SSM_HEADS_PER_GROUP):
            h = g * SSM_HEADS_PER_GROUP + r
            seg = cum[:, h:h + 1] - cum_t[h:h + 1, :]
            decay = jnp.exp(jnp.where(causal, seg, -jnp.inf))
            x_r = xdt_g * (lane_head == r).astype(F32).astype(BF16)
            y = y + _dot((scores * decay).astype(BF16), x_r)
        state_ref[g] = state_ref[g] * chunk_decay_e[:, gcols] + _dot_tn(b_g, xdt_end[:, gcols])
        y_ref[:, gcols] = y

    y = y_ref[...] + dskip_ref[...] * xs
    y = y * gate_ref[...]
    o_ref[...] = _rmsnorm(y, nw_ref[...]).astype(o_ref.dtype)


def _mixer_kernel(gu_ref, gv_ref, vw_ref, ws_ref, bs_ref,
                  gate_ref, xs_ref, bc_ref, dt_ref, dtb_ref, alog_ref, dskip_ref, nw_ref,
                  o_ref,
                  state_ref, y_ref, expand_ref):
    @pl.when(pl.program_id(0) == 0)
    def _():
        _ssd_init(state_ref, expand_ref)

    _sgu_mix(gu_ref, gv_ref, vw_ref, ws_ref, bs_ref, o_ref.at[:, 0:GM_WIDTH])
    _ssd_scan(gate_ref, xs_ref, bc_ref, dt_ref, dtb_ref, alog_ref, dskip_ref, nw_ref, o_ref.at[:, GM_WIDTH:],
              state_ref, y_ref, expand_ref)


def _mixer(proj, dt_raw, v_norm_w, w_s, b_s_t, dt_bias, a_log, d_skip_e, norm_w):
    m = proj.shape[0]
    row = lambda a: a.reshape(1, -1)
    c0 = 2 * GM_WIDTH // SSM_WIDTH
    full = lambda shape: pl.BlockSpec(shape, lambda i: (0,) * len(shape))
    chunk = lambda width, col: pl.BlockSpec((CHUNK, width), lambda i: (i, col))
    return pl.pallas_call(
        _mixer_kernel,
        grid=(m // CHUNK,),
        in_specs=[
            chunk(GM_WIDTH, 0),
            chunk(GM_WIDTH, 1),
            full((1, GM_WIDTH)),
            full((GM_GROUPS, CHUNK, CHUNK)),
            full((CHUNK, LANES)),
            chunk(SSM_WIDTH, c0),
            chunk(SSM_WIDTH, c0 + 1),
            chunk(2 * SSM_BC_WIDTH, c0 + 2),
            chunk(LANES, 0),
            full((1, LANES)),
            full((1, LANES)),
            full((1, SSM_WIDTH)),
            full((1, SSM_WIDTH)),
        ],
        out_specs=chunk(GM_WIDTH + SSM_WIDTH, 0),
        out_shape=jax.ShapeDtypeStruct((m, GM_WIDTH + SSM_WIDTH), BF16),
        scratch_shapes=[pltpu.VMEM((SSM_GROUPS, SSM_STATE, SSM_GROUP_WIDTH), F32),
                        pltpu.VMEM((CHUNK, SSM_WIDTH), F32),
                        pltpu.VMEM((LANES, SSM_WIDTH), BF16)],
        compiler_params=_params(1),
        name="mixer",
    )(proj, proj, row(v_norm_w), w_s, b_s_t,
      proj, proj, proj, dt_raw, row(dt_bias), row(a_log), row(d_skip_e), row(norm_w))


WEIGHT_STAGE_ROWS = 256


def _for_each_row_block(w_hbm, cols, stage_ref, sems, use):
    n_blocks = w_hbm.shape[0] // WEIGHT_STAGE_ROWS

    def block_rows(r):
        return pl.ds(pl.multiple_of(r * WEIGHT_STAGE_ROWS, WEIGHT_STAGE_ROWS), WEIGHT_STAGE_ROWS)

    def copy(r, slot):
        return pltpu.make_async_copy(w_hbm.at[block_rows(r), cols], stage_ref.at[slot], sems.at[slot])

    copy(0, 0).start()

    def body(r, _):
        slot = r % 2

        @pl.when(r + 1 < n_blocks)
        def _():
            copy(r + 1, 1 - slot).start()

        copy(r, slot).wait()
        use(block_rows(r), stage_ref.at[slot])

    lax.fori_loop(0, n_blocks, body, None)


def _load_as_bf16(w_hbm, w_vmem, stage_ref, sems):
    def use(rows, block_ref):
        w_vmem[rows, :] = block_ref[...].astype(BF16)

    _for_each_row_block(w_hbm, slice(None), stage_ref, sems, use)


def _project_memory(memn_ref, w_hbm, cols, out_ref, acc_ref, stage_ref, sems):
    acc_ref[...] = jnp.zeros_like(acc_ref)

    def use(rows, block_ref):
        acc_ref[...] += _dot(memn_ref[:, rows], block_ref[...].astype(BF16))

    _for_each_row_block(w_hbm, cols, stage_ref, sems, use)
    out_ref[...] = acc_ref[...].astype(BF16)


def _xattn_kernel(h_ref, mixed_ref, nw_ref, mem_ref, memw_ref, wout_hbm, wq_hbm, wkv_hbm, wo_hbm,
                  o_ref,
                  wout_ref, wq_ref, wo_ref, k_ref, v_ref, stage_ref, sems, h2_ref, q_ref, att_ref):
    d = h_ref.shape[1]

    @pl.when(pl.program_id(0) == 0)
    def _():
        _load_as_bf16(wout_hbm, wout_ref, stage_ref, sems)
        _load_as_bf16(wq_hbm, wq_ref, stage_ref, sems)
        _load_as_bf16(wo_hbm, wo_ref, stage_ref, sems)
        q_ref[...] = _rmsnorm(mem_ref[...], memw_ref[...]).astype(BF16)
        _project_memory(q_ref, wkv_hbm, pl.ds(0, d), k_ref, h2_ref, stage_ref, sems)
        _project_memory(q_ref, wkv_hbm, pl.ds(d, d), v_ref, h2_ref, stage_ref, sems)

    h2_ref[...] = h_ref[...] + _dot(mixed_ref[...], wout_ref[...])
    q_ref[...] = _dot(_rmsnorm(h2_ref[...], nw_ref[...]).astype(BF16), wq_ref[...]).astype(BF16)
    scale = XA_HEADDIM ** -0.5
    for h in range(XA_HEADS):
        cols = slice(h * XA_HEADDIM, (h + 1) * XA_HEADDIM)
        logits = _dot_nt(q_ref[:, cols], k_ref[:, cols]) * scale
        p = jnp.exp(logits - jnp.max(logits, axis=-1, keepdims=True))
        p = p / jnp.sum(p, axis=-1, keepdims=True)
        att_ref[:, cols] = _dot(p.astype(BF16), v_ref[:, cols]).astype(BF16)
    o_ref[...] = h2_ref[...] + _dot(att_ref[...], wo_ref[...])


def _xattn(h, mixed, norm_w, mem, mem_norm_w, w_out, w_q, w_kv, w_o):
    m, d = h.shape
    tm = MEM_LEN
    hbm = pl.BlockSpec(memory_space=pl.ANY)
    return pl.pallas_call(
        _xattn_kernel,
        grid=(m // tm,),
        in_specs=[
            pl.BlockSpec((tm, d), lambda i: (i, 0)),
            pl.BlockSpec((tm, mixed.shape[1]), lambda i: (i, 0)),
            pl.BlockSpec((1, d), lambda i: (0, 0)),
            pl.BlockSpec((MEM_LEN, d), lambda i: (0, 0), pipeline_mode=pl.Buffered(1)),
            pl.BlockSpec((1, d), lambda i: (0, 0)),
            hbm, hbm, hbm, hbm,
        ],
        out_specs=pl.BlockSpec((tm, d), lambda i: (i, 0)),
        out_shape=jax.ShapeDtypeStruct((m, d), F32),
        scratch_shapes=[pltpu.VMEM(w_out.shape, BF16),
                        pltpu.VMEM(w_q.shape, BF16),
                        pltpu.VMEM(w_o.shape, BF16),
                        pltpu.VMEM((MEM_LEN, d), BF16),
                        pltpu.VMEM((MEM_LEN, d), BF16),
                        pltpu.VMEM((2, WEIGHT_STAGE_ROWS, d), F32),
                        pltpu.SemaphoreType.DMA((2,)),
                        pltpu.VMEM((tm, d), F32),
                        pltpu.VMEM((tm, d), BF16),
                        pltpu.VMEM((tm, d), BF16)],
        compiler_params=_params(1),
        name="xattn",
    )(h, mixed, norm_w.reshape(1, d), mem, mem_norm_w.reshape(1, d), w_out, w_q, w_kv, w_o)


def kernel(x, mem, ffn1_norm, ffn1_w_gu, ffn1_w_down, mix_norm, w_in, gm_v_norm, gm_w_s, gm_b_s, ssm_conv_w, ssm_conv_b, ssm_dt_bias, ssm_a_log, ssm_d, ssm_norm, w_out, xa_norm, mem_norm, xa_w_q, xa_w_kv, xa_w_o, ffn2_norm, ffn2_w_gu, ffn2_w_down, final_norm):
    pad_lanes =lambda a: jnp.pad(a, ((0, 0), (0, LANES - a.shape[1])))
    h = x[0]
    for i in range(ffn1_norm.shape[0]):
        h, xn = _ffn(h, ffn1_norm[i], ffn1_w_gu[i], ffn1_w_down[i], mix_norm[i], tail="next")

        proj, dt_raw = _in_proj(xn, jnp.swapaxes(w_in[i], 0, 1), ssm_conv_w[i], ssm_conv_b[i])
        mixed = _mixer(proj, dt_raw, gm_v_norm[i], gm_w_s[i], pad_lanes(gm_b_s[i].T),
                       pad_lanes(ssm_dt_bias[i][None])[0], pad_lanes(ssm_a_log[i][None])[0],
                       jnp.repeat(ssm_d[i], SSM_HEADDIM), ssm_norm[i])
        h = _xattn(h, mixed, xa_norm[i], mem[0], mem_norm[i], w_out[i], xa_w_q[i], xa_w_kv[i], xa_w_o[i])

        last = i == ffn1_norm.shape[0] - 1
        h = _ffn(h, ffn2_norm[i], ffn2_w_gu[i], ffn2_w_down[i], final_norm, tail="final" if last else "none")[0]
    return h[None]
```

```python
import functools

import jax
import jax.numpy as jnp
from jax import lax
from jax.experimental import pallas as pl
from jax.experimental.pallas import tpu as pltpu

D_MODEL = 2048
SEQ = 8192
MEM_LEN = 256
GM_WIDTH = 2048
GM_GROUPS = 4
GM_GROUP_WIDTH = GM_WIDTH // GM_GROUPS
CHUNK = 128
SSM_WIDTH = 2048
SSM_HEADDIM = 64
SSM_HEADS = SSM_WIDTH // SSM_HEADDIM
SSM_GROUPS = 8
SSM_HEADS_PER_GROUP = SSM_HEADS // SSM_GROUPS
SSM_GROUP_WIDTH = SSM_HEADS_PER_GROUP * SSM_HEADDIM
SSM_STATE = 128
SSM_CONV = 4
SSM_BC_WIDTH = SSM_GROUPS * SSM_STATE
PROJ_MAIN = 2 * GM_WIDTH + SSM_WIDTH + SSM_WIDTH + 2 * SSM_BC_WIDTH
XA_HEADS = 4
XA_HEADDIM = D_MODEL // XA_HEADS
D_FF = 5632
EPS = 1e-6

LANES = 128
HALO_ROWS = 8
ROW_BLOCK = 128
IN_PROJ_ROW_BLOCK = 256
SIDE_CAST_STEPS = 64
VMEM_LIMIT = 60 * 1024 * 1024

BF16 = jnp.bfloat16
F32 = jnp.float32


def _params(n_axes):
    return pltpu.CompilerParams(dimension_semantics=("arbitrary",) * n_axes,
                                vmem_limit_bytes=VMEM_LIMIT)


def _rmsnorm(x, w):
    return x * lax.rsqrt(jnp.mean(x * x, axis=-1, keepdims=True) + EPS) * w


def _silu(x):
    half = 0.5 * x
    return half + half * jnp.tanh(half)


def _gelu_tanh(x):
    c = 0.7978845608028654
    return 0.5 * x * (1.0 + jnp.tanh(x * (c + (0.044715 * c) * (x * x))))


def _dot(a, b):
    return jnp.dot(a, b, preferred_element_type=F32)


def _dot_nt(a, b):
    return lax.dot_general(a, b, (((1,), (1,)), ((), ())), preferred_element_type=F32)


def _dot_tn(a, b):
    return lax.dot_general(a, b, (((0,), (0,)), ((), ())), preferred_element_type=F32)


def _ffn_kernel(x_ref, nw_ref, wg_ref, wu_ref, wd_ref, tw_ref, o_ref, *rest, tail):
    xn_ref = rest[-1]
    j = pl.program_id(1)
    n_row_blocks = x_ref.shape[0] // ROW_BLOCK

    def row_block(r):
        return pl.ds(pl.multiple_of(r * ROW_BLOCK, ROW_BLOCK), ROW_BLOCK)

    @pl.when(j == 0)
    def _():
        def body(r, _):
            rows = row_block(r)
            x = x_ref[rows, :]
            xn_ref[rows, :] = _rmsnorm(x, nw_ref[...]).astype(BF16)
            o_ref[rows, :] = x
        lax.fori_loop(0, n_row_blocks, body, None)

    xn = xn_ref[...]
    g = _dot(xn, wg_ref[...].astype(BF16))
    u = _dot(xn, wu_ref[...].astype(BF16))
    o_ref[...] += _dot((_silu(g) * u).astype(BF16), (0.5 * wd_ref[...]).astype(BF16))

    if tail != "none":
        @pl.when(j == pl.num_programs(1) - 1)
        def _():
            def body(r, _):
                rows = row_block(r)
                hn = _rmsnorm(o_ref[rows, :], tw_ref[...])
                if tail == "final":
                    o_ref[rows, :] = hn
                else:
                    rest[0][rows, :] = hn.astype(BF16)
            lax.fori_loop(0, n_row_blocks, body, None)


def _ffn(x, norm_w, w_gu, w_down, tail_w, *, tail, tm=1024, tf=256):
    m, d = x.shape
    f = w_down.shape[0]
    nf = f // tf
    row_tile = pl.BlockSpec((tm, d), lambda i, j: (i, 0))
    out_specs, out_shape = [row_tile], [jax.ShapeDtypeStruct((m, d), F32)]
    if tail == "next":
        out_specs.append(row_tile)
        out_shape.append(jax.ShapeDtypeStruct((m, d), BF16))
    return pl.pallas_call(
        functools.partial(_ffn_kernel, tail=tail),
        grid=(m // tm, nf),
        in_specs=[
            row_tile,
            pl.BlockSpec((1, d), lambda i, j: (0, 0)),
            pl.BlockSpec((d, tf), lambda i, j: (0, j)),
            pl.BlockSpec((d, tf), lambda i, j: (0, j + nf)),
            pl.BlockSpec((tf, d), lambda i, j: (j, 0)),
            pl.BlockSpec((1, d), lambda i, j: (0, 0)),
        ],
        out_specs=out_specs,
        out_shape=out_shape,
        scratch_shapes=[pltpu.VMEM((tm, d), BF16)],
        compiler_params=_params(2),
        name="ffn",
    )(x, norm_w.reshape(1, d), w_gu, w_gu, w_down, tail_w.reshape(1, d))


def _in_proj_kernel(xn_ref, wt_ref, wdt_ref, cw_ref, cb_ref, *rest, gelu_tiles, silu_tiles, n_side):
    side_in, (o_ref, dt_ref) = rest[:n_side], rest[n_side:n_side + 2]
    side_out = rest[n_side + 2:2 * n_side + 2]
    raw_even_ref, raw_odd_ref, halo_ref = rest[2 * n_side + 2:]
    for src_ref, dst_ref in zip(side_in, side_out):
        dst_ref[...] = src_ref[...].astype(BF16)

    i = pl.program_id(0)
    j = pl.program_id(1)
    conv_first = gelu_tiles + silu_tiles
    raw_refs = (raw_even_ref, raw_odd_ref)

    @pl.when(j == 0)
    def _():
        dt_ref[...] = _dot_nt(xn_ref[...], wdt_ref[...].astype(BF16))

    tm = o_ref.shape[0]

    def project(consume):
        w = wt_ref[...].astype(BF16)
        product = lambda r0: _dot_nt(xn_ref[r0:r0 + IN_PROJ_ROW_BLOCK, :], w)
        ahead = product(0)
        for r0 in range(0, tm, IN_PROJ_ROW_BLOCK):
            current = ahead
            if r0 + IN_PROJ_ROW_BLOCK < tm:
                ahead = product(r0 + IN_PROJ_ROW_BLOCK)
            consume(r0, current)

    def store_with(activation):
        def consume(r0, acc):
            o_ref[r0:r0 + IN_PROJ_ROW_BLOCK, :] = activation(acc)
        return consume

    @pl.when(j < gelu_tiles)
    def _():
        project(store_with(_gelu_tanh))

    @pl.when((j >= gelu_tiles) & (j < gelu_tiles + silu_tiles))
    def _():
        project(store_with(_silu))

    @pl.when(j >= conv_first)
    def _():
        c = j - conv_first
        tail = slice(IN_PROJ_ROW_BLOCK, IN_PROJ_ROW_BLOCK + HALO_ROWS)
        n_blocks = tm // IN_PROJ_ROW_BLOCK

        @pl.when(i == 0)
        def _():
            raw_refs[0][0:HALO_ROWS, :] = jnp.zeros((HALO_ROWS, o_ref.shape[1]), F32)

        @pl.when(i > 0)
        def _():
            raw_refs[0][0:HALO_ROWS, :] = halo_ref[c]

        def conv_silu(r0, acc):
            block = r0 // IN_PROJ_ROW_BLOCK
            raw, previous = raw_refs[block % 2], raw_refs[(block - 1) % 2]
            if block > 0:
                raw[0:HALO_ROWS, :] = previous[tail, :]
            raw[HALO_ROWS:, :] = acc
            out = cb_ref[...]
            for tap in range(SSM_CONV):
                off = HALO_ROWS - (SSM_CONV - 1) + tap
                out = out + raw[off:off + IN_PROJ_ROW_BLOCK, :] * cw_ref[tap:tap + 1, :]
            o_ref[r0:r0 + IN_PROJ_ROW_BLOCK, :] = _silu(out)

        project(conv_silu)
        halo_ref[c] = raw_refs[(n_blocks - 1) % 2][tail, :]


def _in_proj(xn, w_in_t, conv_w, conv_b, side_weights, *, tm=2048, tn=512):
    m, k = xn.shape
    n_col = PROJ_MAIN // tn
    gelu_tiles, silu_tiles = 2 * GM_WIDTH // tn, SSM_WIDTH // tn
    conv_tiles = (SSM_WIDTH + 2 * SSM_BC_WIDTH) // tn
    kernel_fn = functools.partial(_in_proj_kernel, gelu_tiles=gelu_tiles, silu_tiles=silu_tiles,
                                  n_side=len(side_weights))
    assert SIDE_CAST_STEPS <= (m // tm) * n_col
    side_specs = [pl.BlockSpec((w.shape[0] // SIDE_CAST_STEPS, w.shape[1]),
                               lambda i, j: (jnp.minimum(i * n_col + j, SIDE_CAST_STEPS - 1), 0))
                  for w in side_weights]
    conv_tile = lambda i, j: (0, jnp.maximum(j - (gelu_tiles + silu_tiles), 0))
    w_dt_t = jnp.pad(w_in_t[PROJ_MAIN:], ((0, LANES - (w_in_t.shape[0] - PROJ_MAIN)), (0, 0)))
    return pl.pallas_call(
        kernel_fn,
        grid=(m // tm, n_col),
        in_specs=[
            pl.BlockSpec((tm, k), lambda i, j: (i, 0)),
            pl.BlockSpec((tn, k), lambda i, j: (j, 0)),
            pl.BlockSpec((LANES, k), lambda i, j: (0, 0)),
            pl.BlockSpec((SSM_CONV, tn), conv_tile),
            pl.BlockSpec((1, tn), conv_tile),
            *side_specs,
        ],
        out_specs=[pl.BlockSpec((tm, tn), lambda i, j: (i, j)),
                   pl.BlockSpec((tm, LANES), lambda i, j: (i, 0)),
                   *side_specs],
        out_shape=[jax.ShapeDtypeStruct((m, PROJ_MAIN), F32), jax.ShapeDtypeStruct((m, LANES), F32),
                   *[jax.ShapeDtypeStruct(w.shape, BF16) for w in side_weights]],
        scratch_shapes=[pltpu.VMEM((HALO_ROWS + IN_PROJ_ROW_BLOCK, tn), F32),
                        pltpu.VMEM((HALO_ROWS + IN_PROJ_ROW_BLOCK, tn), F32),
                        pltpu.VMEM((conv_tiles, HALO_ROWS, tn), F32)],
        compiler_params=_params(2),
        name="in_proj",
    )(xn, w_in_t, w_dt_t, conv_w, conv_b.reshape(1, -1), *side_weights)


def _sgu_mix(gu_ref, gv_ref, vw_ref, ws_ref, bs_ref, o_ref):
    vn = _rmsnorm(gv_ref[...], vw_ref[...]).astype(BF16)
    t_idx = lax.broadcasted_iota(jnp.int32, (CHUNK, CHUNK), 0)
    s_idx = lax.broadcasted_iota(jnp.int32, (CHUNK, CHUNK), 1)
    causal = s_idx <= t_idx
    bias = bs_ref[...]
    for g in range(GM_GROUPS):
        cols = slice(g * GM_GROUP_WIDTH, (g + 1) * GM_GROUP_WIDTH)
        w = jnp.where(causal, ws_ref[g], 0.0).astype(BF16)
        mixed = _dot(w, vn[:, cols]) + bias[:, g:g + 1]
        o_ref[:, cols] = (gu_ref[:, cols] * mixed).astype(o_ref.dtype)


def _split3(v):
    hi = v.astype(BF16).astype(F32)
    rest = v - hi
    mid = rest.astype(BF16).astype(F32)
    return hi, mid, rest - mid


def _pack3(v):
    hi, mid, lo = _split3(v)
    lane = lax.broadcasted_iota(jnp.int32, v.shape, 1)
    packed = jnp.where(lane < SSM_HEADS, hi,
                       jnp.where(lane < 2 * SSM_HEADS, pltpu.roll(mid, SSM_HEADS, 1),
                                 jnp.where(lane < 3 * SSM_HEADS, pltpu.roll(lo, 2 * SSM_HEADS, 1), 0.0)))
    return packed.astype(BF16)


def _ssd_init(state_ref, expand_ref):
    state_ref[...] = jnp.zeros_like(state_ref)
    k_idx = lax.broadcasted_iota(jnp.int32, (LANES, SSM_WIDTH), 0)
    j_idx = lax.broadcasted_iota(jnp.int32, (LANES, SSM_WIDTH), 1)
    hit = (k_idx < 3 * SSM_HEADS) & (j_idx // SSM_HEADDIM == k_idx % SSM_HEADS)
    expand_ref[...] = hit.astype(F32).astype(BF16)


def _ssd_scan(gate_ref, xs_ref, bc_ref, dt_ref, dtb_ref, alog_ref, dskip_ref, nw_ref, o_ref,
              state_ref, y_ref, expand_ref):
    pre = dt_ref[...] + dtb_ref[...]
    dt = jnp.maximum(pre, 0.0) + jnp.log1p(jnp.exp(-jnp.abs(pre)))
    da = dt * (-jnp.exp(alog_ref[...]))
    row = lax.broadcasted_iota(jnp.int32, (CHUNK, CHUNK), 0)
    col = lax.broadcasted_iota(jnp.int32, (CHUNK, CHUNK), 1)
    causal = col <= row
    tri = causal.astype(F32).astype(BF16)
    da_hi, da_mid, da_lo = _split3(da)
    parts = _dot(tri, jnp.concatenate([da_hi, da_mid, da_lo], axis=1).astype(BF16))
    cum = parts[:, :LANES] + parts[:, LANES:2 * LANES] + parts[:, 2 * LANES:]
    cum_t = cum.T
    cum_last = cum[CHUNK - 1:CHUNK, :]

    packed = jnp.concatenate([_pack3(dt), _pack3(dt * jnp.exp(cum_last - cum)), _pack3(jnp.exp(cum))], axis=0)
    expanded = _dot(packed, expand_ref[...])
    dt_e = expanded[0:CHUNK]
    dt_decay_end_e = expanded[CHUNK:2 * CHUNK]
    exp_cum_e = expanded[2 * CHUNK:]
    chunk_decay_e = exp_cum_e[CHUNK - 1:CHUNK, :]

    xs = xs_ref[...]
    xdt = (xs * dt_e).astype(BF16)
    xdt_end = (xs * dt_decay_end_e).astype(BF16)
    lane_head = lax.broadcasted_iota(jnp.int32, (1, SSM_GROUP_WIDTH), 1) // SSM_HEADDIM

    for g in range(SSM_GROUPS):
        gcols = slice(g * SSM_GROUP_WIDTH, (g + 1) * SSM_GROUP_WIDTH)
        b_g = bc_ref[:, g * SSM_STATE:(g + 1) * SSM_STATE].astype(BF16)
        c_g = bc_ref[:, SSM_BC_WIDTH + g * SSM_STATE:SSM_BC_WIDTH + (g + 1) * SSM_STATE].astype(BF16)
        scores = _dot_nt(c_g, b_g)
        xdt_g = xdt[:, gcols]
        y = _dot(c_g, state_ref[g].astype(BF16)) * exp_cum_e[:, gcols]
        for r in range(SSM_HEADS_PER_GROUP):
            h = g * SSM_HEADS_PER_GROUP + r
            seg = cum[:, h:h + 1] - cum_t[h:h + 1, :]
            decay = jnp.exp(jnp.where(causal, seg, -jnp.inf))
            x_r = xdt_g * (lane_head == r).astype(F32).astype(BF16)
            y = y + _dot((scores * decay).astype(BF16), x_r)
        state_ref[g] = state_ref[g] * chunk_decay_e[:, gcols] + _dot_tn(b_g, xdt_end[:, gcols])
        y_ref[:, gcols] = y

    y = y_ref[...] + dskip_ref[...] * xs
    y = y * gate_ref[...]
    o_ref[...] = _rmsnorm(y, nw_ref[...]).astype(o_ref.dtype)


def _mixer_kernel(gu_ref, gv_ref, vw_ref, ws_ref, bs_ref,
                  gate_ref, xs_ref, bc_ref, dt_ref, dtb_ref, alog_ref, dskip_ref, nw_ref,
                  o_ref,
                  state_ref, y_ref, expand_ref):
    @pl.when(pl.program_id(0) == 0)
    def _():
        _ssd_init(state_ref, expand_ref)

    _sgu_mix(gu_ref, gv_ref, vw_ref, ws_ref, bs_ref, o_ref.at[:, 0:GM_WIDTH])
    _ssd_scan(gate_ref, xs_ref, bc_ref, dt_ref, dtb_ref, alog_ref, dskip_ref, nw_ref, o_ref.at[:, GM_WIDTH:],
              state_ref, y_ref, expand_ref)


def _mixer(proj, dt_raw, v_norm_w, w_s, b_s_t, dt_bias, a_log, d_skip_e, norm_w):
    m = proj.shape[0]
    row = lambda a: a.reshape(1, -1)
    c0 = 2 * GM_WIDTH // SSM_WIDTH
    full = lambda shape: pl.BlockSpec(shape, lambda i: (0,) * len(shape))
    chunk = lambda width, col: pl.BlockSpec((CHUNK, width), lambda i: (i, col))
    return pl.pallas_call(
        _mixer_kernel,
        grid=(m // CHUNK,),
        in_specs=[
            chunk(GM_WIDTH, 0),
            chunk(GM_WIDTH, 1),
            full((1, GM_WIDTH)),
            full((GM_GROUPS, CHUNK, CHUNK)),
            full((CHUNK, LANES)),
            chunk(SSM_WIDTH, c0),
            chunk(SSM_WIDTH, c0 + 1),
            chunk(2 * SSM_BC_WIDTH, c0 + 2),
            chunk(LANES, 0),
            full((1, LANES)),
            full((1, LANES)),
            full((1, SSM_WIDTH)),
            full((1, SSM_WIDTH)),
        ],
        out_specs=chunk(GM_WIDTH + SSM_WIDTH, 0),
        out_shape=jax.ShapeDtypeStruct((m, GM_WIDTH + SSM_WIDTH), BF16),
        scratch_shapes=[pltpu.VMEM((SSM_GROUPS, SSM_STATE, SSM_GROUP_WIDTH), F32),
                        pltpu.VMEM((CHUNK, SSM_WIDTH), F32),
                        pltpu.VMEM((LANES, SSM_WIDTH), BF16)],
        compiler_params=_params(1),
        name="mixer",
    )(proj, proj, row(v_norm_w), w_s, b_s_t,
      proj, proj, proj, dt_raw, row(dt_bias), row(a_log), row(d_skip_e), row(norm_w))


WEIGHT_STAGE_ROWS = 256


def _for_each_row_block(w_hbm, cols, stage_ref, sems, use):
    n_blocks = w_hbm.shape[0] // WEIGHT_STAGE_ROWS

    def block_rows(r):
        return pl.ds(pl.multiple_of(r * WEIGHT_STAGE_ROWS, WEIGHT_STAGE_ROWS), WEIGHT_STAGE_ROWS)

    def copy(r, slot):
        return pltpu.make_async_copy(w_hbm.at[block_rows(r), cols], stage_ref.at[slot], sems.at[slot])

    copy(0, 0).start()

    def body(r, _):
        slot = r % 2

        @pl.when(r + 1 < n_blocks)
        def _():
            copy(r + 1, 1 - slot).start()

        copy(r, slot).wait()
        use(block_rows(r), stage_ref.at[slot])

    lax.fori_loop(0, n_blocks, body, None)


def _project_memory(memn_ref, w_hbm, cols, out_ref, acc_ref, stage_ref, sems):
    acc_ref[...] = jnp.zeros_like(acc_ref)

    def use(rows, block_ref):
        acc_ref[...] += _dot(memn_ref[:, rows], block_ref[...].astype(BF16))

    _for_each_row_block(w_hbm, cols, stage_ref, sems, use)
    out_ref[...] = acc_ref[...].astype(BF16)


def _xattn_kernel(h_ref, mixed_ref, nw_ref, mem_ref, memw_ref, wout_ref, wq_ref, wo_ref, wkv_hbm,
                  o_ref,
                  k_ref, v_ref, stage_ref, sems, h2_ref, q_ref, att_ref):
    d = h_ref.shape[1]

    @pl.when(pl.program_id(0) == 0)
    def _():
        q_ref[...] = _rmsnorm(mem_ref[...], memw_ref[...]).astype(BF16)
        _project_memory(q_ref, wkv_hbm, pl.ds(0, d), k_ref, h2_ref, stage_ref, sems)
        _project_memory(q_ref, wkv_hbm, pl.ds(d, d), v_ref, h2_ref, stage_ref, sems)

    h2_ref[...] = h_ref[...] + _dot(mixed_ref[...], wout_ref[...])
    q_ref[...] = _dot(_rmsnorm(h2_ref[...], nw_ref[...]).astype(BF16), wq_ref[...]).astype(BF16)
    scale = XA_HEADDIM ** -0.5
    for h in range(XA_HEADS):
        cols = slice(h * XA_HEADDIM, (h + 1) * XA_HEADDIM)
        logits = _dot_nt(q_ref[:, cols], k_ref[:, cols]) * scale
        p = jnp.exp(logits - jnp.max(logits, axis=-1, keepdims=True))
        p = p / jnp.sum(p, axis=-1, keepdims=True)
        att_ref[:, cols] = _dot(p.astype(BF16), v_ref[:, cols]).astype(BF16)
    o_ref[...] = h2_ref[...] + _dot(att_ref[...], wo_ref[...])


def _xattn(h, mixed, norm_w, mem, mem_norm_w, w_out, w_q, w_o, w_kv):
    m, d = h.shape
    tm = MEM_LEN
    once = pl.Buffered(1)
    resident = lambda w: pl.BlockSpec(w.shape, lambda i: (0, 0), pipeline_mode=once)
    return pl.pallas_call(
        _xattn_kernel,
        grid=(m // tm,),
        in_specs=[
            pl.BlockSpec((tm, d), lambda i: (i, 0)),
            pl.BlockSpec((tm, mixed.shape[1]), lambda i: (i, 0)),
            pl.BlockSpec((1, d), lambda i: (0, 0)),
            pl.BlockSpec((MEM_LEN, d), lambda i: (0, 0), pipeline_mode=once),
            pl.BlockSpec((1, d), lambda i: (0, 0)),
            resident(w_out), resident(w_q), resident(w_o),
            pl.BlockSpec(memory_space=pl.ANY),
        ],
        out_specs=pl.BlockSpec((tm, d), lambda i: (i, 0)),
        out_shape=jax.ShapeDtypeStruct((m, d), F32),
        scratch_shapes=[pltpu.VMEM((MEM_LEN, d), BF16),
                        pltpu.VMEM((MEM_LEN, d), BF16),
                        pltpu.VMEM((2, WEIGHT_STAGE_ROWS, d), F32),
                        pltpu.SemaphoreType.DMA((2,)),
                        pltpu.VMEM((tm, d), F32),
                        pltpu.VMEM((tm, d), BF16),
                        pltpu.VMEM((tm, d), BF16)],
        compiler_params=_params(1),
        name="xattn",
    )(h, mixed, norm_w.reshape(1, d), mem, mem_norm_w.reshape(1, d), w_out, w_q, w_o, w_kv)


def kernel(x, mem, ffn1_norm, ffn1_w_gu, ffn1_w_down, mix_norm, w_in, gm_v_norm, gm_w_s, gm_b_s, ssm_conv_w, ssm_conv_b, ssm_dt_bias, ssm_a_log, ssm_d, ssm_norm, w_out, xa_norm, mem_norm, xa_w_q, xa_w_kv, xa_w_o, ffn2_norm, ffn2_w_gu, ffn2_w_down, final_norm):
    pad_lanes =lambda a: jnp.pad(a, ((0, 0), (0, LANES - a.shape[1])))
    h = x[0]
    for i in range(ffn1_norm.shape[0]):
        h, xn = _ffn(h, ffn1_norm[i], ffn1_w_gu[i], ffn1_w_down[i], mix_norm[i], tail="next")

        proj, dt_raw, w_out_b, w_q_b, w_o_b = _in_proj(xn, jnp.swapaxes(w_in[i], 0, 1), ssm_conv_w[i], ssm_conv_b[i],
                                                       [w_out[i], xa_w_q[i], xa_w_o[i]])
        mixed = _mixer(proj, dt_raw, gm_v_norm[i], gm_w_s[i], pad_lanes(gm_b_s[i].T),
                       pad_lanes(ssm_dt_bias[i][None])[0], pad_lanes(ssm_a_log[i][None])[0],
                       jnp.repeat(ssm_d[i], SSM_HEADDIM), ssm_norm[i])
        h = _xattn(h, mixed, xa_norm[i], mem[0], mem_norm[i], w_out_b, w_q_b, w_o_b, xa_w_kv[i])

        last = i == ffn1_norm.shape[0] - 1
        h = _ffn(h, ffn2_norm[i], ffn2_w_gu[i], ffn2_w_down[i], final_norm, tail="final" if last else "none")[0]
    return h[None]
```

```python
import functools

import jax
import jax.numpy as jnp
from jax import lax
from jax.experimental import pallas as pl
from jax.experimental.pallas import tpu as pltpu

D_MODEL = 2048
SEQ = 8192
MEM_LEN = 256
GM_WIDTH = 2048
GM_GROUPS = 4
GM_GROUP_WIDTH = GM_WIDTH // GM_GROUPS
CHUNK = 128
SSM_WIDTH = 2048
SSM_HEADDIM = 64
SSM_HEADS = SSM_WIDTH // SSM_HEADDIM
SSM_GROUPS = 8
SSM_HEADS_PER_GROUP = SSM_HEADS // SSM_GROUPS
SSM_GROUP_WIDTH = SSM_HEADS_PER_GROUP * SSM_HEADDIM
SSM_STATE = 128
SSM_CONV = 4
SSM_BC_WIDTH = SSM_GROUPS * SSM_STATE
PROJ_MAIN = 2 * GM_WIDTH + SSM_WIDTH + SSM_WIDTH + 2 * SSM_BC_WIDTH
XA_HEADS = 4
XA_HEADDIM = D_MODEL // XA_HEADS
D_FF = 5632
EPS = 1e-6

LANES = 128
HALO_ROWS = 8
ROW_BLOCK = 128
IN_PROJ_ROW_BLOCK = 256
SIDE_CAST_STEPS = 64
VMEM_LIMIT = 60 * 1024 * 1024

BF16 = jnp.bfloat16
F32 = jnp.float32


def _params(n_axes):
    return pltpu.CompilerParams(dimension_semantics=("arbitrary",) * n_axes,
                                vmem_limit_bytes=VMEM_LIMIT)


def _rmsnorm(x, w):
    return x * lax.rsqrt(jnp.mean(x * x, axis=-1, keepdims=True) + EPS) * w


def _silu(x):
    half = 0.5 * x
    return half + half * jnp.tanh(half)


def _gelu_tanh(x):
    c = 0.7978845608028654
    return 0.5 * x * (1.0 + jnp.tanh(x * (c + (0.044715 * c) * (x * x))))


def _dot(a, b):
    return jnp.dot(a, b, preferred_element_type=F32)


def _dot_nt(a, b):
    return lax.dot_general(a, b, (((1,), (1,)), ((), ())), preferred_element_type=F32)


def _dot_tn(a, b):
    return lax.dot_general(a, b, (((0,), (0,)), ((), ())), preferred_element_type=F32)


def _ffn_kernel(x_hbm, nw_ref, wg_ref, wu_ref, wd_ref, tw_ref, o_ref, *rest, tail):
    xn_ref, sems = rest[-2], rest[-1]
    i = pl.program_id(0)
    j = pl.program_id(1)
    tm = o_ref.shape[0]
    n_row_blocks = tm // ROW_BLOCK

    def row_block(r):
        return pl.ds(pl.multiple_of(r * ROW_BLOCK, ROW_BLOCK), ROW_BLOCK)

    @pl.when(j == 0)
    def _():
        def copy(r):
            src_rows = pl.ds(pl.multiple_of(i * tm + r * ROW_BLOCK, ROW_BLOCK), ROW_BLOCK)
            return pltpu.make_async_copy(x_hbm.at[src_rows, :], o_ref.at[r * ROW_BLOCK:(r + 1) * ROW_BLOCK, :],
                                         sems.at[r])

        for r in range(n_row_blocks):
            copy(r).start()
        for r in range(n_row_blocks):
            copy(r).wait()
            rows = slice(r * ROW_BLOCK, (r + 1) * ROW_BLOCK)
            xn_ref[rows, :] = _rmsnorm(o_ref[rows, :], nw_ref[...]).astype(BF16)

    xn = xn_ref[...]
    g = _dot(xn, wg_ref[...].astype(BF16))
    u = _dot(xn, wu_ref[...].astype(BF16))
    o_ref[...] += _dot((_silu(g) * u).astype(BF16), (0.5 * wd_ref[...]).astype(BF16))

    if tail != "none":
        @pl.when(j == pl.num_programs(1) - 1)
        def _():
            def body(r, _):
                rows = row_block(r)
                hn = _rmsnorm(o_ref[rows, :], tw_ref[...])
                if tail == "final":
                    o_ref[rows, :] = hn
                else:
                    rest[0][rows, :] = hn.astype(BF16)
            lax.fori_loop(0, n_row_blocks, body, None)


def _ffn(x, norm_w, w_gu, w_down, tail_w, *, tail, tm=1024, tf=512):
    m, d = x.shape
    f = w_down.shape[0]
    nf = f // tf
    row_tile = pl.BlockSpec((tm, d), lambda i, j: (i, 0))
    out_specs, out_shape = [row_tile], [jax.ShapeDtypeStruct((m, d), F32)]
    if tail == "next":
        out_specs.append(row_tile)
        out_shape.append(jax.ShapeDtypeStruct((m, d), BF16))
    return pl.pallas_call(
        functools.partial(_ffn_kernel, tail=tail),
        grid=(m // tm, nf),
        in_specs=[
            pl.BlockSpec(memory_space=pl.ANY),
            pl.BlockSpec((1, d), lambda i, j: (0, 0)),
            pl.BlockSpec((d, tf), lambda i, j: (0, j)),
            pl.BlockSpec((d, tf), lambda i, j: (0, j + nf)),
            pl.BlockSpec((tf, d), lambda i, j: (j, 0)),
            pl.BlockSpec((1, d), lambda i, j: (0, 0)),
        ],
        out_specs=out_specs,
        out_shape=out_shape,
        scratch_shapes=[pltpu.VMEM((tm, d), BF16), pltpu.SemaphoreType.DMA((tm // ROW_BLOCK,))],
        compiler_params=_params(2),
        name="ffn",
    )(x, norm_w.reshape(1, d), w_gu, w_gu, w_down, tail_w.reshape(1, d))


def _in_proj_kernel(xn_ref, wt_ref, wdt_ref, cw_ref, cb_ref, *rest, gelu_tiles, silu_tiles, n_side):
    side_in, (o_ref, dt_ref) = rest[:n_side], rest[n_side:n_side + 2]
    side_out = rest[n_side + 2:2 * n_side + 2]
    raw_even_ref, raw_odd_ref, halo_ref = rest[2 * n_side + 2:]
    for src_ref, dst_ref in zip(side_in, side_out):
        dst_ref[...] = src_ref[...].astype(BF16)

    i = pl.program_id(0)
    j = pl.program_id(1)
    conv_first = gelu_tiles + silu_tiles
    raw_refs = (raw_even_ref, raw_odd_ref)

    @pl.when(j == 0)
    def _():
        dt_ref[...] = _dot_nt(xn_ref[...], wdt_ref[...].astype(BF16))

    tm = o_ref.shape[0]

    def project(consume):
        w = wt_ref[...].astype(BF16)
        product = lambda r0: _dot_nt(xn_ref[r0:r0 + IN_PROJ_ROW_BLOCK, :], w)
        ahead = product(0)
        for r0 in range(0, tm, IN_PROJ_ROW_BLOCK):
            current = ahead
            if r0 + IN_PROJ_ROW_BLOCK < tm:
                ahead = product(r0 + IN_PROJ_ROW_BLOCK)
            consume(r0, current)

    def store_with(activation):
        def consume(r0, acc):
            o_ref[r0:r0 + IN_PROJ_ROW_BLOCK, :] = activation(acc)
        return consume

    @pl.when(j < gelu_tiles)
    def _():
        project(store_with(_gelu_tanh))

    @pl.when((j >= gelu_tiles) & (j < gelu_tiles + silu_tiles))
    def _():
        project(store_with(_silu))

    @pl.when(j >= conv_first)
    def _():
        c = j - conv_first
        tail = slice(IN_PROJ_ROW_BLOCK, IN_PROJ_ROW_BLOCK + HALO_ROWS)
        n_blocks = tm // IN_PROJ_ROW_BLOCK

        @pl.when(i == 0)
        def _():
            raw_refs[0][0:HALO_ROWS, :] = jnp.zeros((HALO_ROWS, o_ref.shape[1]), F32)

        @pl.when(i > 0)
        def _():
            raw_refs[0][0:HALO_ROWS, :] = halo_ref[c]

        def conv_silu(r0, acc):
            block = r0 // IN_PROJ_ROW_BLOCK
            raw, previous = raw_refs[block % 2], raw_refs[(block - 1) % 2]
            if block > 0:
                raw[0:HALO_ROWS, :] = previous[tail, :]
            raw[HALO_ROWS:, :] = acc
            out = cb_ref[...]
            for tap in range(SSM_CONV):
                off = HALO_ROWS - (SSM_CONV - 1) + tap
                out = out + raw[off:off + IN_PROJ_ROW_BLOCK, :] * cw_ref[tap:tap + 1, :]
            o_ref[r0:r0 + IN_PROJ_ROW_BLOCK, :] = _silu(out)

        project(conv_silu)
        halo_ref[c] = raw_refs[(n_blocks - 1) % 2][tail, :]


def _in_proj(xn, w_in_t, conv_w, conv_b, side_weights, *, tm=2048, tn=512):
    m, k = xn.shape
    n_col = PROJ_MAIN // tn
    gelu_tiles, silu_tiles = 2 * GM_WIDTH // tn, SSM_WIDTH // tn
    conv_tiles = (SSM_WIDTH + 2 * SSM_BC_WIDTH) // tn
    kernel_fn = functools.partial(_in_proj_kernel, gelu_tiles=gelu_tiles, silu_tiles=silu_tiles,
                                  n_side=len(side_weights))
    assert SIDE_CAST_STEPS <= (m // tm) * n_col
    side_specs = [pl.BlockSpec((w.shape[0] // SIDE_CAST_STEPS, w.shape[1]),
                               lambda i, j: (jnp.minimum(i * n_col + j, SIDE_CAST_STEPS - 1), 0))
                  for w in side_weights]
    conv_tile = lambda i, j: (0, jnp.maximum(j - (gelu_tiles + silu_tiles), 0))
    w_dt_t = jnp.pad(w_in_t[PROJ_MAIN:], ((0, LANES - (w_in_t.shape[0] - PROJ_MAIN)), (0, 0)))
    return pl.pallas_call(
        kernel_fn,
        grid=(m // tm, n_col),
        in_specs=[
            pl.BlockSpec((tm, k), lambda i, j: (i, 0)),
            pl.BlockSpec((tn, k), lambda i, j: (j, 0)),
            pl.BlockSpec((LANES, k), lambda i, j: (0, 0)),
            pl.BlockSpec((SSM_CONV, tn), conv_tile),
            pl.BlockSpec((1, tn), conv_tile),
            *side_specs,
        ],
        out_specs=[pl.BlockSpec((tm, tn), lambda i, j: (i, j)),
                   pl.BlockSpec((tm, LANES), lambda i, j: (i, 0)),
                   *side_specs],
        out_shape=[jax.ShapeDtypeStruct((m, PROJ_MAIN), F32), jax.ShapeDtypeStruct((m, LANES), F32),
                   *[jax.ShapeDtypeStruct(w.shape, BF16) for w in side_weights]],
        scratch_shapes=[pltpu.VMEM((HALO_ROWS + IN_PROJ_ROW_BLOCK, tn), F32),
                        pltpu.VMEM((HALO_ROWS + IN_PROJ_ROW_BLOCK, tn), F32),
                        pltpu.VMEM((conv_tiles, HALO_ROWS, tn), F32)],
        compiler_params=_params(2),
        name="in_proj",
    )(xn, w_in_t, w_dt_t, conv_w, conv_b.reshape(1, -1), *side_weights)


def _sgu_mix(gu_ref, gv_ref, vw_ref, ws_ref, bs_ref, o_ref):
    vn = _rmsnorm(gv_ref[...], vw_ref[...]).astype(BF16)
    t_idx = lax.broadcasted_iota(jnp.int32, (CHUNK, CHUNK), 0)
    s_idx = lax.broadcasted_iota(jnp.int32, (CHUNK, CHUNK), 1)
    causal = s_idx <= t_idx
    bias = bs_ref[...]
    for g in range(GM_GROUPS):
        cols = slice(g * GM_GROUP_WIDTH, (g + 1) * GM_GROUP_WIDTH)
        w = jnp.where(causal, ws_ref[g], 0.0).astype(BF16)
        mixed = _dot(w, vn[:, cols]) + bias[:, g:g + 1]
        o_ref[:, cols] = (gu_ref[:, cols] * mixed).astype(o_ref.dtype)


def _split3(v):
    hi = v.astype(BF16).astype(F32)
    rest = v - hi
    mid = rest.astype(BF16).astype(F32)
    return hi, mid, rest - mid


def _pack3(v):
    hi, mid, lo = _split3(v)
    lane = lax.broadcasted_iota(jnp.int32, v.shape, 1)
    packed = jnp.where(lane < SSM_HEADS, hi,
                       jnp.where(lane < 2 * SSM_HEADS, pltpu.roll(mid, SSM_HEADS, 1),
                                 jnp.where(lane < 3 * SSM_HEADS, pltpu.roll(lo, 2 * SSM_HEADS, 1), 0.0)))
    return packed.astype(BF16)


def _ssd_init(state_ref, expand_ref):
    state_ref[...] = jnp.zeros_like(state_ref)
    k_idx = lax.broadcasted_iota(jnp.int32, (LANES, SSM_WIDTH), 0)
    j_idx = lax.broadcasted_iota(jnp.int32, (LANES, SSM_WIDTH), 1)
    hit = (k_idx < 3 * SSM_HEADS) & (j_idx // SSM_HEADDIM == k_idx % SSM_HEADS)
    expand_ref[...] = hit.astype(F32).astype(BF16)


def _ssd_scan(gate_ref, xs_ref, bc_ref, dt_ref, dtb_ref, alog_ref, dskip_ref, nw_ref, o_ref,
              state_ref, y_ref, expand_ref):
    pre = dt_ref[...] + dtb_ref[...]
    dt = jnp.maximum(pre, 0.0) + jnp.log1p(jnp.exp(-jnp.abs(pre)))
    da = dt * (-jnp.exp(alog_ref[...]))
    row = lax.broadcasted_iota(jnp.int32, (CHUNK, CHUNK), 0)
    col = lax.broadcasted_iota(jnp.int32, (CHUNK, CHUNK), 1)
    causal = col <= row
    tri = causal.astype(F32).astype(BF16)
    da_hi, da_mid, da_lo = _split3(da)
    parts = _dot(tri, jnp.concatenate([da_hi, da_mid, da_lo], axis=1).astype(BF16))
    cum = parts[:, :LANES] + parts[:, LANES:2 * LANES] + parts[:, 2 * LANES:]
    cum_t = cum.T
    cum_last = cum[CHUNK - 1:CHUNK, :]

    packed = jnp.concatenate([_pack3(dt), _pack3(dt * jnp.exp(cum_last - cum)), _pack3(jnp.exp(cum))], axis=0)
    expanded = _dot(packed, expand_ref[...])
    dt_e = expanded[0:CHUNK]
    dt_decay_end_e = expanded[CHUNK:2 * CHUNK]
    exp_cum_e = expanded[2 * CHUNK:]
    chunk_decay_e = exp_cum_e[CHUNK - 1:CHUNK, :]

    xs = xs_ref[...]
    xdt = (xs * dt_e).astype(BF16)
    xdt_end = (xs * dt_decay_end_e).astype(BF16)
    lane_head = lax.broadcasted_iota(jnp.int32, (1, SSM_GROUP_WIDTH), 1) // SSM_HEADDIM

    for g in range(SSM_GROUPS):
        gcols = slice(g * SSM_GROUP_WIDTH, (g + 1) * SSM_GROUP_WIDTH)
        b_g = bc_ref[:, g * SSM_STATE:(g + 1) * SSM_STATE].astype(BF16)
        c_g = bc_ref[:, SSM_BC_WIDTH + g * SSM_STATE:SSM_BC_WIDTH + (g + 1) * SSM_STATE].astype(BF16)
        scores = _dot_nt(c_g, b_g)
        xdt_g = xdt[:, gcols]
        y = _dot(c_g, state_ref[g].astype(BF16)) * exp_cum_e[:, gcols]
        for r in range(SSM_HEADS_PER_GROUP):
            h = g * SSM_HEADS_PER_GROUP + r
            seg = cum[:, h:h + 1] - cum_t[h:h + 1, :]
            decay = jnp.exp(jnp.where(causal, seg, -jnp.inf))
            x_r = xdt_g * (lane_head == r).astype(F32).astype(BF16)
            y = y + _dot((scores * decay).astype(BF16), x_r)
        state_ref[g] = state_ref[g] * chunk_decay_e[:, gcols] + _dot_tn(b_g, xdt_end[:, gcols])
        y_ref[:, gcols] = y

    y = y_ref[...] + dskip_ref[...] * xs
    y = y * gate_ref[...]
    o_ref[...] = _rmsnorm(y, nw_ref[...]).astype(o_ref.dtype)


def _mixer_kernel(gu_ref, gv_ref, vw_ref, ws_ref, bs_ref,
                  gate_ref, xs_ref, bc_ref, dt_ref, dtb_ref, alog_ref, dskip_ref, nw_ref,
                  o_ref,
                  state_ref, y_ref, expand_ref):
    @pl.when(pl.program_id(0) == 0)
    def _():
        _ssd_init(state_ref, expand_ref)

    _sgu_mix(gu_ref, gv_ref, vw_ref, ws_ref, bs_ref, o_ref.at[:, 0:GM_WIDTH])
    _ssd_scan(gate_ref, xs_ref, bc_ref, dt_ref, dtb_ref, alog_ref, dskip_ref, nw_ref, o_ref.at[:, GM_WIDTH:],
              state_ref, y_ref, expand_ref)


def _mixer(proj, dt_raw, v_norm_w, w_s, b_s_t, dt_bias, a_log, d_skip_e, norm_w):
    m = proj.shape[0]
    row = lambda a: a.reshape(1, -1)
    c0 = 2 * GM_WIDTH // SSM_WIDTH
    full = lambda shape: pl.BlockSpec(shape, lambda i: (0,) * len(shape))
    chunk = lambda width, col: pl.BlockSpec((CHUNK, width), lambda i: (i, col))
    return pl.pallas_call(
        _mixer_kernel,
        grid=(m // CHUNK,),
        in_specs=[
            chunk(GM_WIDTH, 0),
            chunk(GM_WIDTH, 1),
            full((1, GM_WIDTH)),
            full((GM_GROUPS, CHUNK, CHUNK)),
            full((CHUNK, LANES)),
            chunk(SSM_WIDTH, c0),
            chunk(SSM_WIDTH, c0 + 1),
            chunk(2 * SSM_BC_WIDTH, c0 + 2),
            chunk(LANES, 0),
            full((1, LANES)),
            full((1, LANES)),
            full((1, SSM_WIDTH)),
            full((1, SSM_WIDTH)),
        ],
        out_specs=chunk(GM_WIDTH + SSM_WIDTH, 0),
        out_shape=jax.ShapeDtypeStruct((m, GM_WIDTH + SSM_WIDTH), BF16),
        scratch_shapes=[pltpu.VMEM((SSM_GROUPS, SSM_STATE, SSM_GROUP_WIDTH), F32),
                        pltpu.VMEM((CHUNK, SSM_WIDTH), F32),
                        pltpu.VMEM((LANES, SSM_WIDTH), BF16)],
        compiler_params=_params(1),
        name="mixer",
    )(proj, proj, row(v_norm_w), w_s, b_s_t,
      proj, proj, proj, dt_raw, row(dt_bias), row(a_log), row(d_skip_e), row(norm_w))


WEIGHT_STAGE_ROWS = 256


def _for_each_row_block(w_hbm, cols, stage_ref, sems, use):
    n_blocks = w_hbm.shape[0] // WEIGHT_STAGE_ROWS

    def block_rows(r):
        return pl.ds(pl.multiple_of(r * WEIGHT_STAGE_ROWS, WEIGHT_STAGE_ROWS), WEIGHT_STAGE_ROWS)

    def copy(r, slot):
        return pltpu.make_async_copy(w_hbm.at[block_rows(r), cols], stage_ref.at[slot], sems.at[slot])

    copy(0, 0).start()

    def body(r, _):
        slot = r % 2

        @pl.when(r + 1 < n_blocks)
        def _():
            copy(r + 1, 1 - slot).start()

        copy(r, slot).wait()
        use(block_rows(r), stage_ref.at[slot])

    lax.fori_loop(0, n_blocks, body, None)


def _project_memory(memn_ref, w_hbm, cols, out_ref, acc_ref, stage_ref, sems):
    acc_ref[...] = jnp.zeros_like(acc_ref)

    def use(rows, block_ref):
        acc_ref[...] += _dot(memn_ref[:, rows], block_ref[...].astype(BF16))

    _for_each_row_block(w_hbm, cols, stage_ref, sems, use)
    out_ref[...] = acc_ref[...].astype(BF16)


def _xattn_kernel(h_ref, mixed_ref, nw_ref, mem_ref, memw_ref, wout_ref, wq_ref, wo_ref, wkv_hbm,
                  o_ref,
                  k_ref, v_ref, stage_ref, sems, h2_ref, q_ref, att_ref):
    d = h_ref.shape[1]

    @pl.when(pl.program_id(0) == 0)
    def _():
        q_ref[...] = _rmsnorm(mem_ref[...], memw_ref[...]).astype(BF16)
        _project_memory(q_ref, wkv_hbm, pl.ds(0, d), k_ref, h2_ref, stage_ref, sems)
        _project_memory(q_ref, wkv_hbm, pl.ds(d, d), v_ref, h2_ref, stage_ref, sems)

    h2_ref[...] = h_ref[...] + _dot(mixed_ref[...], wout_ref[...])
    q_ref[...] = _dot(_rmsnorm(h2_ref[...], nw_ref[...]).astype(BF16), wq_ref[...]).astype(BF16)
    scale = XA_HEADDIM ** -0.5
    for h in range(XA_HEADS):
        cols = slice(h * XA_HEADDIM, (h + 1) * XA_HEADDIM)
        logits = _dot_nt(q_ref[:, cols], k_ref[:, cols]) * scale
        p = jnp.exp(logits - jnp.max(logits, axis=-1, keepdims=True))
        p = p / jnp.sum(p, axis=-1, keepdims=True)
        att_ref[:, cols] = _dot(p.astype(BF16), v_ref[:, cols]).astype(BF16)
    o_ref[...] = h2_ref[...] + _dot(att_ref[...], wo_ref[...])


def _xattn(h, mixed, norm_w, mem, mem_norm_w, w_out, w_q, w_o, w_kv):
    m, d = h.shape
    tm = MEM_LEN
    once = pl.Buffered(1)
    resident = lambda w: pl.BlockSpec(w.shape, lambda i: (0, 0), pipeline_mode=once)
    return pl.pallas_call(
        _xattn_kernel,
        grid=(m // tm,),
        in_specs=[
            pl.BlockSpec((tm, d), lambda i: (i, 0)),
            pl.BlockSpec((tm, mixed.shape[1]), lambda i: (i, 0)),
            pl.BlockSpec((1, d), lambda i: (0, 0)),
            pl.BlockSpec((MEM_LEN, d), lambda i: (0, 0), pipeline_mode=once),
            pl.BlockSpec((1, d), lambda i: (0, 0)),
            resident(w_out), resident(w_q), resident(w_o),
            pl.BlockSpec(memory_space=pl.ANY),
        ],
        out_specs=pl.BlockSpec((tm, d), lambda i: (i, 0)),
        out_shape=jax.ShapeDtypeStruct((m, d), F32),
        scratch_shapes=[pltpu.VMEM((MEM_LEN, d), BF16),
                        pltpu.VMEM((MEM_LEN, d), BF16),
                        pltpu.VMEM((2, WEIGHT_STAGE_ROWS, d), F32),
                        pltpu.SemaphoreType.DMA((2,)),
                        pltpu.VMEM((tm, d), F32),
                        pltpu.VMEM((tm, d), BF16),
                        pltpu.VMEM((tm, d), BF16)],
        compiler_params=_params(1),
        name="xattn",
    )(h, mixed, norm_w.reshape(1, d), mem, mem_norm_w.reshape(1, d), w_out, w_q, w_o, w_kv)


def kernel(x, mem, ffn1_norm, ffn1_w_gu, ffn1_w_down, mix_norm, w_in, gm_v_norm, gm_w_s, gm_b_s, ssm_conv_w, ssm_conv_b, ssm_dt_bias, ssm_a_log, ssm_d, ssm_norm, w_out, xa_norm, mem_norm, xa_w_q, xa_w_kv, xa_w_o, ffn2_norm, ffn2_w_gu, ffn2_w_down, final_norm):
    pad_lanes = lambda a: jnp.pad(a, ((0, 0), (0, LANES - a.shape[1])))
    h = x[0]
    for i in range(ffn1_norm.shape[0]):
        h, xn = _ffn(h, ffn1_norm[i], ffn1_w_gu[i], ffn1_w_down[i], mix_norm[i], tail="next")

        proj, dt_raw, w_out_b, w_q_b, w_o_b = _in_proj(xn, jnp.swapaxes(w_in[i], 0, 1), ssm_conv_w[i], ssm_conv_b[i],
                                                       [w_out[i], xa_w_q[i], xa_w_o[i]])
        mixed = _mixer(proj, dt_raw, gm_v_norm[i], gm_w_s[i], pad_lanes(gm_b_s[i].T),
                       pad_lanes(ssm_dt_bias[i][None])[0], pad_lanes(ssm_a_log[i][None])[0],
                       jnp.repeat(ssm_d[i], SSM_HEADDIM), ssm_norm[i])
        h = _xattn(h, mixed, xa_norm[i], mem[0], mem_norm[i], w_out_b, w_q_b, w_o_b, xa_w_kv[i])

        last = i == ffn1_norm.shape[0] - 1
        h = _ffn(h, ffn2_norm[i], ffn2_w_gu[i], ffn2_w_down[i], final_norm, tail="final" if last else "none")[0]
    return h[None]
```

```python
import functools

import jax
import jax.numpy as jnp
from jax import lax
from jax.experimental import pallas as pl
from jax.experimental.pallas import tpu as pltpu

D_MODEL = 2048
MEM_LEN = 256
GM_WIDTH = 2048
GM_GROUPS = 4
GM_GROUP_WIDTH = GM_WIDTH // GM_GROUPS
CHUNK = 128
SSM_WIDTH = 2048
SSM_HEADDIM = 64
SSM_HEADS = SSM_WIDTH // SSM_HEADDIM
SSM_GROUPS = 8
SSM_HEADS_PER_GROUP = SSM_HEADS // SSM_GROUPS
SSM_GROUP_WIDTH = SSM_HEADS_PER_GROUP * SSM_HEADDIM
SSM_STATE = 128
SSM_CONV = 4
SSM_BC_WIDTH = SSM_GROUPS * SSM_STATE
PROJ_MAIN = 2 * GM_WIDTH + SSM_WIDTH + SSM_WIDTH + 2 * SSM_BC_WIDTH
XA_HEADS = 4
XA_HEADDIM = D_MODEL // XA_HEADS
EPS = 1e-6

LANES = 128
HALO_ROWS = 8
ROW_BLOCK = 128
IN_PROJ_ROW_BLOCK = 256
SIDE_CAST_STEPS = 64
VMEM_LIMIT = 60 * 1024 * 1024

BF16 = jnp.bfloat16
F32 = jnp.float32


def _params(n_axes):
    return pltpu.CompilerParams(dimension_semantics=("arbitrary",) * n_axes,
                                vmem_limit_bytes=VMEM_LIMIT)


def _rmsnorm(x, w):
    return x * lax.rsqrt(jnp.mean(x * x, axis=-1, keepdims=True) + EPS) * w


def _silu(x):
    half = 0.5 * x
    return half + half * jnp.tanh(half)


def _gelu_tanh(x):
    c = 0.7978845608028654
    return 0.5 * x * (1.0 + jnp.tanh(x * (c + (0.044715 * c) * (x * x))))


def _dot(a, b):
    return jnp.dot(a, b, preferred_element_type=F32)


def _dot_nt(a, b):
    return lax.dot_general(a, b, (((1,), (1,)), ((), ())), preferred_element_type=F32)


def _dot_tn(a, b):
    return lax.dot_general(a, b, (((0,), (0,)), ((), ())), preferred_element_type=F32)


def _ffn_kernel(x_ref, nw_ref, wg_ref, wu_ref, wd_ref, tw_ref, o_ref, *rest, tail):
    xn_ref = rest[-1]
    j = pl.program_id(1)
    n_row_blocks = x_ref.shape[0] // ROW_BLOCK

    def row_block(r):
        return pl.ds(pl.multiple_of(r * ROW_BLOCK, ROW_BLOCK), ROW_BLOCK)

    @pl.when(j == 0)
    def _():
        def body(r, _):
            rows = row_block(r)
            x = x_ref[rows, :]
            xn_ref[rows, :] = _rmsnorm(x, nw_ref[...]).astype(BF16)
            o_ref[rows, :] = x
        lax.fori_loop(0, n_row_blocks, body, None)

    xn = xn_ref[...]
    g = _dot(xn, wg_ref[...].astype(BF16))
    u = _dot(xn, wu_ref[...].astype(BF16))
    o_ref[...] += _dot((_silu(g) * u).astype(BF16), (0.5 * wd_ref[...]).astype(BF16))

    if tail != "none":
        @pl.when(j == pl.num_programs(1) - 1)
        def _():
            def body(r, _):
                rows = row_block(r)
                hn = _rmsnorm(o_ref[rows, :], tw_ref[...])
                if tail == "final":
                    o_ref[rows, :] = hn
                else:
                    rest[0][rows, :] = hn.astype(BF16)
            lax.fori_loop(0, n_row_blocks, body, None)


def _ffn(x, norm_w, w_gu, w_down, tail_w, *, tail, tm=1024, tf=256):
    m, d = x.shape
    f = w_down.shape[0]
    nf = f // tf
    row_tile = pl.BlockSpec((tm, d), lambda i, j: (i, 0))
    out_specs, out_shape = [row_tile], [jax.ShapeDtypeStruct((m, d), F32)]
    if tail == "next":
        out_specs.append(row_tile)
        out_shape.append(jax.ShapeDtypeStruct((m, d), BF16))
    return pl.pallas_call(
        functools.partial(_ffn_kernel, tail=tail),
        grid=(m // tm, nf),
        in_specs=[
            row_tile,
            pl.BlockSpec((1, d), lambda i, j: (0, 0)),
            pl.BlockSpec((d, tf), lambda i, j: (0, j)),
            pl.BlockSpec((d, tf), lambda i, j: (0, j + nf)),
            pl.BlockSpec((tf, d), lambda i, j: (j, 0)),
            pl.BlockSpec((1, d), lambda i, j: (0, 0)),
        ],
        out_specs=out_specs,
        out_shape=out_shape,
        scratch_shapes=[pltpu.VMEM((tm, d), BF16)],
        compiler_params=_params(2),
        name="ffn",
    )(x, norm_w.reshape(1, d), w_gu, w_gu, w_down, tail_w.reshape(1, d))


def _in_proj_kernel(xn_ref, wt_ref, wdt_ref, cw_ref, cb_ref, *rest, gelu_tiles, silu_tiles, n_side):
    side_in, (o_ref, dt_ref) = rest[:n_side], rest[n_side:n_side + 2]
    side_out = rest[n_side + 2:2 * n_side + 2]
    raw_even_ref, raw_odd_ref, halo_ref = rest[2 * n_side + 2:]
    for src_ref, dst_ref in zip(side_in, side_out):
        dst_ref[...] = src_ref[...].astype(BF16)

    i = pl.program_id(0)
    j = pl.program_id(1)
    conv_first = gelu_tiles + silu_tiles
    raw_refs = (raw_even_ref, raw_odd_ref)

    @pl.when(j == 0)
    def _():
        dt_ref[...] = _dot_nt(xn_ref[...], wdt_ref[...].astype(BF16))

    tm = o_ref.shape[0]

    def project(consume):
        w = wt_ref[...].astype(BF16)
        product = lambda r0: _dot_nt(xn_ref[r0:r0 + IN_PROJ_ROW_BLOCK, :], w)
        ahead = product(0)
        for r0 in range(0, tm, IN_PROJ_ROW_BLOCK):
            current = ahead
            if r0 + IN_PROJ_ROW_BLOCK < tm:
                ahead = product(r0 + IN_PROJ_ROW_BLOCK)
            consume(r0, current)

    def store_with(activation):
        def consume(r0, acc):
            o_ref[r0:r0 + IN_PROJ_ROW_BLOCK, :] = activation(acc)
        return consume

    @pl.when(j < gelu_tiles)
    def _():
        project(store_with(_gelu_tanh))

    @pl.when((j >= gelu_tiles) & (j < gelu_tiles + silu_tiles))
    def _():
        project(store_with(_silu))

    @pl.when(j >= conv_first)
    def _():
        c = j - conv_first
        tail = slice(IN_PROJ_ROW_BLOCK, IN_PROJ_ROW_BLOCK + HALO_ROWS)
        n_blocks = tm // IN_PROJ_ROW_BLOCK

        @pl.when(i == 0)
        def _():
            raw_refs[0][0:HALO_ROWS, :] = jnp.zeros((HALO_ROWS, o_ref.shape[1]), F32)

        @pl.when(i > 0)
        def _():
            raw_refs[0][0:HALO_ROWS, :] = halo_ref[c]

        def conv_silu(r0, acc):
            block = r0 // IN_PROJ_ROW_BLOCK
            raw, previous = raw_refs[block % 2], raw_refs[(block - 1) % 2]
            if block > 0:
                raw[0:HALO_ROWS, :] = previous[tail, :]
            raw[HALO_ROWS:, :] = acc
            out = cb_ref[...]
            for tap in range(SSM_CONV):
                off = HALO_ROWS - (SSM_CONV - 1) + tap
                out = out + raw[off:off + IN_PROJ_ROW_BLOCK, :] * cw_ref[tap:tap + 1, :]
            o_ref[r0:r0 + IN_PROJ_ROW_BLOCK, :] = _silu(out)

        project(conv_silu)
        halo_ref[c] = raw_refs[(n_blocks - 1) % 2][tail, :]


def _in_proj(xn, w_in_t, conv_w, conv_b, side_weights, *, tm=2048, tn=512):
    m, k = xn.shape
    n_col = PROJ_MAIN // tn
    gelu_tiles, silu_tiles = 2 * GM_WIDTH // tn, SSM_WIDTH // tn
    conv_tiles = (SSM_WIDTH + 2 * SSM_BC_WIDTH) // tn
    kernel_fn = functools.partial(_in_proj_kernel, gelu_tiles=gelu_tiles, silu_tiles=silu_tiles,
                                  n_side=len(side_weights))
    assert SIDE_CAST_STEPS <= (m // tm) * n_col
    side_specs = [pl.BlockSpec((w.shape[0] // SIDE_CAST_STEPS, w.shape[1]),
                               lambda i, j: (jnp.minimum(i * n_col + j, SIDE_CAST_STEPS - 1), 0))
                  for w in side_weights]
    conv_tile = lambda i, j: (0, jnp.maximum(j - (gelu_tiles + silu_tiles), 0))
    w_dt_t = jnp.pad(w_in_t[PROJ_MAIN:], ((0, LANES - (w_in_t.shape[0] - PROJ_MAIN)), (0, 0)))
    return pl.pallas_call(
        kernel_fn,
        grid=(m // tm, n_col),
        in_specs=[
            pl.BlockSpec((tm, k), lambda i, j: (i, 0)),
            pl.BlockSpec((tn, k), lambda i, j: (j, 0)),
            pl.BlockSpec((LANES, k), lambda i, j: (0, 0)),
            pl.BlockSpec((SSM_CONV, tn), conv_tile),
            pl.BlockSpec((1, tn), conv_tile),
            *side_specs,
        ],
        out_specs=[pl.BlockSpec((tm, tn), lambda i, j: (i, j)),
                   pl.BlockSpec((tm, LANES), lambda i, j: (i, 0)),
                   *side_specs],
        out_shape=[jax.ShapeDtypeStruct((m, PROJ_MAIN), F32), jax.ShapeDtypeStruct((m, LANES), F32),
                   *[jax.ShapeDtypeStruct(w.shape, BF16) for w in side_weights]],
        scratch_shapes=[pltpu.VMEM((HALO_ROWS + IN_PROJ_ROW_BLOCK, tn), F32),
                        pltpu.VMEM((HALO_ROWS + IN_PROJ_ROW_BLOCK, tn), F32),
                        pltpu.VMEM((conv_tiles, HALO_ROWS, tn), F32)],
        compiler_params=_params(2),
        name="in_proj",
    )(xn, w_in_t, w_dt_t, conv_w, conv_b.reshape(1, -1), *side_weights)


def _sgu_mix(gu_ref, gv_ref, vw_ref, ws_ref, bs_ref, o_ref):
    vn = _rmsnorm(gv_ref[...], vw_ref[...]).astype(BF16)
    t_idx = lax.broadcasted_iota(jnp.int32, (CHUNK, CHUNK), 0)
    s_idx = lax.broadcasted_iota(jnp.int32, (CHUNK, CHUNK), 1)
    causal = s_idx <= t_idx
    bias = bs_ref[...]
    for g in range(GM_GROUPS):
        cols = slice(g * GM_GROUP_WIDTH, (g + 1) * GM_GROUP_WIDTH)
        w = jnp.where(causal, ws_ref[g], 0.0).astype(BF16)
        mixed = _dot(w, vn[:, cols]) + bias[:, g:g + 1]
        o_ref[:, cols] = (gu_ref[:, cols] * mixed).astype(o_ref.dtype)


def _split3(v):
    hi = v.astype(BF16).astype(F32)
    rest = v - hi
    mid = rest.astype(BF16).astype(F32)
    return hi, mid, rest - mid


def _pack3(v):
    hi, mid, lo = _split3(v)
    lane = lax.broadcasted_iota(jnp.int32, v.shape, 1)
    packed = jnp.where(lane < SSM_HEADS, hi,
                       jnp.where(lane < 2 * SSM_HEADS, pltpu.roll(mid, SSM_HEADS, 1),
                                 jnp.where(lane < 3 * SSM_HEADS, pltpu.roll(lo, 2 * SSM_HEADS, 1), 0.0)))
    return packed.astype(BF16)


def _ssd_init(state_ref, expand_ref):
    state_ref[...] = jnp.zeros_like(state_ref)
    k_idx = lax.broadcasted_iota(jnp.int32, (LANES, SSM_WIDTH), 0)
    j_idx = lax.broadcasted_iota(jnp.int32, (LANES, SSM_WIDTH), 1)
    hit = (k_idx < 3 * SSM_HEADS) & (j_idx // SSM_HEADDIM == k_idx % SSM_HEADS)
    expand_ref[...] = hit.astype(F32).astype(BF16)


def _ssd_scan(gate_ref, xs_ref, bc_ref, dt_ref, dtb_ref, alog_ref, dskip_ref, nw_ref, o_ref,
              state_ref, y_ref, expand_ref):
    pre = dt_ref[...] + dtb_ref[...]
    dt = jnp.maximum(pre, 0.0) + jnp.log1p(jnp.exp(-jnp.abs(pre)))
    da = dt * (-jnp.exp(alog_ref[...]))
    row = lax.broadcasted_iota(jnp.int32, (CHUNK, CHUNK), 0)
    col = lax.broadcasted_iota(jnp.int32, (CHUNK, CHUNK), 1)
    causal = col <= row
    tri = causal.astype(F32).astype(BF16)
    da_hi, da_mid, da_lo = _split3(da)
    parts = _dot(tri, jnp.concatenate([da_hi, da_mid, da_lo], axis=1).astype(BF16))
    cum = parts[:, :LANES] + parts[:, LANES:2 * LANES] + parts[:, 2 * LANES:]
    cum_t = cum.T
    cum_last = cum[CHUNK - 1:CHUNK, :]

    packed = jnp.concatenate([_pack3(dt), _pack3(dt * jnp.exp(cum_last - cum)), _pack3(jnp.exp(cum))], axis=0)
    expanded = _dot(packed, expand_ref[...])
    dt_e = expanded[0:CHUNK]
    dt_decay_end_e = expanded[CHUNK:2 * CHUNK]
    exp_cum_e = expanded[2 * CHUNK:]
    chunk_decay_e = exp_cum_e[CHUNK - 1:CHUNK, :]

    xs = xs_ref[...]
    xdt = (xs * dt_e).astype(BF16)
    xdt_end = (xs * dt_decay_end_e).astype(BF16)
    lane_head = lax.broadcasted_iota(jnp.int32, (1, SSM_GROUP_WIDTH), 1) // SSM_HEADDIM

    for g in range(SSM_GROUPS):
        gcols = slice(g * SSM_GROUP_WIDTH, (g + 1) * SSM_GROUP_WIDTH)
        b_g = bc_ref[:, g * SSM_STATE:(g + 1) * SSM_STATE].astype(BF16)
        c_g = bc_ref[:, SSM_BC_WIDTH + g * SSM_STATE:SSM_BC_WIDTH + (g + 1) * SSM_STATE].astype(BF16)
        scores = _dot_nt(c_g, b_g)
        xdt_g = xdt[:, gcols]
        y = _dot(c_g, state_ref[g].astype(BF16)) * exp_cum_e[:, gcols]
        decayed, stacked = [], []
        for r in range(SSM_HEADS_PER_GROUP):
            h = g * SSM_HEADS_PER_GROUP + r
            seg = cum[:, h:h + 1] - cum_t[h:h + 1, :]
            decay = jnp.exp(jnp.where(causal, seg, -jnp.inf))
            decayed.append((scores * decay).astype(BF16))
            stacked.append(xdt_g * (lane_head == r).astype(F32).astype(BF16))
        y = y + _dot(jnp.concatenate(decayed, axis=1), jnp.concatenate(stacked, axis=0))
        state_ref[g] = state_ref[g] * chunk_decay_e[:, gcols] + _dot_tn(b_g, xdt_end[:, gcols])
        y_ref[:, gcols] = y

    y = y_ref[...] + dskip_ref[...] * xs
    y = y * gate_ref[...]
    o_ref[...] = _rmsnorm(y, nw_ref[...]).astype(o_ref.dtype)


def _mixer_kernel(gu_ref, gv_ref, vw_ref, ws_ref, bs_ref,
                  gate_ref, xs_ref, bc_ref, dt_ref, dtb_ref, alog_ref, dskip_ref, nw_ref,
                  o_ref,
                  state_ref, y_ref, expand_ref):
    @pl.when(pl.program_id(0) == 0)
    def _():
        _ssd_init(state_ref, expand_ref)

    _sgu_mix(gu_ref, gv_ref, vw_ref, ws_ref, bs_ref, o_ref.at[:, 0:GM_WIDTH])
    _ssd_scan(gate_ref, xs_ref, bc_ref, dt_ref, dtb_ref, alog_ref, dskip_ref, nw_ref, o_ref.at[:, GM_WIDTH:],
              state_ref, y_ref, expand_ref)


def _mixer(proj, dt_raw, v_norm_w, w_s, b_s_t, dt_bias, a_log, d_skip_e, norm_w):
    m = proj.shape[0]
    row = lambda a: a.reshape(1, -1)
    c0 = 2 * GM_WIDTH // SSM_WIDTH
    full = lambda shape: pl.BlockSpec(shape, lambda i: (0,) * len(shape))
    chunk = lambda width, col: pl.BlockSpec((CHUNK, width), lambda i: (i, col))
    return pl.pallas_call(
        _mixer_kernel,
        grid=(m // CHUNK,),
        in_specs=[
            chunk(GM_WIDTH, 0),
            chunk(GM_WIDTH, 1),
            full((1, GM_WIDTH)),
            full((GM_GROUPS, CHUNK, CHUNK)),
            full((CHUNK, LANES)),
            chunk(SSM_WIDTH, c0),
            chunk(SSM_WIDTH, c0 + 1),
            chunk(2 * SSM_BC_WIDTH, c0 + 2),
            chunk(LANES, 0),
            full((1, LANES)),
            full((1, LANES)),
            full((1, SSM_WIDTH)),
            full((1, SSM_WIDTH)),
        ],
        out_specs=chunk(GM_WIDTH + SSM_WIDTH, 0),
        out_shape=jax.ShapeDtypeStruct((m, GM_WIDTH + SSM_WIDTH), BF16),
        scratch_shapes=[pltpu.VMEM((SSM_GROUPS, SSM_STATE, SSM_GROUP_WIDTH), F32),
                        pltpu.VMEM((CHUNK, SSM_WIDTH), F32),
                        pltpu.VMEM((LANES, SSM_WIDTH), BF16)],
        compiler_params=_params(1),
        name="mixer",
    )(proj, proj, row(v_norm_w), w_s, b_s_t,
      proj, proj, proj, dt_raw, row(dt_bias), row(a_log), row(d_skip_e), row(norm_w))


WEIGHT_STAGE_ROWS = 256


def _for_each_row_block(w_hbm, cols, stage_ref, sems, use):
    n_blocks = w_hbm.shape[0] // WEIGHT_STAGE_ROWS

    def block_rows(r):
        return pl.ds(pl.multiple_of(r * WEIGHT_STAGE_ROWS, WEIGHT_STAGE_ROWS), WEIGHT_STAGE_ROWS)

    def copy(r, slot):
        return pltpu.make_async_copy(w_hbm.at[block_rows(r), cols], stage_ref.at[slot], sems.at[slot])

    copy(0, 0).start()

    def body(r, _):
        slot = r % 2

        @pl.when(r + 1 < n_blocks)
        def _():
            copy(r + 1, 1 - slot).start()

        copy(r, slot).wait()
        use(block_rows(r), stage_ref.at[slot])

    lax.fori_loop(0, n_blocks, body, None)


def _project_memory(memn_ref, w_hbm, cols, out_ref, acc_ref, stage_ref, sems):
    acc_ref[...] = jnp.zeros_like(acc_ref)

    def use(rows, block_ref):
        acc_ref[...] += _dot(memn_ref[:, rows], block_ref[...].astype(BF16))

    _for_each_row_block(w_hbm, cols, stage_ref, sems, use)
    out_ref[...] = acc_ref[...].astype(BF16)


def _xattn_kernel(h_ref, mixed_ref, nw_ref, mem_ref, memw_ref, wout_ref, wq_ref, wo_ref, wkv_hbm,
                  o_ref,
                  k_ref, v_ref, stage_ref, sems, h2_ref, q_ref, att_ref):
    d = h_ref.shape[1]

    @pl.when(pl.program_id(0) == 0)
    def _():
        q_ref[...] = _rmsnorm(mem_ref[...], memw_ref[...]).astype(BF16)
        _project_memory(q_ref, wkv_hbm, pl.ds(0, d), k_ref, h2_ref, stage_ref, sems)
        _project_memory(q_ref, wkv_hbm, pl.ds(d, d), v_ref, h2_ref, stage_ref, sems)

    h2_ref[...] = h_ref[...] + _dot(mixed_ref[...], wout_ref[...])
    q_ref[...] = _dot(_rmsnorm(h2_ref[...], nw_ref[...]).astype(BF16), wq_ref[...]).astype(BF16)
    scale = XA_HEADDIM ** -0.5
    for h in range(XA_HEADS):
        cols = slice(h * XA_HEADDIM, (h + 1) * XA_HEADDIM)
        logits = _dot_nt(q_ref[:, cols], k_ref[:, cols]) * scale
        p = jnp.exp(logits - jnp.max(logits, axis=-1, keepdims=True))
        p = p / jnp.sum(p, axis=-1, keepdims=True)
        att_ref[:, cols] = _dot(p.astype(BF16), v_ref[:, cols]).astype(BF16)
    o_ref[...] = h2_ref[...] + _dot(att_ref[...], wo_ref[...])


def _xattn(h, mixed, norm_w, mem, mem_norm_w, w_out, w_q, w_o, w_kv):
    m, d = h.shape
    tm = MEM_LEN
    once = pl.Buffered(1)
    resident = lambda w: pl.BlockSpec(w.shape, lambda i: (0, 0), pipeline_mode=once)
    return pl.pallas_call(
        _xattn_kernel,
        grid=(m // tm,),
        in_specs=[
            pl.BlockSpec((tm, d), lambda i: (i, 0)),
            pl.BlockSpec((tm, mixed.shape[1]), lambda i: (i, 0)),
            pl.BlockSpec((1, d), lambda i: (0, 0)),
            pl.BlockSpec((MEM_LEN, d), lambda i: (0, 0), pipeline_mode=once),
            pl.BlockSpec((1, d), lambda i: (0, 0)),
            resident(w_out), resident(w_q), resident(w_o),
            pl.BlockSpec(memory_space=pl.ANY),
        ],
        out_specs=pl.BlockSpec((tm, d), lambda i: (i, 0)),
        out_shape=jax.ShapeDtypeStruct((m, d), F32),
        scratch_shapes=[pltpu.VMEM((MEM_LEN, d), BF16),
                        pltpu.VMEM((MEM_LEN, d), BF16),
                        pltpu.VMEM((2, WEIGHT_STAGE_ROWS, d), F32),
                        pltpu.SemaphoreType.DMA((2,)),
                        pltpu.VMEM((tm, d), F32),
                        pltpu.VMEM((tm, d), BF16),
                        pltpu.VMEM((tm, d), BF16)],
        compiler_params=_params(1),
        name="xattn",
    )(h, mixed, norm_w.reshape(1, d), mem, mem_norm_w.reshape(1, d), w_out, w_q, w_o, w_kv)


def kernel(x, mem, ffn1_norm, ffn1_w_gu, ffn1_w_down, mix_norm, w_in, gm_v_norm, gm_w_s, gm_b_s, ssm_conv_w, ssm_conv_b, ssm_dt_bias, ssm_a_log, ssm_d, ssm_norm, w_out, xa_norm, mem_norm, xa_w_q, xa_w_kv, xa_w_o, ffn2_norm, ffn2_w_gu, ffn2_w_down, final_norm):
    pad_lanes = lambda a: jnp.pad(a, ((0, 0), (0, LANES - a.shape[1])))
    h = x[0]
    for i in range(ffn1_norm.shape[0]):
        h, xn = _ffn(h, ffn1_norm[i], ffn1_w_gu[i], ffn1_w_down[i], mix_norm[i], tail="next")

        proj, dt_raw, w_out_b, w_q_b, w_o_b = _in_proj(xn, jnp.swapaxes(w_in[i], 0, 1), ssm_conv_w[i], ssm_conv_b[i],
                                                       [w_out[i], xa_w_q[i], xa_w_o[i]])
        mixed = _mixer(proj, dt_raw, gm_v_norm[i], gm_w_s[i], pad_lanes(gm_b_s[i].T),
                       pad_lanes(ssm_dt_bias[i][None])[0], pad_lanes(ssm_a_log[i][None])[0],
                       jnp.repeat(ssm_d[i], SSM_HEADDIM), ssm_norm[i])
        h = _xattn(h, mixed, xa_norm[i], mem[0], mem_norm[i], w_out_b, w_q_b, w_o_b, xa_w_kv[i])

        last = i == ffn1_norm.shape[0] - 1
        h = _ffn(h, ffn2_norm[i], ffn2_w_gu[i], ffn2_w_down[i], final_norm, tail="final" if last else "none")[0]
    return h[None]
```

```python
import functools

import jax
import jax.numpy as jnp
from jax import lax
from jax.experimental import pallas as pl
from jax.experimental.pallas import tpu as pltpu

D_MODEL = 2048
MEM_LEN = 256
GM_WIDTH = 2048
GM_GROUPS = 4
GM_GROUP_WIDTH = GM_WIDTH // GM_GROUPS
CHUNK = 128
SSM_WIDTH = 2048
SSM_HEADDIM = 64
SSM_HEADS = SSM_WIDTH // SSM_HEADDIM
SSM_GROUPS = 8
SSM_HEADS_PER_GROUP = SSM_HEADS // SSM_GROUPS
SSM_GROUP_WIDTH = SSM_HEADS_PER_GROUP * SSM_HEADDIM
SSM_STATE = 128
SSM_CONV = 4
SSM_BC_WIDTH = SSM_GROUPS * SSM_STATE
PROJ_MAIN = 2 * GM_WIDTH + SSM_WIDTH + SSM_WIDTH + 2 * SSM_BC_WIDTH
XA_HEADS = 4
XA_HEADDIM = D_MODEL // XA_HEADS
EPS = 1e-6

LANES = 128
HALO_ROWS = 8
ROW_BLOCK = 128
IN_PROJ_ROW_BLOCK = 256
SIDE_CAST_STEPS = 64
VMEM_LIMIT = 60 * 1024 * 1024

BF16 = jnp.bfloat16
F32 = jnp.float32


def _params(n_axes):
    return pltpu.CompilerParams(dimension_semantics=("arbitrary",) * n_axes,
                                vmem_limit_bytes=VMEM_LIMIT)


def _rmsnorm(x, w):
    return x * lax.rsqrt(jnp.mean(x * x, axis=-1, keepdims=True) + EPS) * w


def _silu(x):
    half = 0.5 * x
    return half + half * jnp.tanh(half)


def _gelu_tanh(x):
    c = 0.7978845608028654
    return 0.5 * x * (1.0 + jnp.tanh(x * (c + (0.044715 * c) * (x * x))))


def _dot(a, b):
    return jnp.dot(a, b, preferred_element_type=F32)


def _dot_nt(a, b):
    return lax.dot_general(a, b, (((1,), (1,)), ((), ())), preferred_element_type=F32)


def _dot_tn(a, b):
    return lax.dot_general(a, b, (((0,), (0,)), ((), ())), preferred_element_type=F32)


def _ffn_kernel(x_ref, nw_ref, wg_ref, wu_ref, wd_ref, tw_ref, o_ref, *rest, tail):
    xn_ref = rest[-1]
    j = pl.program_id(1)
    n_row_blocks = x_ref.shape[0] // ROW_BLOCK

    def row_block(r):
        return pl.ds(pl.multiple_of(r * ROW_BLOCK, ROW_BLOCK), ROW_BLOCK)

    @pl.when(j == 0)
    def _():
        def body(r, _):
            rows = row_block(r)
            x = x_ref[rows, :]
            xn_ref[rows, :] = _rmsnorm(x, nw_ref[...]).astype(BF16)
            o_ref[rows, :] = x
        lax.fori_loop(0, n_row_blocks, body, None)

    xn = xn_ref[...]
    g = _dot(xn, wg_ref[...].astype(BF16))
    u = _dot(xn, wu_ref[...].astype(BF16))
    o_ref[...] += _dot((_silu(g) * u).astype(BF16), (0.5 * wd_ref[...]).astype(BF16))

    if tail != "none":
        @pl.when(j == pl.num_programs(1) - 1)
        def _():
            def body(r, _):
                rows = row_block(r)
                hn = _rmsnorm(o_ref[rows, :], tw_ref[...])
                if tail == "final":
                    o_ref[rows, :] = hn
                else:
                    rest[0][rows, :] = hn.astype(BF16)
            lax.fori_loop(0, n_row_blocks, body, None)


def _ffn(x, norm_w, w_gu, w_down, tail_w, *, tail, tm=1024, tf=256):
    m, d = x.shape
    f = w_down.shape[0]
    nf = f // tf
    row_tile = pl.BlockSpec((tm, d), lambda i, j: (i, 0))
    out_specs, out_shape = [row_tile], [jax.ShapeDtypeStruct((m, d), F32)]
    if tail == "next":
        out_specs.append(row_tile)
        out_shape.append(jax.ShapeDtypeStruct((m, d), BF16))
    return pl.pallas_call(
        functools.partial(_ffn_kernel, tail=tail),
        grid=(m // tm, nf),
        in_specs=[
            row_tile,
            pl.BlockSpec((1, d), lambda i, j: (0, 0)),
            pl.BlockSpec((d, tf), lambda i, j: (0, j)),
            pl.BlockSpec((d, tf), lambda i, j: (0, j + nf)),
            pl.BlockSpec((tf, d), lambda i, j: (j, 0)),
            pl.BlockSpec((1, d), lambda i, j: (0, 0)),
        ],
        out_specs=out_specs,
        out_shape=out_shape,
        scratch_shapes=[pltpu.VMEM((tm, d), BF16)],
        compiler_params=_params(2),
        name="ffn",
    )(x, norm_w.reshape(1, d), w_gu, w_gu, w_down, tail_w.reshape(1, d))


def _in_proj_kernel(xn_ref, wt_ref, wdt_ref, cw_ref, cb_ref, *rest, gelu_tiles, silu_tiles, n_side):
    side_in, (o_ref, dt_ref) = rest[:n_side], rest[n_side:n_side + 2]
    side_out = rest[n_side + 2:2 * n_side + 2]
    raw_even_ref, raw_odd_ref, halo_ref = rest[2 * n_side + 2:]
    for src_ref, dst_ref in zip(side_in, side_out):
        dst_ref[...] = src_ref[...].astype(BF16)

    i = pl.program_id(0)
    j = pl.program_id(1)
    conv_first = gelu_tiles + silu_tiles
    raw_refs = (raw_even_ref, raw_odd_ref)

    @pl.when(j == 0)
    def _():
        dt_ref[...] = _dot_nt(xn_ref[...], wdt_ref[...].astype(BF16))

    tm = o_ref.shape[0]

    def project(consume):
        w = wt_ref[...].astype(BF16)
        product = lambda r0: _dot_nt(xn_ref[r0:r0 + IN_PROJ_ROW_BLOCK, :], w)
        ahead = product(0)
        for r0 in range(0, tm, IN_PROJ_ROW_BLOCK):
            current = ahead
            if r0 + IN_PROJ_ROW_BLOCK < tm:
                ahead = product(r0 + IN_PROJ_ROW_BLOCK)
            consume(r0, current)

    def store_with(activation):
        def consume(r0, acc):
            o_ref[r0:r0 + IN_PROJ_ROW_BLOCK, :] = activation(acc)
        return consume

    @pl.when(j < gelu_tiles)
    def _():
        project(store_with(_gelu_tanh))

    @pl.when((j >= gelu_tiles) & (j < gelu_tiles + silu_tiles))
    def _():
        project(store_with(_silu))

    @pl.when(j >= conv_first)
    def _():
        c = j - conv_first
        tail = slice(IN_PROJ_ROW_BLOCK, IN_PROJ_ROW_BLOCK + HALO_ROWS)
        n_blocks = tm // IN_PROJ_ROW_BLOCK

        @pl.when(i == 0)
        def _():
            raw_refs[0][0:HALO_ROWS, :] = jnp.zeros((HALO_ROWS, o_ref.shape[1]), F32)

        @pl.when(i > 0)
        def _():
            raw_refs[0][0:HALO_ROWS, :] = halo_ref[c]

        def conv_silu(r0, acc):
            block = r0 // IN_PROJ_ROW_BLOCK
            raw, previous = raw_refs[block % 2], raw_refs[(block - 1) % 2]
            if block > 0:
                raw[0:HALO_ROWS, :] = previous[tail, :]
            raw[HALO_ROWS:, :] = acc
            out = cb_ref[...]
            for tap in range(SSM_CONV):
                off = HALO_ROWS - (SSM_CONV - 1) + tap
                out = out + raw[off:off + IN_PROJ_ROW_BLOCK, :] * cw_ref[tap:tap + 1, :]
            o_ref[r0:r0 + IN_PROJ_ROW_BLOCK, :] = _silu(out)

        project(conv_silu)
        halo_ref[c] = raw_refs[(n_blocks - 1) % 2][tail, :]


def _in_proj(xn, w_in_t, conv_w, conv_b, side_weights, *, tm=2048, tn=512):
    m, k = xn.shape
    n_col = PROJ_MAIN // tn
    gelu_tiles, silu_tiles = 2 * GM_WIDTH // tn, SSM_WIDTH // tn
    conv_tiles = (SSM_WIDTH + 2 * SSM_BC_WIDTH) // tn
    kernel_fn = functools.partial(_in_proj_kernel, gelu_tiles=gelu_tiles, silu_tiles=silu_tiles,
                                  n_side=len(side_weights))
    assert SIDE_CAST_STEPS <= (m // tm) * n_col
    side_specs = [pl.BlockSpec((w.shape[0] // SIDE_CAST_STEPS, w.shape[1]),
                               lambda i, j: (jnp.minimum(i * n_col + j, SIDE_CAST_STEPS - 1), 0))
                  for w in side_weights]
    conv_tile = lambda i, j: (0, jnp.maximum(j - (gelu_tiles + silu_tiles), 0))
    w_dt_t = jnp.pad(w_in_t[PROJ_MAIN:], ((0, LANES - (w_in_t.shape[0] - PROJ_MAIN)), (0, 0)))
    return pl.pallas_call(
        kernel_fn,
        grid=(m // tm, n_col),
        in_specs=[
            pl.BlockSpec((tm, k), lambda i, j: (i, 0)),
            pl.BlockSpec((tn, k), lambda i, j: (j, 0)),
            pl.BlockSpec((LANES, k), lambda i, j: (0, 0)),
            pl.BlockSpec((SSM_CONV, tn), conv_tile),
            pl.BlockSpec((1, tn), conv_tile),
            *side_specs,
        ],
        out_specs=[pl.BlockSpec((tm, tn), lambda i, j: (i, j)),
                   pl.BlockSpec((tm, LANES), lambda i, j: (i, 0)),
                   *side_specs],
        out_shape=[jax.ShapeDtypeStruct((m, PROJ_MAIN), F32), jax.ShapeDtypeStruct((m, LANES), F32),
                   *[jax.ShapeDtypeStruct(w.shape, BF16) for w in side_weights]],
        scratch_shapes=[pltpu.VMEM((HALO_ROWS + IN_PROJ_ROW_BLOCK, tn), F32),
                        pltpu.VMEM((HALO_ROWS + IN_PROJ_ROW_BLOCK, tn), F32),
                        pltpu.VMEM((conv_tiles, HALO_ROWS, tn), F32)],
        compiler_params=_params(2),
        name="in_proj",
    )(xn, w_in_t, w_dt_t, conv_w, conv_b.reshape(1, -1), *side_weights)


def _sgu_mix(gu_ref, gv_ref, vw_ref, ws_ref, bs_ref, o_ref):
    vn = _rmsnorm(gv_ref[...], vw_ref[...]).astype(BF16)
    t_idx = lax.broadcasted_iota(jnp.int32, (CHUNK, CHUNK), 0)
    s_idx = lax.broadcasted_iota(jnp.int32, (CHUNK, CHUNK), 1)
    causal = s_idx <= t_idx
    bias = bs_ref[...]
    for g in range(GM_GROUPS):
        cols = slice(g * GM_GROUP_WIDTH, (g + 1) * GM_GROUP_WIDTH)
        w = jnp.where(causal, ws_ref[g], 0.0).astype(BF16)
        mixed = _dot(w, vn[:, cols]) + bias[:, g:g + 1]
        o_ref[:, cols] = (gu_ref[:, cols] * mixed).astype(o_ref.dtype)


def _split3(v):
    hi = v.astype(BF16).astype(F32)
    rest = v - hi
    mid = rest.astype(BF16).astype(F32)
    return hi, mid, rest - mid


def _pack3(v):
    hi, mid, lo = _split3(v)
    lane = lax.broadcasted_iota(jnp.int32, v.shape, 1)
    packed = jnp.where(lane < SSM_HEADS, hi,
                       jnp.where(lane < 2 * SSM_HEADS, pltpu.roll(mid, SSM_HEADS, 1),
                                 jnp.where(lane < 3 * SSM_HEADS, pltpu.roll(lo, 2 * SSM_HEADS, 1), 0.0)))
    return packed.astype(BF16)


def _ssd_init(state_ref, expand_ref):
    state_ref[...] = jnp.zeros_like(state_ref)
    k_idx = lax.broadcasted_iota(jnp.int32, (LANES, SSM_WIDTH), 0)
    j_idx = lax.broadcasted_iota(jnp.int32, (LANES, SSM_WIDTH), 1)
    hit = (k_idx < 3 * SSM_HEADS) & (j_idx // SSM_HEADDIM == k_idx % SSM_HEADS)
    expand_ref[...] = hit.astype(F32).astype(BF16)


def _ssd_scan(gate_ref, xs_ref, bc_ref, dt_ref, dtb_ref, alog_ref, dskip_ref, nw_ref, o_ref,
              state_ref, y_ref, expand_ref):
    pre = dt_ref[...] + dtb_ref[...]
    dt = jnp.maximum(pre, 0.0) + jnp.log1p(jnp.exp(-jnp.abs(pre)))
    da = dt * (-jnp.exp(alog_ref[...]))
    row = lax.broadcasted_iota(jnp.int32, (CHUNK, CHUNK), 0)
    col = lax.broadcasted_iota(jnp.int32, (CHUNK, CHUNK), 1)
    causal = col <= row
    tri = causal.astype(F32).astype(BF16)
    da_hi, da_mid, da_lo = _split3(da)
    parts = _dot(tri, jnp.concatenate([da_hi, da_mid, da_lo], axis=1).astype(BF16))
    cum = parts[:, :LANES] + parts[:, LANES:2 * LANES] + parts[:, 2 * LANES:]
    cum_t = cum.T
    cum_last = cum[CHUNK - 1:CHUNK, :]

    packed = jnp.concatenate([_pack3(dt), _pack3(dt * jnp.exp(cum_last - cum)), _pack3(jnp.exp(cum))], axis=0)
    expanded = _dot(packed, expand_ref[...])
    dt_e = expanded[0:CHUNK]
    dt_decay_end_e = expanded[CHUNK:2 * CHUNK]
    exp_cum_e = expanded[2 * CHUNK:]
    chunk_decay_e = exp_cum_e[CHUNK - 1:CHUNK, :]

    xs = xs_ref[...]
    xdt = (xs * dt_e).astype(BF16)
    xdt_end = (xs * dt_decay_end_e).astype(BF16)
    lane_head = lax.broadcasted_iota(jnp.int32, (1, SSM_GROUP_WIDTH), 1) // SSM_HEADDIM

    b_of = lambda g: bc_ref[:, g * SSM_STATE:(g + 1) * SSM_STATE].astype(BF16)
    c_of = lambda g: bc_ref[:, SSM_BC_WIDTH + g * SSM_STATE:SSM_BC_WIDTH + (g + 1) * SSM_STATE].astype(BF16)
    for g in range(SSM_GROUPS):
        gcols = slice(g * SSM_GROUP_WIDTH, (g + 1) * SSM_GROUP_WIDTH)
        b_g, c_g = b_of(g), c_of(g)
        if g % 2 == 0:
            pair = _dot_nt(jnp.concatenate([c_g, c_of(g + 1)], axis=0), jnp.concatenate([b_g, b_of(g + 1)], axis=0))
        half = (g % 2) * CHUNK
        scores = pair[half:half + CHUNK, half:half + CHUNK]
        xdt_g = xdt[:, gcols]
        y = _dot(c_g, state_ref[g].astype(BF16)) * exp_cum_e[:, gcols]
        decayed, stacked = [], []
        for r in range(SSM_HEADS_PER_GROUP):
            h = g * SSM_HEADS_PER_GROUP + r
            seg = cum[:, h:h + 1] - cum_t[h:h + 1, :]
            decay = jnp.exp(jnp.where(causal, seg, -jnp.inf))
            decayed.append((scores * decay).astype(BF16))
            stacked.append(xdt_g * (lane_head == r).astype(F32).astype(BF16))
        y = y + _dot(jnp.concatenate(decayed, axis=1), jnp.concatenate(stacked, axis=0))
        state_ref[g] = state_ref[g] * chunk_decay_e[:, gcols] + _dot_tn(b_g, xdt_end[:, gcols])
        y_ref[:, gcols] = y

    y = y_ref[...] + dskip_ref[...] * xs
    y = y * gate_ref[...]
    o_ref[...] = _rmsnorm(y, nw_ref[...]).astype(o_ref.dtype)


def _mixer_kernel(gu_ref, gv_ref, vw_ref, ws_ref, bs_ref,
                  gate_ref, xs_ref, bc_ref, dt_ref, dtb_ref, alog_ref, dskip_ref, nw_ref,
                  o_ref,
                  state_ref, y_ref, expand_ref):
    @pl.when(pl.program_id(0) == 0)
    def _():
        _ssd_init(state_ref, expand_ref)

    _sgu_mix(gu_ref, gv_ref, vw_ref, ws_ref, bs_ref, o_ref.at[:, 0:GM_WIDTH])
    _ssd_scan(gate_ref, xs_ref, bc_ref, dt_ref, dtb_ref, alog_ref, dskip_ref, nw_ref, o_ref.at[:, GM_WIDTH:],
              state_ref, y_ref, expand_ref)


def _mixer(proj, dt_raw, v_norm_w, w_s, b_s_t, dt_bias, a_log, d_skip_e, norm_w):
    m = proj.shape[0]
    row = lambda a: a.reshape(1, -1)
    c0 = 2 * GM_WIDTH // SSM_WIDTH
    full = lambda shape: pl.BlockSpec(shape, lambda i: (0,) * len(shape))
    chunk = lambda width, col: pl.BlockSpec((CHUNK, width), lambda i: (i, col))
    return pl.pallas_call(
        _mixer_kernel,
        grid=(m // CHUNK,),
        in_specs=[
            chunk(GM_WIDTH, 0),
            chunk(GM_WIDTH, 1),
            full((1, GM_WIDTH)),
            full((GM_GROUPS, CHUNK, CHUNK)),
            full((CHUNK, LANES)),
            chunk(SSM_WIDTH, c0),
            chunk(SSM_WIDTH, c0 + 1),
            chunk(2 * SSM_BC_WIDTH, c0 + 2),
            chunk(LANES, 0),
            full((1, LANES)),
            full((1, LANES)),
            full((1, SSM_WIDTH)),
            full((1, SSM_WIDTH)),
        ],
        out_specs=chunk(GM_WIDTH + SSM_WIDTH, 0),
        out_shape=jax.ShapeDtypeStruct((m, GM_WIDTH + SSM_WIDTH), BF16),
        scratch_shapes=[pltpu.VMEM((SSM_GROUPS, SSM_STATE, SSM_GROUP_WIDTH), F32),
                        pltpu.VMEM((CHUNK, SSM_WIDTH), F32),
                        pltpu.VMEM((LANES, SSM_WIDTH), BF16)],
        compiler_params=_params(1),
        name="mixer",
    )(proj, proj, row(v_norm_w), w_s, b_s_t,
      proj, proj, proj, dt_raw, row(dt_bias), row(a_log), row(d_skip_e), row(norm_w))


WEIGHT_STAGE_ROWS = 256


def _for_each_row_block(w_hbm, cols, stage_ref, sems, use):
    n_blocks = w_hbm.shape[0] // WEIGHT_STAGE_ROWS

    def block_rows(r):
        return pl.ds(pl.multiple_of(r * WEIGHT_STAGE_ROWS, WEIGHT_STAGE_ROWS), WEIGHT_STAGE_ROWS)

    def copy(r, slot):
        return pltpu.make_async_copy(w_hbm.at[block_rows(r), cols], stage_ref.at[slot], sems.at[slot])

    copy(0, 0).start()

    def body(r, _):
        slot = r % 2

        @pl.when(r + 1 < n_blocks)
        def _():
            copy(r + 1, 1 - slot).start()

        copy(r, slot).wait()
        use(block_rows(r), stage_ref.at[slot])

    lax.fori_loop(0, n_blocks, body, None)


def _project_memory(memn_ref, w_hbm, cols, out_ref, acc_ref, stage_ref, sems):
    acc_ref[...] = jnp.zeros_like(acc_ref)

    def use(rows, block_ref):
        acc_ref[...] += _dot(memn_ref[:, rows], block_ref[...].astype(BF16))

    _for_each_row_block(w_hbm, cols, stage_ref, sems, use)
    out_ref[...] = acc_ref[...].astype(BF16)


def _xattn_kernel(h_ref, mixed_ref, nw_ref, mem_ref, memw_ref, wout_ref, wq_ref, wo_ref, wkv_hbm,
                  o_ref,
                  k_ref, v_ref, stage_ref, sems, h2_ref, q_ref, att_ref):
    d = h_ref.shape[1]

    @pl.when(pl.program_id(0) == 0)
    def _():
        q_ref[...] = _rmsnorm(mem_ref[...], memw_ref[...]).astype(BF16)
        _project_memory(q_ref, wkv_hbm, pl.ds(0, d), k_ref, h2_ref, stage_ref, sems)
        _project_memory(q_ref, wkv_hbm, pl.ds(d, d), v_ref, h2_ref, stage_ref, sems)

    h2_ref[...] = h_ref[...] + _dot(mixed_ref[...], wout_ref[...])
    q_ref[...] = _dot(_rmsnorm(h2_ref[...], nw_ref[...]).astype(BF16), wq_ref[...]).astype(BF16)
    scale = XA_HEADDIM ** -0.5
    for h in range(XA_HEADS):
        cols = slice(h * XA_HEADDIM, (h + 1) * XA_HEADDIM)
        logits = _dot_nt(q_ref[:, cols], k_ref[:, cols]) * scale
        p = jnp.exp(logits - jnp.max(logits, axis=-1, keepdims=True))
        p = p / jnp.sum(p, axis=-1, keepdims=True)
        att_ref[:, cols] = _dot(p.astype(BF16), v_ref[:, cols]).astype(BF16)
    o_ref[...] = h2_ref[...] + _dot(att_ref[...], wo_ref[...])


def _xattn(h, mixed, norm_w, mem, mem_norm_w, w_out, w_q, w_o, w_kv):
    m, d = h.shape
    tm = MEM_LEN
    once = pl.Buffered(1)
    resident = lambda w: pl.BlockSpec(w.shape, lambda i: (0, 0), pipeline_mode=once)
    return pl.pallas_call(
        _xattn_kernel,
        grid=(m // tm,),
        in_specs=[
            pl.BlockSpec((tm, d), lambda i: (i, 0)),
            pl.BlockSpec((tm, mixed.shape[1]), lambda i: (i, 0)),
            pl.BlockSpec((1, d), lambda i: (0, 0)),
            pl.BlockSpec((MEM_LEN, d), lambda i: (0, 0), pipeline_mode=once),
            pl.BlockSpec((1, d), lambda i: (0, 0)),
            resident(w_out), resident(w_q), resident(w_o),
            pl.BlockSpec(memory_space=pl.ANY),
        ],
        out_specs=pl.BlockSpec((tm, d), lambda i: (i, 0)),
        out_shape=jax.ShapeDtypeStruct((m, d), F32),
        scratch_shapes=[pltpu.VMEM((MEM_LEN, d), BF16),
                        pltpu.VMEM((MEM_LEN, d), BF16),
                        pltpu.VMEM((2, WEIGHT_STAGE_ROWS, d), F32),
                        pltpu.SemaphoreType.DMA((2,)),
                        pltpu.VMEM((tm, d), F32),
                        pltpu.VMEM((tm, d), BF16),
                        pltpu.VMEM((tm, d), BF16)],
        compiler_params=_params(1),
        name="xattn",
    )(h, mixed, norm_w.reshape(1, d), mem, mem_norm_w.reshape(1, d), w_out, w_q, w_o, w_kv)


def kernel(x, mem, ffn1_norm, ffn1_w_gu, ffn1_w_down, mix_norm, w_in, gm_v_norm, gm_w_s, gm_b_s, ssm_conv_w, ssm_conv_b, ssm_dt_bias, ssm_a_log, ssm_d, ssm_norm, w_out, xa_norm, mem_norm, xa_w_q, xa_w_kv, xa_w_o, ffn2_norm, ffn2_w_gu, ffn2_w_down, final_norm):
    pad_lanes = lambda a: jnp.pad(a, ((0, 0), (0, LANES - a.shape[1])))
    h = x[0]
    for i in range(ffn1_norm.shape[0]):
        h, xn = _ffn(h, ffn1_norm[i], ffn1_w_gu[i], ffn1_w_down[i], mix_norm[i], tail="next")

        proj, dt_raw, w_out_b, w_q_b, w_o_b = _in_proj(xn, jnp.swapaxes(w_in[i], 0, 1), ssm_conv_w[i], ssm_conv_b[i],
                                                       [w_out[i], xa_w_q[i], xa_w_o[i]])
        mixed = _mixer(proj, dt_raw, gm_v_norm[i], gm_w_s[i], pad_lanes(gm_b_s[i].T),
                       pad_lanes(ssm_dt_bias[i][None])[0], pad_lanes(ssm_a_log[i][None])[0],
                       jnp.repeat(ssm_d[i], SSM_HEADDIM), ssm_norm[i])
        h = _xattn(h, mixed, xa_norm[i], mem[0], mem_norm[i], w_out_b, w_q_b, w_o_b, xa_w_kv[i])

        last = i == ffn1_norm.shape[0] - 1
        h = _ffn(h, ffn2_norm[i], ffn2_w_gu[i], ffn2_w_down[i], final_norm, tail="final" if last else "none")[0]
    return h[None]
```

```python
import functools

import jax
import jax.numpy as jnp
from jax import lax
from jax.experimental import pallas as pl
from jax.experimental.pallas import tpu as pltpu

D_MODEL = 2048
MEM_LEN = 256
GM_WIDTH = 2048
GM_GROUPS = 4
GM_GROUP_WIDTH = GM_WIDTH // GM_GROUPS
CHUNK = 128
SSM_WIDTH = 2048
SSM_HEADDIM = 64
SSM_HEADS = SSM_WIDTH // SSM_HEADDIM
SSM_GROUPS = 8
SSM_HEADS_PER_GROUP = SSM_HEADS // SSM_GROUPS
SSM_GROUP_WIDTH = SSM_HEADS_PER_GROUP * SSM_HEADDIM
SSM_STATE = 128
SSM_CONV = 4
SSM_BC_WIDTH = SSM_GROUPS * SSM_STATE
PROJ_MAIN = 2 * GM_WIDTH + SSM_WIDTH + SSM_WIDTH + 2 * SSM_BC_WIDTH
XA_HEADS = 4
XA_HEADDIM = D_MODEL // XA_HEADS
EPS = 1e-6

LANES = 128
HALO_ROWS = 8
ROW_BLOCK = 128
IN_PROJ_ROW_BLOCK = 256
SIDE_CAST_STEPS = 64
VMEM_LIMIT = 60 * 1024 * 1024

BF16 = jnp.bfloat16
F32 = jnp.float32


def _params(n_axes):
    return pltpu.CompilerParams(dimension_semantics=("arbitrary",) * n_axes,
                                vmem_limit_bytes=VMEM_LIMIT)


def _rmsnorm(x, w):
    return x * lax.rsqrt(jnp.mean(x * x, axis=-1, keepdims=True) + EPS) * w


def _silu(x):
    half = 0.5 * x
    return half + half * jnp.tanh(half)


def _gelu_tanh(x):
    c = 0.7978845608028654
    return 0.5 * x * (1.0 + jnp.tanh(x * (c + (0.044715 * c) * (x * x))))


def _dot(a, b):
    return jnp.dot(a, b, preferred_element_type=F32)


def _dot_nt(a, b):
    return lax.dot_general(a, b, (((1,), (1,)), ((), ())), preferred_element_type=F32)


def _dot_tn(a, b):
    return lax.dot_general(a, b, (((0,), (0,)), ((), ())), preferred_element_type=F32)


def _ffn_kernel(x_ref, nw_ref, wg_ref, wu_ref, wd_ref, tw_ref, o_ref, *rest, tail):
    xn_ref = rest[-1]
    j = pl.program_id(1)
    n_row_blocks = x_ref.shape[0] // ROW_BLOCK

    def row_block(r):
        return pl.ds(pl.multiple_of(r * ROW_BLOCK, ROW_BLOCK), ROW_BLOCK)

    @pl.when(j == 0)
    def _():
        def body(r, _):
            rows = row_block(r)
            x = x_ref[rows, :]
            xn_ref[rows, :] = _rmsnorm(x, nw_ref[...]).astype(BF16)
            o_ref[rows, :] = x
        lax.fori_loop(0, n_row_blocks, body, None)

    xn = xn_ref[...]
    g = _dot(xn, wg_ref[...].astype(BF16))
    u = _dot(xn, wu_ref[...].astype(BF16))
    o_ref[...] += _dot((_silu(g) * u).astype(BF16), (0.5 * wd_ref[...]).astype(BF16))

    if tail != "none":
        @pl.when(j == pl.num_programs(1) - 1)
        def _():
            def body(r, _):
                rows = row_block(r)
                hn = _rmsnorm(o_ref[rows, :], tw_ref[...])
                if tail == "final":
                    o_ref[rows, :] = hn
                else:
                    rest[0][rows, :] = hn.astype(BF16)
            lax.fori_loop(0, n_row_blocks, body, None)


def _ffn(x, norm_w, w_gu, w_down, tail_w, *, tail, tm=1024, tf=256):
    m, d = x.shape
    f = w_down.shape[0]
    nf = f // tf
    row_tile = pl.BlockSpec((tm, d), lambda i, j: (i, 0))
    out_specs, out_shape = [row_tile], [jax.ShapeDtypeStruct((m, d), F32)]
    if tail == "next":
        out_specs.append(row_tile)
        out_shape.append(jax.ShapeDtypeStruct((m, d), BF16))
    return pl.pallas_call(
        functools.partial(_ffn_kernel, tail=tail),
        grid=(m // tm, nf),
        in_specs=[
            row_tile,
            pl.BlockSpec((1, d), lambda i, j: (0, 0)),
            pl.BlockSpec((d, tf), lambda i, j: (0, j)),
            pl.BlockSpec((d, tf), lambda i, j: (0, j + nf)),
            pl.BlockSpec((tf, d), lambda i, j: (j, 0)),
            pl.BlockSpec((1, d), lambda i, j: (0, 0)),
        ],
        out_specs=out_specs,
        out_shape=out_shape,
        scratch_shapes=[pltpu.VMEM((tm, d), BF16)],
        compiler_params=_params(2),
        name="ffn",
    )(x, norm_w.reshape(1, d), w_gu, w_gu, w_down, tail_w.reshape(1, d))


def _in_proj_kernel(xn_ref, wt_ref, wdt_ref, cw_ref, cb_ref, *rest, gelu_tiles, silu_tiles, n_side):
    side_in, (o_ref, dt_ref) = rest[:n_side], rest[n_side:n_side + 2]
    side_out = rest[n_side + 2:2 * n_side + 2]
    raw_even_ref, raw_odd_ref, halo_ref = rest[2 * n_side + 2:]
    for src_ref, dst_ref in zip(side_in, side_out):
        dst_ref[...] = src_ref[...].astype(BF16)

    i = pl.program_id(0)
    j = pl.program_id(1)
    conv_first = gelu_tiles + silu_tiles
    raw_refs = (raw_even_ref, raw_odd_ref)

    @pl.when(j == 0)
    def _():
        dt_ref[...] = _dot_nt(xn_ref[...], wdt_ref[...].astype(BF16))

    tm = o_ref.shape[0]

    def project(consume):
        w = wt_ref[...].astype(BF16)
        for r0 in range(0, tm, IN_PROJ_ROW_BLOCK):
            consume(r0, _dot_nt(xn_ref[r0:r0 + IN_PROJ_ROW_BLOCK, :], w))

    def store_with(activation):
        def consume(r0, acc):
            o_ref[r0:r0 + IN_PROJ_ROW_BLOCK, :] = activation(acc)
        return consume

    @pl.when(j < gelu_tiles)
    def _():
        project(store_with(_gelu_tanh))

    @pl.when((j >= gelu_tiles) & (j < gelu_tiles + silu_tiles))
    def _():
        project(store_with(_silu))

    @pl.when(j >= conv_first)
    def _():
        c = j - conv_first
        tail = slice(IN_PROJ_ROW_BLOCK, IN_PROJ_ROW_BLOCK + HALO_ROWS)
        n_blocks = tm // IN_PROJ_ROW_BLOCK

        @pl.when(i == 0)
        def _():
            raw_refs[0][0:HALO_ROWS, :] = jnp.zeros((HALO_ROWS, o_ref.shape[1]), F32)

        @pl.when(i > 0)
        def _():
            raw_refs[0][0:HALO_ROWS, :] = halo_ref[c]

        def conv_silu(r0, acc):
            block = r0 // IN_PROJ_ROW_BLOCK
            raw, previous = raw_refs[block % 2], raw_refs[(block - 1) % 2]
            if block > 0:
                raw[0:HALO_ROWS, :] = previous[tail, :]
            raw[HALO_ROWS:, :] = acc
            out = cb_ref[...]
            for tap in range(SSM_CONV):
                off = HALO_ROWS - (SSM_CONV - 1) + tap
                out = out + raw[off:off + IN_PROJ_ROW_BLOCK, :] * cw_ref[tap:tap + 1, :]
            o_ref[r0:r0 + IN_PROJ_ROW_BLOCK, :] = _silu(out)

        project(conv_silu)
        halo_ref[c] = raw_refs[(n_blocks - 1) % 2][tail, :]


def _in_proj(xn, w_in_t, conv_w, conv_b, side_weights, *, tm=2048, tn=512):
    m, k = xn.shape
    n_col = PROJ_MAIN // tn
    gelu_tiles, silu_tiles = 2 * GM_WIDTH // tn, SSM_WIDTH // tn
    conv_tiles = (SSM_WIDTH + 2 * SSM_BC_WIDTH) // tn
    kernel_fn = functools.partial(_in_proj_kernel, gelu_tiles=gelu_tiles, silu_tiles=silu_tiles,
                                  n_side=len(side_weights))
    assert SIDE_CAST_STEPS <= (m // tm) * n_col
    side_specs = [pl.BlockSpec((w.shape[0] // SIDE_CAST_STEPS, w.shape[1]),
                               lambda i, j: (jnp.minimum(i * n_col + j, SIDE_CAST_STEPS - 1), 0))
                  for w in side_weights]
    conv_tile = lambda i, j: (0, jnp.maximum(j - (gelu_tiles + silu_tiles), 0))
    w_dt_t = jnp.pad(w_in_t[PROJ_MAIN:], ((0, LANES - (w_in_t.shape[0] - PROJ_MAIN)), (0, 0)))
    return pl.pallas_call(
        kernel_fn,
        grid=(m // tm, n_col),
        in_specs=[
            pl.BlockSpec((tm, k), lambda i, j: (i, 0)),
            pl.BlockSpec((tn, k), lambda i, j: (j, 0)),
            pl.BlockSpec((LANES, k), lambda i, j: (0, 0)),
            pl.BlockSpec((SSM_CONV, tn), conv_tile),
            pl.BlockSpec((1, tn), conv_tile),
            *side_specs,
        ],
        out_specs=[pl.BlockSpec((tm, tn), lambda i, j: (i, j)),
                   pl.BlockSpec((tm, LANES), lambda i, j: (i, 0)),
                   *side_specs],
        out_shape=[jax.ShapeDtypeStruct((m, PROJ_MAIN), F32), jax.ShapeDtypeStruct((m, LANES), F32),
                   *[jax.ShapeDtypeStruct(w.shape, BF16) for w in side_weights]],
        scratch_shapes=[pltpu.VMEM((HALO_ROWS + IN_PROJ_ROW_BLOCK, tn), F32),
                        pltpu.VMEM((HALO_ROWS + IN_PROJ_ROW_BLOCK, tn), F32),
                        pltpu.VMEM((conv_tiles, HALO_ROWS, tn), F32)],
        compiler_params=_params(2),
        name="in_proj",
    )(xn, w_in_t, w_dt_t, conv_w, conv_b.reshape(1, -1), *side_weights)


def _sgu_init(ws_ref, wsc_ref):
    t_idx = lax.broadcasted_iota(jnp.int32, (CHUNK, CHUNK), 0)
    s_idx = lax.broadcasted_iota(jnp.int32, (CHUNK, CHUNK), 1)
    for g in range(GM_GROUPS):
        wsc_ref[g] = jnp.where(s_idx <= t_idx, ws_ref[g], 0.0).astype(BF16)


def _sgu_mix(gu_ref, gv_ref, vw_ref, wsc_ref, bs_ref, o_ref):
    vn = _rmsnorm(gv_ref[...], vw_ref[...]).astype(BF16)
    bias = bs_ref[...]
    for g in range(GM_GROUPS):
        cols = slice(g * GM_GROUP_WIDTH, (g + 1) * GM_GROUP_WIDTH)
        mixed = _dot(wsc_ref[g], vn[:, cols]) + bias[:, g:g + 1]
        o_ref[:, cols] = (gu_ref[:, cols] * mixed).astype(o_ref.dtype)


def _split3(v):
    hi = v.astype(BF16).astype(F32)
    rest = v - hi
    mid = rest.astype(BF16).astype(F32)
    return hi, mid, rest - mid


def _pack3(v):
    hi, mid, lo = _split3(v)
    lane = lax.broadcasted_iota(jnp.int32, v.shape, 1)
    packed = jnp.where(lane < SSM_HEADS, hi,
                       jnp.where(lane < 2 * SSM_HEADS, pltpu.roll(mid, SSM_HEADS, 1),
                                 jnp.where(lane < 3 * SSM_HEADS, pltpu.roll(lo, 2 * SSM_HEADS, 1), 0.0)))
    return packed.astype(BF16)


def _ssd_init(state_ref, expand_ref):
    state_ref[...] = jnp.zeros_like(state_ref)
    k_idx = lax.broadcasted_iota(jnp.int32, (LANES, SSM_WIDTH), 0)
    j_idx = lax.broadcasted_iota(jnp.int32, (LANES, SSM_WIDTH), 1)
    hit = (k_idx < 3 * SSM_HEADS) & (j_idx // SSM_HEADDIM == k_idx % SSM_HEADS)
    expand_ref[...] = hit.astype(F32).astype(BF16)


def _ssd_scan(gate_ref, xs_ref, bc_ref, dt_ref, dtb_ref, alog_ref, dskip_ref, nw_ref, o_ref,
              state_ref, y_ref, expand_ref):
    pre = dt_ref[...] + dtb_ref[...]
    dt = jnp.maximum(pre, 0.0) + jnp.log1p(jnp.exp(-jnp.abs(pre)))
    da = dt * (-jnp.exp(alog_ref[...]))
    row = lax.broadcasted_iota(jnp.int32, (CHUNK, CHUNK), 0)
    col = lax.broadcasted_iota(jnp.int32, (CHUNK, CHUNK), 1)
    causal = col <= row
    tri = causal.astype(F32).astype(BF16)
    da_hi, da_mid, da_lo = _split3(da)
    parts = _dot(tri, jnp.concatenate([da_hi, da_mid, da_lo], axis=1).astype(BF16))
    cum = parts[:, :LANES] + parts[:, LANES:2 * LANES] + parts[:, 2 * LANES:]
    cum_t = cum.T
    cum_last = cum[CHUNK - 1:CHUNK, :]

    packed = jnp.concatenate([_pack3(dt), _pack3(dt * jnp.exp(cum_last - cum)), _pack3(jnp.exp(cum))], axis=0)
    expanded = _dot(packed, expand_ref[...])
    dt_e = expanded[0:CHUNK]
    dt_decay_end_e = expanded[CHUNK:2 * CHUNK]
    exp_cum_e = expanded[2 * CHUNK:]
    chunk_decay_e = exp_cum_e[CHUNK - 1:CHUNK, :]

    xs = xs_ref[...]
    xdt = (xs * dt_e).astype(BF16)
    xdt_end = (xs * dt_decay_end_e).astype(BF16)
    lane_head = lax.broadcasted_iota(jnp.int32, (1, SSM_GROUP_WIDTH), 1) // SSM_HEADDIM
    head_lanes = [(lane_head == r).astype(F32).astype(BF16) for r in range(SSM_HEADS_PER_GROUP)]

    b_of = lambda g: bc_ref[:, g * SSM_STATE:(g + 1) * SSM_STATE].astype(BF16)
    c_of = lambda g: bc_ref[:, SSM_BC_WIDTH + g * SSM_STATE:SSM_BC_WIDTH + (g + 1) * SSM_STATE].astype(BF16)
    for g in range(SSM_GROUPS):
        gcols = slice(g * SSM_GROUP_WIDTH, (g + 1) * SSM_GROUP_WIDTH)
        b_g, c_g = b_of(g), c_of(g)
        if g % 2 == 0:
            pair = _dot_nt(jnp.concatenate([c_g, c_of(g + 1)], axis=0), jnp.concatenate([b_g, b_of(g + 1)], axis=0))
        half = (g % 2) * CHUNK
        scores = pair[half:half + CHUNK, half:half + CHUNK]
        xdt_g = xdt[:, gcols]
        y = _dot(c_g, state_ref[g].astype(BF16)) * exp_cum_e[:, gcols]
        decayed, stacked = [], []
        for r in range(SSM_HEADS_PER_GROUP):
            h = g * SSM_HEADS_PER_GROUP + r
            seg = cum[:, h:h + 1] - cum_t[h:h + 1, :]
            decay = jnp.exp(jnp.where(causal, seg, -jnp.inf))
            decayed.append((scores * decay).astype(BF16))
            stacked.append(xdt_g * head_lanes[r])
        y = y + _dot(jnp.concatenate(decayed, axis=1), jnp.concatenate(stacked, axis=0))
        state_ref[g] = state_ref[g] * chunk_decay_e[:, gcols] + _dot_tn(b_g, xdt_end[:, gcols])
        y_ref[:, gcols] = y

    y = y_ref[...] + dskip_ref[...] * xs
    y = y * gate_ref[...]
    o_ref[...] = _rmsnorm(y, nw_ref[...]).astype(o_ref.dtype)


def _mixer_kernel(gu_ref, gv_ref, vw_ref, ws_ref, bs_ref,
                  gate_ref, xs_ref, bc_ref, dt_ref, dtb_ref, alog_ref, dskip_ref, nw_ref,
                  o_ref,
                  state_ref, y_ref, expand_ref, wsc_ref):
    @pl.when(pl.program_id(0) == 0)
    def _():
        _ssd_init(state_ref, expand_ref)
        _sgu_init(ws_ref, wsc_ref)

    _sgu_mix(gu_ref, gv_ref, vw_ref, wsc_ref, bs_ref, o_ref.at[:, 0:GM_WIDTH])
    _ssd_scan(gate_ref, xs_ref, bc_ref, dt_ref, dtb_ref, alog_ref, dskip_ref, nw_ref, o_ref.at[:, GM_WIDTH:],
              state_ref, y_ref, expand_ref)


def _mixer(proj, dt_raw, v_norm_w, w_s, b_s_t, dt_bias, a_log, d_skip_e, norm_w):
    m = proj.shape[0]
    row = lambda a: a.reshape(1, -1)
    c0 = 2 * GM_WIDTH // SSM_WIDTH
    full = lambda shape: pl.BlockSpec(shape, lambda i: (0,) * len(shape))
    chunk = lambda width, col: pl.BlockSpec((CHUNK, width), lambda i: (i, col))
    return pl.pallas_call(
        _mixer_kernel,
        grid=(m // CHUNK,),
        in_specs=[
            chunk(GM_WIDTH, 0),
            chunk(GM_WIDTH, 1),
            full((1, GM_WIDTH)),
            full((GM_GROUPS, CHUNK, CHUNK)),
            full((CHUNK, LANES)),
            chunk(SSM_WIDTH, c0),
            chunk(SSM_WIDTH, c0 + 1),
            chunk(2 * SSM_BC_WIDTH, c0 + 2),
            chunk(LANES, 0),
            full((1, LANES)),
            full((1, LANES)),
            full((1, SSM_WIDTH)),
            full((1, SSM_WIDTH)),
        ],
        out_specs=chunk(GM_WIDTH + SSM_WIDTH, 0),
        out_shape=jax.ShapeDtypeStruct((m, GM_WIDTH + SSM_WIDTH), BF16),
        scratch_shapes=[pltpu.VMEM((SSM_GROUPS, SSM_STATE, SSM_GROUP_WIDTH), F32),
                        pltpu.VMEM((CHUNK, SSM_WIDTH), F32),
                        pltpu.VMEM((LANES, SSM_WIDTH), BF16),
                        pltpu.VMEM((GM_GROUPS, CHUNK, CHUNK), BF16)],
        compiler_params=_params(1),
        name="mixer",
    )(proj, proj, row(v_norm_w), w_s, b_s_t,
      proj, proj, proj, dt_raw, row(dt_bias), row(a_log), row(d_skip_e), row(norm_w))


WEIGHT_STAGE_ROWS = 256


def _for_each_row_block(w_hbm, cols, stage_ref, sems, use):
    n_blocks = w_hbm.shape[0] // WEIGHT_STAGE_ROWS

    def block_rows(r):
        return pl.ds(pl.multiple_of(r * WEIGHT_STAGE_ROWS, WEIGHT_STAGE_ROWS), WEIGHT_STAGE_ROWS)

    def copy(r, slot):
        return pltpu.make_async_copy(w_hbm.at[block_rows(r), cols], stage_ref.at[slot], sems.at[slot])

    copy(0, 0).start()

    def body(r, _):
        slot = r % 2

        @pl.when(r + 1 < n_blocks)
        def _():
            copy(r + 1, 1 - slot).start()

        copy(r, slot).wait()
        use(block_rows(r), stage_ref.at[slot])

    lax.fori_loop(0, n_blocks, body, None)


def _project_memory(memn_ref, w_hbm, cols, out_ref, acc_ref, stage_ref, sems):
    acc_ref[...] = jnp.zeros_like(acc_ref)

    def use(rows, block_ref):
        acc_ref[...] += _dot(memn_ref[:, rows], block_ref[...].astype(BF16))

    _for_each_row_block(w_hbm, cols, stage_ref, sems, use)
    out_ref[...] = acc_ref[...].astype(BF16)


def _xattn_kernel(h_ref, mixed_ref, nw_ref, mem_ref, memw_ref, wout_ref, wq_ref, wo_ref, wkv_hbm,
                  o_ref,
                  k_ref, v_ref, stage_ref, sems, h2_ref, q_ref, att_ref):
    d = h_ref.shape[1]

    @pl.when(pl.program_id(0) == 0)
    def _():
        q_ref[...] = _rmsnorm(mem_ref[...], memw_ref[...]).astype(BF16)
        _project_memory(q_ref, wkv_hbm, pl.ds(0, d), k_ref, h2_ref, stage_ref, sems)
        _project_memory(q_ref, wkv_hbm, pl.ds(d, d), v_ref, h2_ref, stage_ref, sems)

    h2_ref[...] = h_ref[...] + _dot(mixed_ref[...], wout_ref[...])
    q_ref[...] = _dot(_rmsnorm(h2_ref[...], nw_ref[...]).astype(BF16), wq_ref[...]).astype(BF16)
    scale = XA_HEADDIM ** -0.5
    for h in range(XA_HEADS):
        cols = slice(h * XA_HEADDIM, (h + 1) * XA_HEADDIM)
        logits = _dot_nt(q_ref[:, cols], k_ref[:, cols]) * scale
        p = jnp.exp(logits - jnp.max(logits, axis=-1, keepdims=True))
        p = p / jnp.sum(p, axis=-1, keepdims=True)
        att_ref[:, cols] = _dot(p.astype(BF16), v_ref[:, cols]).astype(BF16)
    o_ref[...] = h2_ref[...] + _dot(att_ref[...], wo_ref[...])


def _xattn(h, mixed, norm_w, mem, mem_norm_w, w_out, w_q, w_o, w_kv):
    m, d = h.shape
    tm = MEM_LEN
    once = pl.Buffered(1)
    resident = lambda w: pl.BlockSpec(w.shape, lambda i: (0, 0), pipeline_mode=once)
    return pl.pallas_call(
        _xattn_kernel,
        grid=(m // tm,),
        in_specs=[
            pl.BlockSpec((tm, d), lambda i: (i, 0)),
            pl.BlockSpec((tm, mixed.shape[1]), lambda i: (i, 0)),
            pl.BlockSpec((1, d), lambda i: (0, 0)),
            pl.BlockSpec((MEM_LEN, d), lambda i: (0, 0), pipeline_mode=once),
            pl.BlockSpec((1, d), lambda i: (0, 0)),
            resident(w_out), resident(w_q), resident(w_o),
            pl.BlockSpec(memory_space=pl.ANY),
        ],
        out_specs=pl.BlockSpec((tm, d), lambda i: (i, 0)),
        out_shape=jax.ShapeDtypeStruct((m, d), F32),
        scratch_shapes=[pltpu.VMEM((MEM_LEN, d), BF16),
                        pltpu.VMEM((MEM_LEN, d), BF16),
                        pltpu.VMEM((2, WEIGHT_STAGE_ROWS, d), F32),
                        pltpu.SemaphoreType.DMA((2,)),
                        pltpu.VMEM((tm, d), F32),
                        pltpu.VMEM((tm, d), BF16),
                        pltpu.VMEM((tm, d), BF16)],
        compiler_params=_params(1),
        name="xattn",
    )(h, mixed, norm_w.reshape(1, d), mem, mem_norm_w.reshape(1, d), w_out, w_q, w_o, w_kv)


def kernel(x, mem, ffn1_norm, ffn1_w_gu, ffn1_w_down, mix_norm, w_in, gm_v_norm, gm_w_s, gm_b_s, ssm_conv_w, ssm_conv_b, ssm_dt_bias, ssm_a_log, ssm_d, ssm_norm, w_out, xa_norm, mem_norm, xa_w_q, xa_w_kv, xa_w_o, ffn2_norm, ffn2_w_gu, ffn2_w_down, final_norm):
    pad_lanes = lambda a: jnp.pad(a, ((0, 0), (0, LANES - a.shape[1])))
    h = x[0]
    for i in range(ffn1_norm.shape[0]):
        h, xn = _ffn(h, ffn1_norm[i], ffn1_w_gu[i], ffn1_w_down[i], mix_norm[i], tail="next")

        proj, dt_raw, w_out_b, w_q_b, w_o_b = _in_proj(xn, jnp.swapaxes(w_in[i], 0, 1), ssm_conv_w[i], ssm_conv_b[i],
                                                       [w_out[i], xa_w_q[i], xa_w_o[i]])
        mixed = _mixer(proj, dt_raw, gm_v_norm[i], gm_w_s[i], pad_lanes(gm_b_s[i].T),
                       pad_lanes(ssm_dt_bias[i][None])[0], pad_lanes(ssm_a_log[i][None])[0],
                       jnp.repeat(ssm_d[i], SSM_HEADDIM), ssm_norm[i])
        h = _xattn(h, mixed, xa_norm[i], mem[0], mem_norm[i], w_out_b, w_q_b, w_o_b, xa_w_kv[i])

        last = i == ffn1_norm.shape[0] - 1
        h = _ffn(h, ffn2_norm[i], ffn2_w_gu[i], ffn2_w_down[i], final_norm, tail="final" if last else "none")[0]
    return h[None]
```

```python
import functools

import jax
import jax.numpy as jnp
from jax import lax
from jax.experimental import pallas as pl
from jax.experimental.pallas import tpu as pltpu

D_MODEL = 2048
MEM_LEN = 256
GM_WIDTH = 2048
GM_GROUPS = 4
GM_GROUP_WIDTH = GM_WIDTH // GM_GROUPS
CHUNK = 128
SSM_WIDTH = 2048
SSM_HEADDIM = 64
SSM_HEADS = SSM_WIDTH // SSM_HEADDIM
SSM_GROUPS = 8
SSM_HEADS_PER_GROUP = SSM_HEADS // SSM_GROUPS
SSM_GROUP_WIDTH = SSM_HEADS_PER_GROUP * SSM_HEADDIM
SSM_STATE = 128
SSM_CONV = 4
SSM_BC_WIDTH = SSM_GROUPS * SSM_STATE
PROJ_MAIN = 2 * GM_WIDTH + SSM_WIDTH + SSM_WIDTH + 2 * SSM_BC_WIDTH
XA_HEADS = 4
XA_HEADDIM = D_MODEL // XA_HEADS
EPS = 1e-6

LANES = 128
HALO_ROWS = 8
ROW_BLOCK = 128
IN_PROJ_ROW_BLOCK = 256
SIDE_CAST_STEPS = 64
MIXER_CHUNKS = 2
VMEM_LIMIT = 60 * 1024 * 1024

BF16 = jnp.bfloat16
F32 = jnp.float32


def _params(n_axes):
    return pltpu.CompilerParams(dimension_semantics=("arbitrary",) * n_axes,
                                vmem_limit_bytes=VMEM_LIMIT)


def _rmsnorm(x, w):
    return x * lax.rsqrt(jnp.mean(x * x, axis=-1, keepdims=True) + EPS) * w


def _silu(x):
    half = 0.5 * x
    return half + half * jnp.tanh(half)


def _gelu_tanh(x):
    c = 0.7978845608028654
    return 0.5 * x * (1.0 + jnp.tanh(x * (c + (0.044715 * c) * (x * x))))


def _dot(a, b):
    return jnp.dot(a, b, preferred_element_type=F32)


def _dot_nt(a, b):
    return lax.dot_general(a, b, (((1,), (1,)), ((), ())), preferred_element_type=F32)


def _dot_tn(a, b):
    return lax.dot_general(a, b, (((0,), (0,)), ((), ())), preferred_element_type=F32)


def _ffn_kernel(x_ref, nw_ref, wg_ref, wu_ref, wd_ref, tw_ref, o_ref, *rest, tail):
    xn_ref = rest[-1]
    j = pl.program_id(1)
    n_row_blocks = x_ref.shape[0] // ROW_BLOCK

    def row_block(r):
        return pl.ds(pl.multiple_of(r * ROW_BLOCK, ROW_BLOCK), ROW_BLOCK)

    @pl.when(j == 0)
    def _():
        def body(r, _):
            rows = row_block(r)
            x = x_ref[rows, :]
            xn_ref[rows, :] = _rmsnorm(x, nw_ref[...]).astype(BF16)
            o_ref[rows, :] = x
        lax.fori_loop(0, n_row_blocks, body, None)

    xn = xn_ref[...]
    g = _dot(xn, wg_ref[...].astype(BF16))
    u = _dot(xn, wu_ref[...].astype(BF16))
    o_ref[...] += _dot((_silu(g) * u).astype(BF16), (0.5 * wd_ref[...]).astype(BF16))

    if tail != "none":
        @pl.when(j == pl.num_programs(1) - 1)
        def _():
            def body(r, _):
                rows = row_block(r)
                hn = _rmsnorm(o_ref[rows, :], tw_ref[...])
                if tail == "final":
                    o_ref[rows, :] = hn
                else:
                    rest[0][rows, :] = hn.astype(BF16)
            lax.fori_loop(0, n_row_blocks, body, None)


def _ffn(x, norm_w, w_gu, w_down, tail_w, *, tail, tm=1024, tf=256):
    m, d = x.shape
    f = w_down.shape[0]
    nf = f // tf
    row_tile = pl.BlockSpec((tm, d), lambda i, j: (i, 0))
    out_specs, out_shape = [row_tile], [jax.ShapeDtypeStruct((m, d), F32)]
    if tail == "next":
        out_specs.append(row_tile)
        out_shape.append(jax.ShapeDtypeStruct((m, d), BF16))
    return pl.pallas_call(
        functools.partial(_ffn_kernel, tail=tail),
        grid=(m // tm, nf),
        in_specs=[
            row_tile,
            pl.BlockSpec((1, d), lambda i, j: (0, 0)),
            pl.BlockSpec((d, tf), lambda i, j: (0, j)),
            pl.BlockSpec((d, tf), lambda i, j: (0, j + nf)),
            pl.BlockSpec((tf, d), lambda i, j: (j, 0)),
            pl.BlockSpec((1, d), lambda i, j: (0, 0)),
        ],
        out_specs=out_specs,
        out_shape=out_shape,
        scratch_shapes=[pltpu.VMEM((tm, d), BF16)],
        compiler_params=_params(2),
        name="ffn",
    )(x, norm_w.reshape(1, d), w_gu, w_gu, w_down, tail_w.reshape(1, d))


def _in_proj_kernel(xn_ref, wt_ref, wdt_ref, cw_ref, cb_ref, *rest, gelu_tiles, silu_tiles, n_side):
    side_in, (o_ref, dt_ref) = rest[:n_side], rest[n_side:n_side + 2]
    side_out = rest[n_side + 2:2 * n_side + 2]
    raw_even_ref, raw_odd_ref, halo_ref = rest[2 * n_side + 2:]
    for src_ref, dst_ref in zip(side_in, side_out):
        dst_ref[...] = src_ref[...].astype(BF16)

    i = pl.program_id(0)
    j = pl.program_id(1)
    conv_first = gelu_tiles + silu_tiles
    raw_refs = (raw_even_ref, raw_odd_ref)

    @pl.when(j == 0)
    def _():
        dt_ref[...] = _dot_nt(xn_ref[...], wdt_ref[...].astype(BF16))

    tm = o_ref.shape[0]

    def project(consume):
        w = wt_ref[...].astype(BF16)
        for r0 in range(0, tm, IN_PROJ_ROW_BLOCK):
            consume(r0, _dot_nt(xn_ref[r0:r0 + IN_PROJ_ROW_BLOCK, :], w))

    def store_with(activation):
        def consume(r0, acc):
            o_ref[r0:r0 + IN_PROJ_ROW_BLOCK, :] = activation(acc)
        return consume

    @pl.when(j < gelu_tiles)
    def _():
        project(store_with(_gelu_tanh))

    @pl.when((j >= gelu_tiles) & (j < gelu_tiles + silu_tiles))
    def _():
        project(store_with(_silu))

    @pl.when(j >= conv_first)
    def _():
        c = j - conv_first
        tail = slice(IN_PROJ_ROW_BLOCK, IN_PROJ_ROW_BLOCK + HALO_ROWS)
        n_blocks = tm // IN_PROJ_ROW_BLOCK

        @pl.when(i == 0)
        def _():
            raw_refs[0][0:HALO_ROWS, :] = jnp.zeros((HALO_ROWS, o_ref.shape[1]), F32)

        @pl.when(i > 0)
        def _():
            raw_refs[0][0:HALO_ROWS, :] = halo_ref[c]

        def conv_silu(r0, acc):
            block = r0 // IN_PROJ_ROW_BLOCK
            raw, previous = raw_refs[block % 2], raw_refs[(block - 1) % 2]
            if block > 0:
                raw[0:HALO_ROWS, :] = previous[tail, :]
            raw[HALO_ROWS:, :] = acc
            out = cb_ref[...]
            for tap in range(SSM_CONV):
                off = HALO_ROWS - (SSM_CONV - 1) + tap
                out = out + raw[off:off + IN_PROJ_ROW_BLOCK, :] * cw_ref[tap:tap + 1, :]
            o_ref[r0:r0 + IN_PROJ_ROW_BLOCK, :] = _silu(out)

        project(conv_silu)
        halo_ref[c] = raw_refs[(n_blocks - 1) % 2][tail, :]


def _in_proj(xn, w_in_t, conv_w, conv_b, side_weights, *, tm=2048, tn=512):
    m, k = xn.shape
    n_col = PROJ_MAIN // tn
    gelu_tiles, silu_tiles = 2 * GM_WIDTH // tn, SSM_WIDTH // tn
    conv_tiles = (SSM_WIDTH + 2 * SSM_BC_WIDTH) // tn
    kernel_fn = functools.partial(_in_proj_kernel, gelu_tiles=gelu_tiles, silu_tiles=silu_tiles,
                                  n_side=len(side_weights))
    assert SIDE_CAST_STEPS <= (m // tm) * n_col
    side_specs = [pl.BlockSpec((w.shape[0] // SIDE_CAST_STEPS, w.shape[1]),
                               lambda i, j: (jnp.minimum(i * n_col + j, SIDE_CAST_STEPS - 1), 0))
                  for w in side_weights]
    conv_tile = lambda i, j: (0, jnp.maximum(j - (gelu_tiles + silu_tiles), 0))
    w_dt_t = jnp.pad(w_in_t[PROJ_MAIN:], ((0, LANES - (w_in_t.shape[0] - PROJ_MAIN)), (0, 0)))
    return pl.pallas_call(
        kernel_fn,
        grid=(m // tm, n_col),
        in_specs=[
            pl.BlockSpec((tm, k), lambda i, j: (i, 0)),
            pl.BlockSpec((tn, k), lambda i, j: (j, 0)),
            pl.BlockSpec((LANES, k), lambda i, j: (0, 0)),
            pl.BlockSpec((SSM_CONV, tn), conv_tile),
            pl.BlockSpec((1, tn), conv_tile),
            *side_specs,
        ],
        out_specs=[pl.BlockSpec((tm, tn), lambda i, j: (i, j)),
                   pl.BlockSpec((tm, LANES), lambda i, j: (i, 0)),
                   *side_specs],
        out_shape=[jax.ShapeDtypeStruct((m, PROJ_MAIN), F32), jax.ShapeDtypeStruct((m, LANES), F32),
                   *[jax.ShapeDtypeStruct(w.shape, BF16) for w in side_weights]],
        scratch_shapes=[pltpu.VMEM((HALO_ROWS + IN_PROJ_ROW_BLOCK, tn), F32),
                        pltpu.VMEM((HALO_ROWS + IN_PROJ_ROW_BLOCK, tn), F32),
                        pltpu.VMEM((conv_tiles, HALO_ROWS, tn), F32)],
        compiler_params=_params(2),
        name="in_proj",
    )(xn, w_in_t, w_dt_t, conv_w, conv_b.reshape(1, -1), *side_weights)


def _sgu_init(ws_ref, wsc_ref):
    t_idx = lax.broadcasted_iota(jnp.int32, (CHUNK, CHUNK), 0)
    s_idx = lax.broadcasted_iota(jnp.int32, (CHUNK, CHUNK), 1)
    for g in range(GM_GROUPS):
        wsc_ref[g] = jnp.where(s_idx <= t_idx, ws_ref[g], 0.0).astype(BF16)


def _sgu_mix(gu_ref, gv_ref, vw_ref, wsc_ref, bs_ref, o_ref):
    vn = _rmsnorm(gv_ref[...], vw_ref[...]).astype(BF16)
    bias = bs_ref[...]
    for g in range(GM_GROUPS):
        cols = slice(g * GM_GROUP_WIDTH, (g + 1) * GM_GROUP_WIDTH)
        mixed = _dot(wsc_ref[g], vn[:, cols]) + bias[:, g:g + 1]
        o_ref[:, cols] = (gu_ref[:, cols] * mixed).astype(o_ref.dtype)


def _split3(v):
    hi = v.astype(BF16).astype(F32)
    rest = v - hi
    mid = rest.astype(BF16).astype(F32)
    return hi, mid, rest - mid


def _pack3(v):
    hi, mid, lo = _split3(v)
    lane = lax.broadcasted_iota(jnp.int32, v.shape, 1)
    packed = jnp.where(lane < SSM_HEADS, hi,
                       jnp.where(lane < 2 * SSM_HEADS, pltpu.roll(mid, SSM_HEADS, 1),
                                 jnp.where(lane < 3 * SSM_HEADS, pltpu.roll(lo, 2 * SSM_HEADS, 1), 0.0)))
    return packed.astype(BF16)


def _ssd_init(state_ref, expand_ref):
    state_ref[...] = jnp.zeros_like(state_ref)
    k_idx = lax.broadcasted_iota(jnp.int32, (LANES, SSM_WIDTH), 0)
    j_idx = lax.broadcasted_iota(jnp.int32, (LANES, SSM_WIDTH), 1)
    hit = (k_idx < 3 * SSM_HEADS) & (j_idx // SSM_HEADDIM == k_idx % SSM_HEADS)
    expand_ref[...] = hit.astype(F32).astype(BF16)


def _ssd_scan(gate_ref, xs_ref, bc_ref, dt_ref, dtb_ref, alog_ref, dskip_ref, nw_ref, o_ref,
              state_ref, y_ref, expand_ref):
    pre = dt_ref[...] + dtb_ref[...]
    dt = jnp.maximum(pre, 0.0) + jnp.log1p(jnp.exp(-jnp.abs(pre)))
    da = dt * (-jnp.exp(alog_ref[...]))
    row = lax.broadcasted_iota(jnp.int32, (CHUNK, CHUNK), 0)
    col = lax.broadcasted_iota(jnp.int32, (CHUNK, CHUNK), 1)
    causal = col <= row
    tri = causal.astype(F32).astype(BF16)
    da_hi, da_mid, da_lo = _split3(da)
    parts = _dot(tri, jnp.concatenate([da_hi, da_mid, da_lo], axis=1).astype(BF16))
    cum = parts[:, :LANES] + parts[:, LANES:2 * LANES] + parts[:, 2 * LANES:]
    cum_t = cum.T
    cum_last = cum[CHUNK - 1:CHUNK, :]

    packed = jnp.concatenate([_pack3(dt), _pack3(dt * jnp.exp(cum_last - cum)), _pack3(jnp.exp(cum))], axis=0)
    expanded = _dot(packed, expand_ref[...])
    dt_e = expanded[0:CHUNK]
    dt_decay_end_e = expanded[CHUNK:2 * CHUNK]
    exp_cum_e = expanded[2 * CHUNK:]
    chunk_decay_e = exp_cum_e[CHUNK - 1:CHUNK, :]

    xs = xs_ref[...]
    xdt = (xs * dt_e).astype(BF16)
    xdt_end = (xs * dt_decay_end_e).astype(BF16)
    lane_head = lax.broadcasted_iota(jnp.int32, (1, SSM_GROUP_WIDTH), 1) // SSM_HEADDIM
    head_lanes = [(lane_head == r).astype(F32).astype(BF16) for r in range(SSM_HEADS_PER_GROUP)]

    b_of = lambda g: bc_ref[:, g * SSM_STATE:(g + 1) * SSM_STATE].astype(BF16)
    c_of = lambda g: bc_ref[:, SSM_BC_WIDTH + g * SSM_STATE:SSM_BC_WIDTH + (g + 1) * SSM_STATE].astype(BF16)
    for g in range(SSM_GROUPS):
        gcols = slice(g * SSM_GROUP_WIDTH, (g + 1) * SSM_GROUP_WIDTH)
        b_g, c_g = b_of(g), c_of(g)
        if g % 2 == 0:
            pair = _dot_nt(jnp.concatenate([c_g, c_of(g + 1)], axis=0), jnp.concatenate([b_g, b_of(g + 1)], axis=0))
        half = (g % 2) * CHUNK
        scores = pair[half:half + CHUNK, half:half + CHUNK]
        xdt_g = xdt[:, gcols]
        y = _dot(c_g, state_ref[g].astype(BF16)) * exp_cum_e[:, gcols]
        decayed, stacked = [], []
        for r in range(SSM_HEADS_PER_GROUP):
            h = g * SSM_HEADS_PER_GROUP + r
            seg = cum[:, h:h + 1] - cum_t[h:h + 1, :]
            decay = jnp.exp(jnp.where(causal, seg, -jnp.inf))
            decayed.append((scores * decay).astype(BF16))
            stacked.append(xdt_g * head_lanes[r])
        y = y + _dot(jnp.concatenate(decayed, axis=1), jnp.concatenate(stacked, axis=0))
        state_ref[g] = state_ref[g] * chunk_decay_e[:, gcols] + _dot_tn(b_g, xdt_end[:, gcols])
        y_ref[:, gcols] = y

    y = y_ref[...] + dskip_ref[...] * xs
    y = y * gate_ref[...]
    o_ref[...] = _rmsnorm(y, nw_ref[...]).astype(o_ref.dtype)


def _mixer_kernel(gu_ref, gv_ref, vw_ref, ws_ref, bs_ref,
                  gate_ref, xs_ref, bc_ref, dt_ref, dtb_ref, alog_ref, dskip_ref, nw_ref,
                  o_ref,
                  state_ref, y_ref, expand_ref, wsc_ref):
    @pl.when(pl.program_id(0) == 0)
    def _():
        _ssd_init(state_ref, expand_ref)
        _sgu_init(ws_ref, wsc_ref)

    for c in range(MIXER_CHUNKS):
        rows = lambda ref: ref.at[c * CHUNK:(c + 1) * CHUNK, :]
        out = rows(o_ref)
        _sgu_mix(rows(gu_ref), rows(gv_ref), vw_ref, wsc_ref, bs_ref, out.at[:, 0:GM_WIDTH])
        _ssd_scan(rows(gate_ref), rows(xs_ref), rows(bc_ref), rows(dt_ref), dtb_ref, alog_ref, dskip_ref, nw_ref,
                  out.at[:, GM_WIDTH:], state_ref, y_ref, expand_ref)


def _mixer(proj, dt_raw, v_norm_w, w_s, b_s_t, dt_bias, a_log, d_skip_e, norm_w):
    m = proj.shape[0]
    row = lambda a: a.reshape(1, -1)
    c0 = 2 * GM_WIDTH // SSM_WIDTH
    full = lambda shape: pl.BlockSpec(shape, lambda i: (0,) * len(shape))
    chunk = lambda width, col: pl.BlockSpec((MIXER_CHUNKS * CHUNK, width), lambda i: (i, col))
    return pl.pallas_call(
        _mixer_kernel,
        grid=(m // (MIXER_CHUNKS * CHUNK),),
        in_specs=[
            chunk(GM_WIDTH, 0),
            chunk(GM_WIDTH, 1),
            full((1, GM_WIDTH)),
            full((GM_GROUPS, CHUNK, CHUNK)),
            full((CHUNK, LANES)),
            chunk(SSM_WIDTH, c0),
            chunk(SSM_WIDTH, c0 + 1),
            chunk(2 * SSM_BC_WIDTH, c0 + 2),
            chunk(LANES, 0),
            full((1, LANES)),
            full((1, LANES)),
            full((1, SSM_WIDTH)),
            full((1, SSM_WIDTH)),
        ],
        out_specs=chunk(GM_WIDTH + SSM_WIDTH, 0),
        out_shape=jax.ShapeDtypeStruct((m, GM_WIDTH + SSM_WIDTH), BF16),
        scratch_shapes=[pltpu.VMEM((SSM_GROUPS, SSM_STATE, SSM_GROUP_WIDTH), F32),
                        pltpu.VMEM((CHUNK, SSM_WIDTH), F32),
                        pltpu.VMEM((LANES, SSM_WIDTH), BF16),
                        pltpu.VMEM((GM_GROUPS, CHUNK, CHUNK), BF16)],
        compiler_params=_params(1),
        name="mixer",
    )(proj, proj, row(v_norm_w), w_s, b_s_t,
      proj, proj, proj, dt_raw, row(dt_bias), row(a_log), row(d_skip_e), row(norm_w))


WEIGHT_STAGE_ROWS = 256


def _for_each_row_block(w_hbm, cols, stage_ref, sems, use):
    n_blocks = w_hbm.shape[0] // WEIGHT_STAGE_ROWS

    def block_rows(r):
        return pl.ds(pl.multiple_of(r * WEIGHT_STAGE_ROWS, WEIGHT_STAGE_ROWS), WEIGHT_STAGE_ROWS)

    def copy(r, slot):
        return pltpu.make_async_copy(w_hbm.at[block_rows(r), cols], stage_ref.at[slot], sems.at[slot])

    copy(0, 0).start()

    def body(r, _):
        slot = r % 2

        @pl.when(r + 1 < n_blocks)
        def _():
            copy(r + 1, 1 - slot).start()

        copy(r, slot).wait()
        use(block_rows(r), stage_ref.at[slot])

    lax.fori_loop(0, n_blocks, body, None)


def _project_memory(memn_ref, w_hbm, cols, out_ref, acc_ref, stage_ref, sems):
    acc_ref[...] = jnp.zeros_like(acc_ref)

    def use(rows, block_ref):
        acc_ref[...] += _dot(memn_ref[:, rows], block_ref[...].astype(BF16))

    _for_each_row_block(w_hbm, cols, stage_ref, sems, use)
    out_ref[...] = acc_ref[...].astype(BF16)


def _xattn_kernel(h_ref, mixed_ref, nw_ref, mem_ref, memw_ref, wout_ref, wq_ref, wo_ref, wkv_hbm,
                  o_ref,
                  k_ref, v_ref, stage_ref, sems, h2_ref, q_ref, att_ref):
    d = h_ref.shape[1]

    @pl.when(pl.program_id(0) == 0)
    def _():
        q_ref[...] = _rmsnorm(mem_ref[...], memw_ref[...]).astype(BF16)
        _project_memory(q_ref, wkv_hbm, pl.ds(0, d), k_ref, h2_ref, stage_ref, sems)
        _project_memory(q_ref, wkv_hbm, pl.ds(d, d), v_ref, h2_ref, stage_ref, sems)

    h2_ref[...] = h_ref[...] + _dot(mixed_ref[...], wout_ref[...])
    q_ref[...] = _dot(_rmsnorm(h2_ref[...], nw_ref[...]).astype(BF16), wq_ref[...]).astype(BF16)
    scale = XA_HEADDIM ** -0.5
    for h in range(XA_HEADS):
        cols = slice(h * XA_HEADDIM, (h + 1) * XA_HEADDIM)
        logits = _dot_nt(q_ref[:, cols], k_ref[:, cols]) * scale
        p = jnp.exp(logits - jnp.max(logits, axis=-1, keepdims=True))
        p = p / jnp.sum(p, axis=-1, keepdims=True)
        att_ref[:, cols] = _dot(p.astype(BF16), v_ref[:, cols]).astype(BF16)
    o_ref[...] = h2_ref[...] + _dot(att_ref[...], wo_ref[...])


def _xattn(h, mixed, norm_w, mem, mem_norm_w, w_out, w_q, w_o, w_kv):
    m, d = h.shape
    tm = MEM_LEN
    once = pl.Buffered(1)
    resident = lambda w: pl.BlockSpec(w.shape, lambda i: (0, 0), pipeline_mode=once)
    return pl.pallas_call(
        _xattn_kernel,
        grid=(m // tm,),
        in_specs=[
            pl.BlockSpec((tm, d), lambda i: (i, 0)),
            pl.BlockSpec((tm, mixed.shape[1]), lambda i: (i, 0)),
            pl.BlockSpec((1, d), lambda i: (0, 0)),
            pl.BlockSpec((MEM_LEN, d), lambda i: (0, 0), pipeline_mode=once),
            pl.BlockSpec((1, d), lambda i: (0, 0)),
            resident(w_out), resident(w_q), resident(w_o),
            pl.BlockSpec(memory_space=pl.ANY),
        ],
        out_specs=pl.BlockSpec((tm, d), lambda i: (i, 0)),
        out_shape=jax.ShapeDtypeStruct((m, d), F32),
        scratch_shapes=[pltpu.VMEM((MEM_LEN, d), BF16),
                        pltpu.VMEM((MEM_LEN, d), BF16),
                        pltpu.VMEM((2, WEIGHT_STAGE_ROWS, d), F32),
                        pltpu.SemaphoreType.DMA((2,)),
                        pltpu.VMEM((tm, d), F32),
                        pltpu.VMEM((tm, d), BF16),
                        pltpu.VMEM((tm, d), BF16)],
        compiler_params=_params(1),
        name="xattn",
    )(h, mixed, norm_w.reshape(1, d), mem, mem_norm_w.reshape(1, d), w_out, w_q, w_o, w_kv)


def kernel(x, mem, ffn1_norm, ffn1_w_gu, ffn1_w_down, mix_norm, w_in, gm_v_norm, gm_w_s, gm_b_s, ssm_conv_w, ssm_conv_b, ssm_dt_bias, ssm_a_log, ssm_d, ssm_norm, w_out, xa_norm, mem_norm, xa_w_q, xa_w_kv, xa_w_o, ffn2_norm, ffn2_w_gu, ffn2_w_down, final_norm):
    pad_lanes = lambda a: jnp.pad(a, ((0, 0), (0, LANES - a.shape[1])))
    h = x[0]
    for i in range(ffn1_norm.shape[0]):
        h, xn = _ffn(h, ffn1_norm[i], ffn1_w_gu[i], ffn1_w_down[i], mix_norm[i], tail="next")

        proj, dt_raw, w_out_b, w_q_b, w_o_b = _in_proj(xn, jnp.swapaxes(w_in[i], 0, 1), ssm_conv_w[i], ssm_conv_b[i],
                                                       [w_out[i], xa_w_q[i], xa_w_o[i]])
        mixed = _mixer(proj, dt_raw, gm_v_norm[i], gm_w_s[i], pad_lanes(gm_b_s[i].T),
                       pad_lanes(ssm_dt_bias[i][None])[0], pad_lanes(ssm_a_log[i][None])[0],
                       jnp.repeat(ssm_d[i], SSM_HEADDIM), ssm_norm[i])
        h = _xattn(h, mixed, xa_norm[i], mem[0], mem_norm[i], w_out_b, w_q_b, w_o_b, xa_w_kv[i])

        last = i == ffn1_norm.shape[0] - 1
        h = _ffn(h, ffn2_norm[i], ffn2_w_gu[i], ffn2_w_down[i], final_norm, tail="final" if last else "none")[0]
    return h[None]
```

```python
import functools

import jax
import jax.numpy as jnp
from jax import lax
from jax.experimental import pallas as pl
from jax.experimental.pallas import tpu as pltpu

D_MODEL = 2048
MEM_LEN = 256
GM_WIDTH = 2048
GM_GROUPS = 4
GM_GROUP_WIDTH = GM_WIDTH // GM_GROUPS
CHUNK = 128
SSM_WIDTH = 2048
SSM_HEADDIM = 64
SSM_HEADS = SSM_WIDTH // SSM_HEADDIM
SSM_GROUPS = 8
SSM_HEADS_PER_GROUP = SSM_HEADS // SSM_GROUPS
SSM_GROUP_WIDTH = SSM_HEADS_PER_GROUP * SSM_HEADDIM
SSM_STATE = 128
SSM_CONV = 4
SSM_BC_WIDTH = SSM_GROUPS * SSM_STATE
PROJ_MAIN = 2 * GM_WIDTH + SSM_WIDTH + SSM_WIDTH + 2 * SSM_BC_WIDTH
XA_HEADS = 4
XA_HEADDIM = D_MODEL // XA_HEADS
EPS = 1e-6

LANES = 128
HALO_ROWS = 8
ROW_BLOCK = 128
IN_PROJ_ROW_BLOCK = 256
SIDE_CAST_STEPS = 64
MIXER_CHUNKS = 4
VMEM_LIMIT = 60 * 1024 * 1024

BF16 = jnp.bfloat16
F32 = jnp.float32


def _params(n_axes):
    return pltpu.CompilerParams(dimension_semantics=("arbitrary",) * n_axes,
                                vmem_limit_bytes=VMEM_LIMIT)


def _rmsnorm(x, w):
    return x * lax.rsqrt(jnp.mean(x * x, axis=-1, keepdims=True) + EPS) * w


def _silu(x):
    half = 0.5 * x
    return half + half * jnp.tanh(half)


def _gelu_tanh(x):
    c = 0.7978845608028654
    return 0.5 * x * (1.0 + jnp.tanh(x * (c + (0.044715 * c) * (x * x))))


def _dot(a, b):
    return jnp.dot(a, b, preferred_element_type=F32)


def _dot_nt(a, b):
    return lax.dot_general(a, b, (((1,), (1,)), ((), ())), preferred_element_type=F32)


def _dot_tn(a, b):
    return lax.dot_general(a, b, (((0,), (0,)), ((), ())), preferred_element_type=F32)


def _ffn_kernel(x_ref, nw_ref, wg_ref, wu_ref, wd_ref, tw_ref, o_ref, *rest, tail):
    xn_ref = rest[-1]
    j = pl.program_id(1)
    n_row_blocks = x_ref.shape[0] // ROW_BLOCK

    def row_block(r):
        return pl.ds(pl.multiple_of(r * ROW_BLOCK, ROW_BLOCK), ROW_BLOCK)

    @pl.when(j == 0)
    def _():
        def body(r, _):
            rows = row_block(r)
            x = x_ref[rows, :]
            xn_ref[rows, :] = _rmsnorm(x, nw_ref[...]).astype(BF16)
            o_ref[rows, :] = x
        lax.fori_loop(0, n_row_blocks, body, None)

    xn = xn_ref[...]
    g = _dot(xn, wg_ref[...].astype(BF16))
    u = _dot(xn, wu_ref[...].astype(BF16))
    o_ref[...] += _dot((_silu(g) * u).astype(BF16), (0.5 * wd_ref[...]).astype(BF16))

    if tail != "none":
        @pl.when(j == pl.num_programs(1) - 1)
        def _():
            def body(r, _):
                rows = row_block(r)
                hn = _rmsnorm(o_ref[rows, :], tw_ref[...])
                if tail == "final":
                    o_ref[rows, :] = hn
                else:
                    rest[0][rows, :] = hn.astype(BF16)
            lax.fori_loop(0, n_row_blocks, body, None)


def _ffn(x, norm_w, w_gu, w_down, tail_w, *, tail, tm=1024, tf=256):
    m, d = x.shape
    f = w_down.shape[0]
    nf = f // tf
    row_tile = pl.BlockSpec((tm, d), lambda i, j: (i, 0))
    out_specs, out_shape = [row_tile], [jax.ShapeDtypeStruct((m, d), F32)]
    if tail == "next":
        out_specs.append(row_tile)
        out_shape.append(jax.ShapeDtypeStruct((m, d), BF16))
    return pl.pallas_call(
        functools.partial(_ffn_kernel, tail=tail),
        grid=(m // tm, nf),
        in_specs=[
            row_tile,
            pl.BlockSpec((1, d), lambda i, j: (0, 0)),
            pl.BlockSpec((d, tf), lambda i, j: (0, j)),
            pl.BlockSpec((d, tf), lambda i, j: (0, j + nf)),
            pl.BlockSpec((tf, d), lambda i, j: (j, 0)),
            pl.BlockSpec((1, d), lambda i, j: (0, 0)),
        ],
        out_specs=out_specs,
        out_shape=out_shape,
        scratch_shapes=[pltpu.VMEM((tm, d), BF16)],
        compiler_params=_params(2),
        name="ffn",
    )(x, norm_w.reshape(1, d), w_gu, w_gu, w_down, tail_w.reshape(1, d))


def _in_proj_kernel(xn_ref, wt_ref, wdt_ref, cw_ref, cb_ref, *rest, gelu_tiles, silu_tiles, n_side):
    side_in, (o_ref, dt_ref) = rest[:n_side], rest[n_side:n_side + 2]
    side_out = rest[n_side + 2:2 * n_side + 2]
    raw_even_ref, raw_odd_ref, halo_ref = rest[2 * n_side + 2:]
    for src_ref, dst_ref in zip(side_in, side_out):
        dst_ref[...] = src_ref[...].astype(BF16)

    i = pl.program_id(0)
    j = pl.program_id(1)
    conv_first = gelu_tiles + silu_tiles
    raw_refs = (raw_even_ref, raw_odd_ref)

    @pl.when(j == 0)
    def _():
        dt_ref[...] = _dot_nt(xn_ref[...], wdt_ref[...].astype(BF16))

    tm = o_ref.shape[0]

    def project(consume):
        w = wt_ref[...].astype(BF16)
        for r0 in range(0, tm, IN_PROJ_ROW_BLOCK):
            consume(r0, _dot_nt(xn_ref[r0:r0 + IN_PROJ_ROW_BLOCK, :], w))

    def store_with(activation):
        def consume(r0, acc):
            o_ref[r0:r0 + IN_PROJ_ROW_BLOCK, :] = activation(acc)
        return consume

    @pl.when(j < gelu_tiles)
    def _():
        project(store_with(_gelu_tanh))

    @pl.when((j >= gelu_tiles) & (j < gelu_tiles + silu_tiles))
    def _():
        project(store_with(_silu))

    @pl.when(j >= conv_first)
    def _():
        c = j - conv_first
        tail = slice(IN_PROJ_ROW_BLOCK, IN_PROJ_ROW_BLOCK + HALO_ROWS)
        n_blocks = tm // IN_PROJ_ROW_BLOCK

        @pl.when(i == 0)
        def _():
            raw_refs[0][0:HALO_ROWS, :] = jnp.zeros((HALO_ROWS, o_ref.shape[1]), F32)

        @pl.when(i > 0)
        def _():
            raw_refs[0][0:HALO_ROWS, :] = halo_ref[c]

        def conv_silu(r0, acc):
            block = r0 // IN_PROJ_ROW_BLOCK
            raw, previous = raw_refs[block % 2], raw_refs[(block - 1) % 2]
            if block > 0:
                raw[0:HALO_ROWS, :] = previous[tail, :]
            raw[HALO_ROWS:, :] = acc
            out = cb_ref[...]
            for tap in range(SSM_CONV):
                off = HALO_ROWS - (SSM_CONV - 1) + tap
                out = out + raw[off:off + IN_PROJ_ROW_BLOCK, :] * cw_ref[tap:tap + 1, :]
            o_ref[r0:r0 + IN_PROJ_ROW_BLOCK, :] = _silu(out)

        project(conv_silu)
        halo_ref[c] = raw_refs[(n_blocks - 1) % 2][tail, :]


def _in_proj(xn, w_in_t, conv_w, conv_b, side_weights, *, tm=2048, tn=512):
    m, k = xn.shape
    n_col = PROJ_MAIN // tn
    gelu_tiles, silu_tiles = 2 * GM_WIDTH // tn, SSM_WIDTH // tn
    conv_tiles = (SSM_WIDTH + 2 * SSM_BC_WIDTH) // tn
    kernel_fn = functools.partial(_in_proj_kernel, gelu_tiles=gelu_tiles, silu_tiles=silu_tiles,
                                  n_side=len(side_weights))
    assert SIDE_CAST_STEPS <= (m // tm) * n_col
    side_specs = [pl.BlockSpec((w.shape[0] // SIDE_CAST_STEPS, w.shape[1]),
                               lambda i, j: (jnp.minimum(i * n_col + j, SIDE_CAST_STEPS - 1), 0))
                  for w in side_weights]
    conv_tile = lambda i, j: (0, jnp.maximum(j - (gelu_tiles + silu_tiles), 0))
    w_dt_t = jnp.pad(w_in_t[PROJ_MAIN:], ((0, LANES - (w_in_t.shape[0] - PROJ_MAIN)), (0, 0)))
    return pl.pallas_call(
        kernel_fn,
        grid=(m // tm, n_col),
        in_specs=[
            pl.BlockSpec((tm, k), lambda i, j: (i, 0)),
            pl.BlockSpec((tn, k), lambda i, j: (j, 0)),
            pl.BlockSpec((LANES, k), lambda i, j: (0, 0)),
            pl.BlockSpec((SSM_CONV, tn), conv_tile),
            pl.BlockSpec((1, tn), conv_tile),
            *side_specs,
        ],
        out_specs=[pl.BlockSpec((tm, tn), lambda i, j: (i, j)),
                   pl.BlockSpec((tm, LANES), lambda i, j: (i, 0)),
                   *side_specs],
        out_shape=[jax.ShapeDtypeStruct((m, PROJ_MAIN), F32), jax.ShapeDtypeStruct((m, LANES), F32),
                   *[jax.ShapeDtypeStruct(w.shape, BF16) for w in side_weights]],
        scratch_shapes=[pltpu.VMEM((HALO_ROWS + IN_PROJ_ROW_BLOCK, tn), F32),
                        pltpu.VMEM((HALO_ROWS + IN_PROJ_ROW_BLOCK, tn), F32),
                        pltpu.VMEM((conv_tiles, HALO_ROWS, tn), F32)],
        compiler_params=_params(2),
        name="in_proj",
    )(xn, w_in_t, w_dt_t, conv_w, conv_b.reshape(1, -1), *side_weights)


def _sgu_init(ws_ref, wsc_ref):
    t_idx = lax.broadcasted_iota(jnp.int32, (CHUNK, CHUNK), 0)
    s_idx = lax.broadcasted_iota(jnp.int32, (CHUNK, CHUNK), 1)
    for g in range(GM_GROUPS):
        wsc_ref[g] = jnp.where(s_idx <= t_idx, ws_ref[g], 0.0).astype(BF16)


def _sgu_mix(gu_ref, gv_ref, vw_ref, wsc_ref, bs_ref, o_ref):
    vn = _rmsnorm(gv_ref[...], vw_ref[...]).astype(BF16)
    bias = bs_ref[...]
    for g in range(GM_GROUPS):
        cols = slice(g * GM_GROUP_WIDTH, (g + 1) * GM_GROUP_WIDTH)
        mixed = _dot(wsc_ref[g], vn[:, cols]) + bias[:, g:g + 1]
        o_ref[:, cols] = (gu_ref[:, cols] * mixed).astype(o_ref.dtype)


def _split3(v):
    hi = v.astype(BF16).astype(F32)
    rest = v - hi
    mid = rest.astype(BF16).astype(F32)
    return hi, mid, rest - mid


def _pack3(v):
    hi, mid, lo = _split3(v)
    lane = lax.broadcasted_iota(jnp.int32, v.shape, 1)
    packed = jnp.where(lane < SSM_HEADS, hi,
                       jnp.where(lane < 2 * SSM_HEADS, pltpu.roll(mid, SSM_HEADS, 1),
                                 jnp.where(lane < 3 * SSM_HEADS, pltpu.roll(lo, 2 * SSM_HEADS, 1), 0.0)))
    return packed.astype(BF16)


def _ssd_init(state_ref, expand_ref):
    state_ref[...] = jnp.zeros_like(state_ref)
    k_idx = lax.broadcasted_iota(jnp.int32, (LANES, SSM_WIDTH), 0)
    j_idx = lax.broadcasted_iota(jnp.int32, (LANES, SSM_WIDTH), 1)
    hit = (k_idx < 3 * SSM_HEADS) & (j_idx // SSM_HEADDIM == k_idx % SSM_HEADS)
    expand_ref[...] = hit.astype(F32).astype(BF16)


def _ssd_scan(gate_ref, xs_ref, bc_ref, dt_ref, dtb_ref, alog_ref, dskip_ref, nw_ref, o_ref,
              state_ref, y_ref, expand_ref):
    pre = dt_ref[...] + dtb_ref[...]
    dt = jnp.maximum(pre, 0.0) + jnp.log1p(jnp.exp(-jnp.abs(pre)))
    da = dt * (-jnp.exp(alog_ref[...]))
    row = lax.broadcasted_iota(jnp.int32, (CHUNK, CHUNK), 0)
    col = lax.broadcasted_iota(jnp.int32, (CHUNK, CHUNK), 1)
    causal = col <= row
    tri = causal.astype(F32).astype(BF16)
    da_hi, da_mid, da_lo = _split3(da)
    parts = _dot(tri, jnp.concatenate([da_hi, da_mid, da_lo], axis=1).astype(BF16))
    cum = parts[:, :LANES] + parts[:, LANES:2 * LANES] + parts[:, 2 * LANES:]
    cum_t = cum.T
    cum_last = cum[CHUNK - 1:CHUNK, :]

    packed = jnp.concatenate([_pack3(dt), _pack3(dt * jnp.exp(cum_last - cum)), _pack3(jnp.exp(cum))], axis=0)
    expanded = _dot(packed, expand_ref[...])
    dt_e = expanded[0:CHUNK]
    dt_decay_end_e = expanded[CHUNK:2 * CHUNK]
    exp_cum_e = expanded[2 * CHUNK:]
    chunk_decay_e = exp_cum_e[CHUNK - 1:CHUNK, :]

    xs = xs_ref[...]
    xdt = (xs * dt_e).astype(BF16)
    xdt_end = (xs * dt_decay_end_e).astype(BF16)
    lane_head = lax.broadcasted_iota(jnp.int32, (1, SSM_GROUP_WIDTH), 1) // SSM_HEADDIM
    head_lanes = [(lane_head == r).astype(F32).astype(BF16) for r in range(SSM_HEADS_PER_GROUP)]

    b_of = lambda g: bc_ref[:, g * SSM_STATE:(g + 1) * SSM_STATE].astype(BF16)
    c_of = lambda g: bc_ref[:, SSM_BC_WIDTH + g * SSM_STATE:SSM_BC_WIDTH + (g + 1) * SSM_STATE].astype(BF16)
    for g in range(SSM_GROUPS):
        gcols = slice(g * SSM_GROUP_WIDTH, (g + 1) * SSM_GROUP_WIDTH)
        b_g, c_g = b_of(g), c_of(g)
        if g % 2 == 0:
            pair = _dot_nt(jnp.concatenate([c_g, c_of(g + 1)], axis=0), jnp.concatenate([b_g, b_of(g + 1)], axis=0))
        half = (g % 2) * CHUNK
        scores = pair[half:half + CHUNK, half:half + CHUNK]
        xdt_g = xdt[:, gcols]
        y = _dot(c_g, state_ref[g].astype(BF16)) * exp_cum_e[:, gcols]
        decayed, stacked = [], []
        for r in range(SSM_HEADS_PER_GROUP):
            h = g * SSM_HEADS_PER_GROUP + r
            seg = cum[:, h:h + 1] - cum_t[h:h + 1, :]
            decay = jnp.exp(jnp.where(causal, seg, -jnp.inf))
            decayed.append((scores * decay).astype(BF16))
            stacked.append(xdt_g * head_lanes[r])
        y = y + _dot(jnp.concatenate(decayed, axis=1), jnp.concatenate(stacked, axis=0))
        state_ref[g] = state_ref[g] * chunk_decay_e[:, gcols] + _dot_tn(b_g, xdt_end[:, gcols])
        y_ref[:, gcols] = y

    y = y_ref[...] + dskip_ref[...] * xs
    y = y * gate_ref[...]
    o_ref[...] = _rmsnorm(y, nw_ref[...]).astype(o_ref.dtype)


def _mixer_kernel(gu_ref, gv_ref, vw_ref, ws_ref, bs_ref,
                  gate_ref, xs_ref, bc_ref, dt_ref, dtb_ref, alog_ref, dskip_ref, nw_ref,
                  o_ref,
                  state_ref, y_ref, expand_ref, wsc_ref):
    @pl.when(pl.program_id(0) == 0)
    def _():
        _ssd_init(state_ref, expand_ref)
        _sgu_init(ws_ref, wsc_ref)

    for c in range(MIXER_CHUNKS):
        rows = lambda ref: ref.at[c * CHUNK:(c + 1) * CHUNK, :]
        out = rows(o_ref)
        _sgu_mix(rows(gu_ref), rows(gv_ref), vw_ref, wsc_ref, bs_ref, out.at[:, 0:GM_WIDTH])
        _ssd_scan(rows(gate_ref), rows(xs_ref), rows(bc_ref), rows(dt_ref), dtb_ref, alog_ref, dskip_ref, nw_ref,
                  out.at[:, GM_WIDTH:], state_ref, y_ref, expand_ref)


def _mixer(proj, dt_raw, v_norm_w, w_s, b_s_t, dt_bias, a_log, d_skip_e, norm_w):
    m = proj.shape[0]
    row = lambda a: a.reshape(1, -1)
    c0 = 2 * GM_WIDTH // SSM_WIDTH
    full = lambda shape: pl.BlockSpec(shape, lambda i: (0,) * len(shape))
    chunk = lambda width, col: pl.BlockSpec((MIXER_CHUNKS * CHUNK, width), lambda i: (i, col))
    return pl.pallas_call(
        _mixer_kernel,
        grid=(m // (MIXER_CHUNKS * CHUNK),),
        in_specs=[
            chunk(GM_WIDTH, 0),
            chunk(GM_WIDTH, 1),
            full((1, GM_WIDTH)),
            full((GM_GROUPS, CHUNK, CHUNK)),
            full((CHUNK, LANES)),
            chunk(SSM_WIDTH, c0),
            chunk(SSM_WIDTH, c0 + 1),
            chunk(2 * SSM_BC_WIDTH, c0 + 2),
            chunk(LANES, 0),
            full((1, LANES)),
            full((1, LANES)),
            full((1, SSM_WIDTH)),
            full((1, SSM_WIDTH)),
        ],
        out_specs=chunk(GM_WIDTH + SSM_WIDTH, 0),
        out_shape=jax.ShapeDtypeStruct((m, GM_WIDTH + SSM_WIDTH), BF16),
        scratch_shapes=[pltpu.VMEM((SSM_GROUPS, SSM_STATE, SSM_GROUP_WIDTH), F32),
                        pltpu.VMEM((CHUNK, SSM_WIDTH), F32),
                        pltpu.VMEM((LANES, SSM_WIDTH), BF16),
                        pltpu.VMEM((GM_GROUPS, CHUNK, CHUNK), BF16)],
        compiler_params=_params(1),
        name="mixer",
    )(proj, proj, row(v_norm_w), w_s, b_s_t,
      proj, proj, proj, dt_raw, row(dt_bias), row(a_log), row(d_skip_e), row(norm_w))


WEIGHT_STAGE_ROWS = 256


def _for_each_row_block(w_hbm, cols, stage_ref, sems, use):
    n_blocks = w_hbm.shape[0] // WEIGHT_STAGE_ROWS

    def block_rows(r):
        return pl.ds(pl.multiple_of(r * WEIGHT_STAGE_ROWS, WEIGHT_STAGE_ROWS), WEIGHT_STAGE_ROWS)

    def copy(r, slot):
        return pltpu.make_async_copy(w_hbm.at[block_rows(r), cols], stage_ref.at[slot], sems.at[slot])

    copy(0, 0).start()

    def body(r, _):
        slot = r % 2

        @pl.when(r + 1 < n_blocks)
        def _():
            copy(r + 1, 1 - slot).start()

        copy(r, slot).wait()
        use(block_rows(r), stage_ref.at[slot])

    lax.fori_loop(0, n_blocks, body, None)


def _project_memory(memn_ref, w_hbm, cols, out_ref, acc_ref, stage_ref, sems):
    acc_ref[...] = jnp.zeros_like(acc_ref)

    def use(rows, block_ref):
        acc_ref[...] += _dot(memn_ref[:, rows], block_ref[...].astype(BF16))

    _for_each_row_block(w_hbm, cols, stage_ref, sems, use)
    out_ref[...] = acc_ref[...].astype(BF16)


def _xattn_kernel(h_ref, mixed_ref, nw_ref, mem_ref, memw_ref, wout_ref, wq_ref, wo_ref, wkv_hbm,
                  o_ref,
                  k_ref, v_ref, stage_ref, sems, h2_ref, q_ref, att_ref):
    d = h_ref.shape[1]

    @pl.when(pl.program_id(0) == 0)
    def _():
        q_ref[...] = _rmsnorm(mem_ref[...], memw_ref[...]).astype(BF16)
        _project_memory(q_ref, wkv_hbm, pl.ds(0, d), k_ref, h2_ref, stage_ref, sems)
        _project_memory(q_ref, wkv_hbm, pl.ds(d, d), v_ref, h2_ref, stage_ref, sems)

    h2_ref[...] = h_ref[...] + _dot(mixed_ref[...], wout_ref[...])
    q_ref[...] = _dot(_rmsnorm(h2_ref[...], nw_ref[...]).astype(BF16), wq_ref[...]).astype(BF16)
    scale = XA_HEADDIM ** -0.5
    for h in range(XA_HEADS):
        cols = slice(h * XA_HEADDIM, (h + 1) * XA_HEADDIM)
        logits = _dot_nt(q_ref[:, cols], k_ref[:, cols]) * scale
        p = jnp.exp(logits - jnp.max(logits, axis=-1, keepdims=True))
        p = p / jnp.sum(p, axis=-1, keepdims=True)
        att_ref[:, cols] = _dot(p.astype(BF16), v_ref[:, cols]).astype(BF16)
    o_ref[...] = h2_ref[...] + _dot(att_ref[...], wo_ref[...])


def _xattn(h, mixed, norm_w, mem, mem_norm_w, w_out, w_q, w_o, w_kv):
    m, d = h.shape
    tm = MEM_LEN
    once = pl.Buffered(1)
    resident = lambda w: pl.BlockSpec(w.shape, lambda i: (0, 0), pipeline_mode=once)
    return pl.pallas_call(
        _xattn_kernel,
        grid=(m // tm,),
        in_specs=[
            pl.BlockSpec((tm, d), lambda i: (i, 0)),
            pl.BlockSpec((tm, mixed.shape[1]), lambda i: (i, 0)),
            pl.BlockSpec((1, d), lambda i: (0, 0)),
            pl.BlockSpec((MEM_LEN, d), lambda i: (0, 0), pipeline_mode=once),
            pl.BlockSpec((1, d), lambda i: (0, 0)),
            resident(w_out), resident(w_q), resident(w_o),
            pl.BlockSpec(memory_space=pl.ANY),
        ],
        out_specs=pl.BlockSpec((tm, d), lambda i: (i, 0)),
        out_shape=jax.ShapeDtypeStruct((m, d), F32),
        scratch_shapes=[pltpu.VMEM((MEM_LEN, d), BF16),
                        pltpu.VMEM((MEM_LEN, d), BF16),
                        pltpu.VMEM((2, WEIGHT_STAGE_ROWS, d), F32),
                        pltpu.SemaphoreType.DMA((2,)),
                        pltpu.VMEM((tm, d), F32),
                        pltpu.VMEM((tm, d), BF16),
                        pltpu.VMEM((tm, d), BF16)],
        compiler_params=_params(1),
        name="xattn",
    )(h, mixed, norm_w.reshape(1, d), mem, mem_norm_w.reshape(1, d), w_out, w_q, w_o, w_kv)


def kernel(x, mem, ffn1_norm, ffn1_w_gu, ffn1_w_down, mix_norm, w_in, gm_v_norm, gm_w_s, gm_b_s, ssm_conv_w, ssm_conv_b, ssm_dt_bias, ssm_a_log, ssm_d, ssm_norm, w_out, xa_norm, mem_norm, xa_w_q, xa_w_kv, xa_w_o, ffn2_norm, ffn2_w_gu, ffn2_w_down, final_norm):
    pad_lanes = lambda a: jnp.pad(a, ((0, 0), (0, LANES - a.shape[1])))
    h = x[0]
    for i in range(ffn1_norm.shape[0]):
        h, xn = _ffn(h, ffn1_norm[i], ffn1_w_gu[i], ffn1_w_down[i], mix_norm[i], tail="next")

        proj, dt_raw, w_out_b, w_q_b, w_o_b = _in_proj(xn, jnp.swapaxes(w_in[i], 0, 1), ssm_conv_w[i], ssm_conv_b[i],
                                                       [w_out[i], xa_w_q[i], xa_w_o[i]])
        mixed = _mixer(proj, dt_raw, gm_v_norm[i], gm_w_s[i], pad_lanes(gm_b_s[i].T),
                       pad_lanes(ssm_dt_bias[i][None])[0], pad_lanes(ssm_a_log[i][None])[0],
                       jnp.repeat(ssm_d[i], SSM_HEADDIM), ssm_norm[i])
        h = _xattn(h, mixed, xa_norm[i], mem[0], mem_norm[i], w_out_b, w_q_b, w_o_b, xa_w_kv[i])

        last = i == ffn1_norm.shape[0] - 1
        h = _ffn(h, ffn2_norm[i], ffn2_w_gu[i], ffn2_w_down[i], final_norm, tail="final" if last else "none")[0]
    return h[None]
```

```python
import functools

import jax
import jax.numpy as jnp
from jax import lax
from jax.experimental import pallas as pl
from jax.experimental.pallas import tpu as pltpu

D_MODEL = 2048
MEM_LEN = 256
GM_WIDTH = 2048
GM_GROUPS = 4
GM_GROUP_WIDTH = GM_WIDTH // GM_GROUPS
CHUNK = 128
SSM_WIDTH = 2048
SSM_HEADDIM = 64
SSM_HEADS = SSM_WIDTH // SSM_HEADDIM
SSM_GROUPS = 8
SSM_HEADS_PER_GROUP = SSM_HEADS // SSM_GROUPS
SSM_GROUP_WIDTH = SSM_HEADS_PER_GROUP * SSM_HEADDIM
SSM_STATE = 128
SSM_CONV = 4
SSM_BC_WIDTH = SSM_GROUPS * SSM_STATE
PROJ_MAIN = 2 * GM_WIDTH + SSM_WIDTH + SSM_WIDTH + 2 * SSM_BC_WIDTH
XA_HEADS = 4
XA_HEADDIM = D_MODEL // XA_HEADS
EPS = 1e-6

LANES = 128
HALO_ROWS = 8
ROW_BLOCK = 128
IN_PROJ_ROW_BLOCK = 256
IN_PROJ_SUBTILE = 512
SIDE_CAST_STEPS = 32
MIXER_CHUNKS = 2
VMEM_LIMIT = 62 * 1024 * 1024

BF16 = jnp.bfloat16
F32 = jnp.float32


def _params(n_axes):
    return pltpu.CompilerParams(dimension_semantics=("arbitrary",) * n_axes,
                                vmem_limit_bytes=VMEM_LIMIT)


def _rmsnorm(x, w):
    return x * lax.rsqrt(jnp.mean(x * x, axis=-1, keepdims=True) + EPS) * w


def _silu(x):
    half = 0.5 * x
    return half + half * jnp.tanh(half)


def _gelu_tanh(x):
    c = 0.7978845608028654
    return 0.5 * x * (1.0 + jnp.tanh(x * (c + (0.044715 * c) * (x * x))))


def _dot(a, b):
    return jnp.dot(a, b, preferred_element_type=F32)


def _dot_nt(a, b):
    return lax.dot_general(a, b, (((1,), (1,)), ((), ())), preferred_element_type=F32)


def _dot_tn(a, b):
    return lax.dot_general(a, b, (((0,), (0,)), ((), ())), preferred_element_type=F32)


def _ffn_kernel(x_ref, nw_ref, wg_ref, wu_ref, wd_ref, tw_ref, o_ref, *rest, tail):
    xn_ref = rest[-1]
    j = pl.program_id(1)
    n_row_blocks = x_ref.shape[0] // ROW_BLOCK

    def row_block(r):
        return pl.ds(pl.multiple_of(r * ROW_BLOCK, ROW_BLOCK), ROW_BLOCK)

    @pl.when(j == 0)
    def _():
        def body(r, _):
            rows = row_block(r)
            x = x_ref[rows, :]
            xn_ref[rows, :] = _rmsnorm(x, nw_ref[...]).astype(BF16)
            o_ref[rows, :] = x
        lax.fori_loop(0, n_row_blocks, body, None)

    xn = xn_ref[...]
    g = _dot(xn, wg_ref[...].astype(BF16))
    u = _dot(xn, wu_ref[...].astype(BF16))
    o_ref[...] += _dot((_silu(g) * u).astype(BF16), (0.5 * wd_ref[...]).astype(BF16))

    if tail != "none":
        @pl.when(j == pl.num_programs(1) - 1)
        def _():
            def body(r, _):
                rows = row_block(r)
                hn = _rmsnorm(o_ref[rows, :], tw_ref[...])
                if tail == "final":
                    o_ref[rows, :] = hn
                else:
                    rest[0][rows, :] = hn.astype(BF16)
            lax.fori_loop(0, n_row_blocks, body, None)


def _ffn(x, norm_w, w_gu, w_down, tail_w, *, tail, tm=1024, tf=256):
    m, d = x.shape
    f = w_down.shape[0]
    nf = f // tf
    row_tile = pl.BlockSpec((tm, d), lambda i, j: (i, 0))
    out_specs, out_shape = [row_tile], [jax.ShapeDtypeStruct((m, d), F32)]
    if tail == "next":
        out_specs.append(row_tile)
        out_shape.append(jax.ShapeDtypeStruct((m, d), BF16))
    return pl.pallas_call(
        functools.partial(_ffn_kernel, tail=tail),
        grid=(m // tm, nf),
        in_specs=[
            row_tile,
            pl.BlockSpec((1, d), lambda i, j: (0, 0)),
            pl.BlockSpec((d, tf), lambda i, j: (0, j)),
            pl.BlockSpec((d, tf), lambda i, j: (0, j + nf)),
            pl.BlockSpec((tf, d), lambda i, j: (j, 0)),
            pl.BlockSpec((1, d), lambda i, j: (0, 0)),
        ],
        out_specs=out_specs,
        out_shape=out_shape,
        scratch_shapes=[pltpu.VMEM((tm, d), BF16)],
        compiler_params=_params(2),
        name="ffn",
    )(x, norm_w.reshape(1, d), w_gu, w_gu, w_down, tail_w.reshape(1, d))


def _in_proj_kernel(xn_ref, wt_ref, wdt_ref, cw_ref, cb_ref, *rest, gelu_tiles, silu_tiles, n_side):
    side_in, (o_ref, dt_ref) = rest[:n_side], rest[n_side:n_side + 2]
    side_out = rest[n_side + 2:2 * n_side + 2]
    raw_even_ref, raw_odd_ref, halo_ref = rest[2 * n_side + 2:]
    for src_ref, dst_ref in zip(side_in, side_out):
        dst_ref[...] = src_ref[...].astype(BF16)

    i = pl.program_id(0)
    j = pl.program_id(1)
    conv_first = gelu_tiles + silu_tiles
    raw_refs = (raw_even_ref, raw_odd_ref)

    @pl.when(j == 0)
    def _():
        dt_ref[...] = _dot_nt(xn_ref[...], wdt_ref[...].astype(BF16))

    tm = o_ref.shape[0]

    subtiles = range(0, o_ref.shape[1], IN_PROJ_SUBTILE)

    def project(c0, consume):
        w = wt_ref[c0:c0 + IN_PROJ_SUBTILE, :].astype(BF16)
        for r0 in range(0, tm, IN_PROJ_ROW_BLOCK):
            consume(r0, _dot_nt(xn_ref[r0:r0 + IN_PROJ_ROW_BLOCK, :], w))

    def store_with(activation, c0):
        def consume(r0, acc):
            o_ref[r0:r0 + IN_PROJ_ROW_BLOCK, c0:c0 + IN_PROJ_SUBTILE] = activation(acc)
        return consume

    @pl.when(j < gelu_tiles)
    def _():
        for c0 in subtiles:
            project(c0, store_with(_gelu_tanh, c0))

    @pl.when((j >= gelu_tiles) & (j < gelu_tiles + silu_tiles))
    def _():
        for c0 in subtiles:
            project(c0, store_with(_silu, c0))

    @pl.when(j >= conv_first)
    def _():
        tail = slice(IN_PROJ_ROW_BLOCK, IN_PROJ_ROW_BLOCK + HALO_ROWS)
        n_blocks = tm // IN_PROJ_ROW_BLOCK
        for c0 in subtiles:
            cols = slice(c0, c0 + IN_PROJ_SUBTILE)
            c = (j - conv_first) * len(subtiles) + c0 // IN_PROJ_SUBTILE

            @pl.when(i == 0)
            def _():
                raw_refs[0][0:HALO_ROWS, :] = jnp.zeros((HALO_ROWS, IN_PROJ_SUBTILE), F32)

            @pl.when(i > 0)
            def _():
                raw_refs[0][0:HALO_ROWS, :] = halo_ref[c]

            def conv_silu(r0, acc):
                block = r0 // IN_PROJ_ROW_BLOCK
                raw, previous = raw_refs[block % 2], raw_refs[(block - 1) % 2]
                if block > 0:
                    raw[0:HALO_ROWS, :] = previous[tail, :]
                raw[HALO_ROWS:, :] = acc
                out = cb_ref[:, cols]
                for tap in range(SSM_CONV):
                    off = HALO_ROWS - (SSM_CONV - 1) + tap
                    out = out + raw[off:off + IN_PROJ_ROW_BLOCK, :] * cw_ref[tap:tap + 1, cols]
                o_ref[r0:r0 + IN_PROJ_ROW_BLOCK, cols] = _silu(out)

            project(c0, conv_silu)
            halo_ref[c] = raw_refs[(n_blocks - 1) % 2][tail, :]


def _in_proj(xn, w_in_t, conv_w, conv_b, side_weights, *, tm=2048, tn=1024):
    m, k = xn.shape
    n_col = PROJ_MAIN // tn
    gelu_tiles, silu_tiles = 2 * GM_WIDTH // tn, SSM_WIDTH // tn
    conv_subtiles = (SSM_WIDTH + 2 * SSM_BC_WIDTH) // IN_PROJ_SUBTILE
    kernel_fn = functools.partial(_in_proj_kernel, gelu_tiles=gelu_tiles, silu_tiles=silu_tiles,
                                  n_side=len(side_weights))
    assert SIDE_CAST_STEPS <= (m // tm) * n_col
    side_specs = [pl.BlockSpec((w.shape[0] // SIDE_CAST_STEPS, w.shape[1]),
                               lambda i, j: (jnp.minimum(i * n_col + j, SIDE_CAST_STEPS - 1), 0))
                  for w in side_weights]
    conv_tile = lambda i, j: (0, jnp.maximum(j - (gelu_tiles + silu_tiles), 0))
    w_dt_t = jnp.pad(w_in_t[PROJ_MAIN:], ((0, LANES - (w_in_t.shape[0] - PROJ_MAIN)), (0, 0)))
    return pl.pallas_call(
        kernel_fn,
        grid=(m // tm, n_col),
        in_specs=[
            pl.BlockSpec((tm, k), lambda i, j: (i, 0)),
            pl.BlockSpec((tn, k), lambda i, j: (j, 0)),
            pl.BlockSpec((LANES, k), lambda i, j: (0, 0), pipeline_mode=pl.Buffered(1)),
            pl.BlockSpec((SSM_CONV, tn), conv_tile),
            pl.BlockSpec((1, tn), conv_tile),
            *side_specs,
        ],
        out_specs=[pl.BlockSpec((tm, tn), lambda i, j: (i, j)),
                   pl.BlockSpec((tm, LANES), lambda i, j: (i, 0)),
                   *side_specs],
        out_shape=[jax.ShapeDtypeStruct((m, PROJ_MAIN), F32), jax.ShapeDtypeStruct((m, LANES), F32),
                   *[jax.ShapeDtypeStruct(w.shape, BF16) for w in side_weights]],
        scratch_shapes=[pltpu.VMEM((HALO_ROWS + IN_PROJ_ROW_BLOCK, IN_PROJ_SUBTILE), F32),
                        pltpu.VMEM((HALO_ROWS + IN_PROJ_ROW_BLOCK, IN_PROJ_SUBTILE), F32),
                        pltpu.VMEM((conv_subtiles, HALO_ROWS, IN_PROJ_SUBTILE), F32)],
        compiler_params=_params(2),
        name="in_proj",
    )(xn, w_in_t, w_dt_t, conv_w, conv_b.reshape(1, -1), *side_weights)


def _sgu_init(ws_ref, wsc_ref):
    t_idx = lax.broadcasted_iota(jnp.int32, (CHUNK, CHUNK), 0)
    s_idx = lax.broadcasted_iota(jnp.int32, (CHUNK, CHUNK), 1)
    for g in range(GM_GROUPS):
        wsc_ref[g] = jnp.where(s_idx <= t_idx, ws_ref[g], 0.0).astype(BF16)


def _sgu_mix(gu_ref, gv_ref, vw_ref, wsc_ref, bs_ref, o_ref):
    vn = _rmsnorm(gv_ref[...], vw_ref[...]).astype(BF16)
    bias = bs_ref[...]
    for g in range(GM_GROUPS):
        cols = slice(g * GM_GROUP_WIDTH, (g + 1) * GM_GROUP_WIDTH)
        mixed = _dot(wsc_ref[g], vn[:, cols]) + bias[:, g:g + 1]
        o_ref[:, cols] = (gu_ref[:, cols] * mixed).astype(o_ref.dtype)


def _split3(v):
    hi = v.astype(BF16).astype(F32)
    rest = v - hi
    mid = rest.astype(BF16).astype(F32)
    return hi, mid, rest - mid


def _pack3(v):
    hi, mid, lo = _split3(v)
    lane = lax.broadcasted_iota(jnp.int32, v.shape, 1)
    packed = jnp.where(lane < SSM_HEADS, hi,
                       jnp.where(lane < 2 * SSM_HEADS, pltpu.roll(mid, SSM_HEADS, 1),
                                 jnp.where(lane < 3 * SSM_HEADS, pltpu.roll(lo, 2 * SSM_HEADS, 1), 0.0)))
    return packed.astype(BF16)


def _ssd_init(state_ref, expand_ref):
    state_ref[...] = jnp.zeros_like(state_ref)
    k_idx = lax.broadcasted_iota(jnp.int32, (LANES, SSM_WIDTH), 0)
    j_idx = lax.broadcasted_iota(jnp.int32, (LANES, SSM_WIDTH), 1)
    hit = (k_idx < 3 * SSM_HEADS) & (j_idx // SSM_HEADDIM == k_idx % SSM_HEADS)
    expand_ref[...] = hit.astype(F32).astype(BF16)


def _ssd_scan(gate_ref, xs_ref, bc_ref, dt_ref, dtb_ref, alog_ref, dskip_ref, nw_ref, o_ref,
              state_ref, y_ref, expand_ref):
    pre = dt_ref[...] + dtb_ref[...]
    dt = jnp.maximum(pre, 0.0) + jnp.log1p(jnp.exp(-jnp.abs(pre)))
    da = dt * (-jnp.exp(alog_ref[...]))
    row = lax.broadcasted_iota(jnp.int32, (CHUNK, CHUNK), 0)
    col = lax.broadcasted_iota(jnp.int32, (CHUNK, CHUNK), 1)
    causal = col <= row
    tri = causal.astype(F32).astype(BF16)
    da_hi, da_mid, da_lo = _split3(da)
    parts = _dot(tri, jnp.concatenate([da_hi, da_mid, da_lo], axis=1).astype(BF16))
    cum = parts[:, :LANES] + parts[:, LANES:2 * LANES] + parts[:, 2 * LANES:]
    cum_t = cum.T
    cum_last = cum[CHUNK - 1:CHUNK, :]

    packed = jnp.concatenate([_pack3(dt), _pack3(dt * jnp.exp(cum_last - cum)), _pack3(jnp.exp(cum))], axis=0)
    expanded = _dot(packed, expand_ref[...])
    dt_e = expanded[0:CHUNK]
    dt_decay_end_e = expanded[CHUNK:2 * CHUNK]
    exp_cum_e = expanded[2 * CHUNK:]
    chunk_decay_e = exp_cum_e[CHUNK - 1:CHUNK, :]

    xs = xs_ref[...]
    xdt = (xs * dt_e).astype(BF16)
    xdt_end = (xs * dt_decay_end_e).astype(BF16)
    lane_head = lax.broadcasted_iota(jnp.int32, (1, SSM_GROUP_WIDTH), 1) // SSM_HEADDIM
    head_lanes = [(lane_head == r).astype(F32).astype(BF16) for r in range(SSM_HEADS_PER_GROUP)]

    b_of = lambda g: bc_ref[:, g * SSM_STATE:(g + 1) * SSM_STATE].astype(BF16)
    c_of = lambda g: bc_ref[:, SSM_BC_WIDTH + g * SSM_STATE:SSM_BC_WIDTH + (g + 1) * SSM_STATE].astype(BF16)
    for g in range(SSM_GROUPS):
        gcols = slice(g * SSM_GROUP_WIDTH, (g + 1) * SSM_GROUP_WIDTH)
        b_g, c_g = b_of(g), c_of(g)
        if g % 2 == 0:
            pair = _dot_nt(jnp.concatenate([c_g, c_of(g + 1)], axis=0), jnp.concatenate([b_g, b_of(g + 1)], axis=0))
        half = (g % 2) * CHUNK
        scores = pair[half:half + CHUNK, half:half + CHUNK]
        xdt_g = xdt[:, gcols]
        y = _dot(c_g, state_ref[g].astype(BF16)) * exp_cum_e[:, gcols]
        decayed, stacked = [], []
        for r in range(SSM_HEADS_PER_GROUP):
            h = g * SSM_HEADS_PER_GROUP + r
            seg = cum[:, h:h + 1] - cum_t[h:h + 1, :]
            decay = jnp.exp(jnp.where(causal, seg, -jnp.inf))
            decayed.append((scores * decay).astype(BF16))
            stacked.append(xdt_g * head_lanes[r])
        y = y + _dot(jnp.concatenate(decayed, axis=1), jnp.concatenate(stacked, axis=0))
        state_ref[g] = state_ref[g] * chunk_decay_e[:, gcols] + _dot_tn(b_g, xdt_end[:, gcols])
        y_ref[:, gcols] = y

    y = y_ref[...] + dskip_ref[...] * xs
    y = y * gate_ref[...]
    o_ref[...] = _rmsnorm(y, nw_ref[...]).astype(o_ref.dtype)


def _mixer_kernel(gu_ref, gv_ref, vw_ref, ws_ref, bs_ref,
                  gate_ref, xs_ref, bc_ref, dt_ref, dtb_ref, alog_ref, dskip_ref, nw_ref,
                  o_ref,
                  state_ref, y_ref, expand_ref, wsc_ref):
    @pl.when(pl.program_id(0) == 0)
    def _():
        _ssd_init(state_ref, expand_ref)
        _sgu_init(ws_ref, wsc_ref)

    for c in range(MIXER_CHUNKS):
        rows = lambda ref: ref.at[c * CHUNK:(c + 1) * CHUNK, :]
        out = rows(o_ref)
        _sgu_mix(rows(gu_ref), rows(gv_ref), vw_ref, wsc_ref, bs_ref, out.at[:, 0:GM_WIDTH])
        _ssd_scan(rows(gate_ref), rows(xs_ref), rows(bc_ref), rows(dt_ref), dtb_ref, alog_ref, dskip_ref, nw_ref,
                  out.at[:, GM_WIDTH:], state_ref, y_ref, expand_ref)


def _mixer(proj, dt_raw, v_norm_w, w_s, b_s_t, dt_bias, a_log, d_skip_e, norm_w):
    m = proj.shape[0]
    row = lambda a: a.reshape(1, -1)
    c0 = 2 * GM_WIDTH // SSM_WIDTH
    full = lambda shape: pl.BlockSpec(shape, lambda i: (0,) * len(shape))
    chunk = lambda width, col: pl.BlockSpec((MIXER_CHUNKS * CHUNK, width), lambda i: (i, col))
    return pl.pallas_call(
        _mixer_kernel,
        grid=(m // (MIXER_CHUNKS * CHUNK),),
        in_specs=[
            chunk(GM_WIDTH, 0),
            chunk(GM_WIDTH, 1),
            full((1, GM_WIDTH)),
            full((GM_GROUPS, CHUNK, CHUNK)),
            full((CHUNK, LANES)),
            chunk(SSM_WIDTH, c0),
            chunk(SSM_WIDTH, c0 + 1),
            chunk(2 * SSM_BC_WIDTH, c0 + 2),
            chunk(LANES, 0),
            full((1, LANES)),
            full((1, LANES)),
            full((1, SSM_WIDTH)),
            full((1, SSM_WIDTH)),
        ],
        out_specs=chunk(GM_WIDTH + SSM_WIDTH, 0),
        out_shape=jax.ShapeDtypeStruct((m, GM_WIDTH + SSM_WIDTH), BF16),
        scratch_shapes=[pltpu.VMEM((SSM_GROUPS, SSM_STATE, SSM_GROUP_WIDTH), F32),
                        pltpu.VMEM((CHUNK, SSM_WIDTH), F32),
                        pltpu.VMEM((LANES, SSM_WIDTH), BF16),
                        pltpu.VMEM((GM_GROUPS, CHUNK, CHUNK), BF16)],
        compiler_params=_params(1),
        name="mixer",
    )(proj, proj, row(v_norm_w), w_s, b_s_t,
      proj, proj, proj, dt_raw, row(dt_bias), row(a_log), row(d_skip_e), row(norm_w))


WEIGHT_STAGE_ROWS = 256


def _for_each_row_block(w_hbm, cols, stage_ref, sems, use):
    n_blocks = w_hbm.shape[0] // WEIGHT_STAGE_ROWS

    def block_rows(r):
        return pl.ds(pl.multiple_of(r * WEIGHT_STAGE_ROWS, WEIGHT_STAGE_ROWS), WEIGHT_STAGE_ROWS)

    def copy(r, slot):
        return pltpu.make_async_copy(w_hbm.at[block_rows(r), cols], stage_ref.at[slot], sems.at[slot])

    copy(0, 0).start()

    def body(r, _):
        slot = r % 2

        @pl.when(r + 1 < n_blocks)
        def _():
            copy(r + 1, 1 - slot).start()

        copy(r, slot).wait()
        use(block_rows(r), stage_ref.at[slot])

    lax.fori_loop(0, n_blocks, body, None)


def _project_memory(memn_ref, w_hbm, cols, out_ref, acc_ref, stage_ref, sems):
    acc_ref[...] = jnp.zeros_like(acc_ref)

    def use(rows, block_ref):
        acc_ref[...] += _dot(memn_ref[:, rows], block_ref[...].astype(BF16))

    _for_each_row_block(w_hbm, cols, stage_ref, sems, use)
    out_ref[...] = acc_ref[...].astype(BF16)


def _xattn_kernel(h_ref, mixed_ref, nw_ref, mem_ref, memw_ref, wout_ref, wq_ref, wo_ref, wkv_hbm,
                  o_ref,
                  k_ref, v_ref, stage_ref, sems, h2_ref, q_ref, att_ref):
    d = h_ref.shape[1]

    @pl.when(pl.program_id(0) == 0)
    def _():
        q_ref[...] = _rmsnorm(mem_ref[...], memw_ref[...]).astype(BF16)
        _project_memory(q_ref, wkv_hbm, pl.ds(0, d), k_ref, h2_ref, stage_ref, sems)
        _project_memory(q_ref, wkv_hbm, pl.ds(d, d), v_ref, h2_ref, stage_ref, sems)

    h2_ref[...] = h_ref[...] + _dot(mixed_ref[...], wout_ref[...])
    q_ref[...] = _dot(_rmsnorm(h2_ref[...], nw_ref[...]).astype(BF16), wq_ref[...]).astype(BF16)
    scale = XA_HEADDIM ** -0.5
    for h in range(XA_HEADS):
        cols = slice(h * XA_HEADDIM, (h + 1) * XA_HEADDIM)
        logits = _dot_nt(q_ref[:, cols], k_ref[:, cols]) * scale
        p = jnp.exp(logits - jnp.max(logits, axis=-1, keepdims=True))
        p = p / jnp.sum(p, axis=-1, keepdims=True)
        att_ref[:, cols] = _dot(p.astype(BF16), v_ref[:, cols]).astype(BF16)
    o_ref[...] = h2_ref[...] + _dot(att_ref[...], wo_ref[...])


def _xattn(h, mixed, norm_w, mem, mem_norm_w, w_out, w_q, w_o, w_kv):
    m, d = h.shape
    tm = MEM_LEN
    once = pl.Buffered(1)
    resident = lambda w: pl.BlockSpec(w.shape, lambda i: (0, 0), pipeline_mode=once)
    return pl.pallas_call(
        _xattn_kernel,
        grid=(m // tm,),
        in_specs=[
            pl.BlockSpec((tm, d), lambda i: (i, 0)),
            pl.BlockSpec((tm, mixed.shape[1]), lambda i: (i, 0)),
            pl.BlockSpec((1, d), lambda i: (0, 0)),
            pl.BlockSpec((MEM_LEN, d), lambda i: (0, 0), pipeline_mode=once),
            pl.BlockSpec((1, d), lambda i: (0, 0)),
            resident(w_out), resident(w_q), resident(w_o),
            pl.BlockSpec(memory_space=pl.ANY),
        ],
        out_specs=pl.BlockSpec((tm, d), lambda i: (i, 0)),
        out_shape=jax.ShapeDtypeStruct((m, d), F32),
        scratch_shapes=[pltpu.VMEM((MEM_LEN, d), BF16),
                        pltpu.VMEM((MEM_LEN, d), BF16),
                        pltpu.VMEM((2, WEIGHT_STAGE_ROWS, d), F32),
                        pltpu.SemaphoreType.DMA((2,)),
                        pltpu.VMEM((tm, d), F32),
                        pltpu.VMEM((tm, d), BF16),
                        pltpu.VMEM((tm, d), BF16)],
        compiler_params=_params(1),
        name="xattn",
    )(h, mixed, norm_w.reshape(1, d), mem, mem_norm_w.reshape(1, d), w_out, w_q, w_o, w_kv)


def kernel(x, mem, ffn1_norm, ffn1_w_gu, ffn1_w_down, mix_norm, w_in, gm_v_norm, gm_w_s, gm_b_s, ssm_conv_w, ssm_conv_b, ssm_dt_bias, ssm_a_log, ssm_d, ssm_norm, w_out, xa_norm, mem_norm, xa_w_q, xa_w_kv, xa_w_o, ffn2_norm, ffn2_w_gu, ffn2_w_down, final_norm):
    pad_lanes = lambda a: jnp.pad(a, ((0, 0), (0, LANES - a.shape[1])))
    h = x[0]
    for i in range(ffn1_norm.shape[0]):
        h, xn = _ffn(h, ffn1_norm[i], ffn1_w_gu[i], ffn1_w_down[i], mix_norm[i], tail="next")

        proj, dt_raw, w_out_b, w_q_b, w_o_b = _in_proj(xn, jnp.swapaxes(w_in[i], 0, 1), ssm_conv_w[i], ssm_conv_b[i],
                                                       [w_out[i], xa_w_q[i], xa_w_o[i]])
        mixed = _mixer(proj, dt_raw, gm_v_norm[i], gm_w_s[i], pad_lanes(gm_b_s[i].T),
                       pad_lanes(ssm_dt_bias[i][None])[0], pad_lanes(ssm_a_log[i][None])[0],
                       jnp.repeat(ssm_d[i], SSM_HEADDIM), ssm_norm[i])
        h = _xattn(h, mixed, xa_norm[i], mem[0], mem_norm[i], w_out_b, w_q_b, w_o_b, xa_w_kv[i])

        last = i == ffn1_norm.shape[0] - 1
        h = _ffn(h, ffn2_norm[i], ffn2_w_gu[i], ffn2_w_down[i], final_norm, tail="final" if last else "none")[0]
    return h[None]
```

```python
import functools

import jax
import jax.numpy as jnp
from jax import lax
from jax.experimental import pallas as pl
from jax.experimental.pallas import tpu as pltpu

D_MODEL = 2048
MEM_LEN = 256
GM_WIDTH = 2048
GM_GROUPS = 4
GM_GROUP_WIDTH = GM_WIDTH // GM_GROUPS
CHUNK = 128
SSM_WIDTH = 2048
SSM_HEADDIM = 64
SSM_HEADS = SSM_WIDTH // SSM_HEADDIM
SSM_GROUPS = 8
SSM_HEADS_PER_GROUP = SSM_HEADS // SSM_GROUPS
SSM_GROUP_WIDTH = SSM_HEADS_PER_GROUP * SSM_HEADDIM
SSM_STATE = 128
SSM_CONV = 4
SSM_BC_WIDTH = SSM_GROUPS * SSM_STATE
PROJ_MAIN = 2 * GM_WIDTH + SSM_WIDTH + SSM_WIDTH + 2 * SSM_BC_WIDTH
XA_HEADS = 4
XA_HEADDIM = D_MODEL // XA_HEADS
EPS = 1e-6

LANES = 128
HALO_ROWS = 8
ROW_BLOCK = 512
IN_PROJ_ROW_BLOCK = 256
SIDE_CAST_STEPS = 64
MIXER_CHUNKS = 2
VMEM_LIMIT = 60 * 1024 * 1024

BF16 = jnp.bfloat16
F32 = jnp.float32


def _params(n_axes):
    return pltpu.CompilerParams(dimension_semantics=("arbitrary",) * n_axes,
                                vmem_limit_bytes=VMEM_LIMIT)


def _rmsnorm(x, w):
    return x * lax.rsqrt(jnp.mean(x * x, axis=-1, keepdims=True) + EPS) * w


def _silu(x):
    half = 0.5 * x
    return half + half * jnp.tanh(half)


def _gelu_tanh(x):
    c = 0.7978845608028654
    return 0.5 * x * (1.0 + jnp.tanh(x * (c + (0.044715 * c) * (x * x))))


def _dot(a, b):
    return jnp.dot(a, b, preferred_element_type=F32)


def _dot_nt(a, b):
    return lax.dot_general(a, b, (((1,), (1,)), ((), ())), preferred_element_type=F32)


def _dot_tn(a, b):
    return lax.dot_general(a, b, (((0,), (0,)), ((), ())), preferred_element_type=F32)


def _ffn_kernel(x_ref, nw_ref, wg_ref, wu_ref, wd_ref, tw_ref, o_ref, *rest, tail):
    xn_ref = rest[-1]
    j = pl.program_id(1)
    n_row_blocks = x_ref.shape[0] // ROW_BLOCK

    def row_block(r):
        return pl.ds(pl.multiple_of(r * ROW_BLOCK, ROW_BLOCK), ROW_BLOCK)

    @pl.when(j == 0)
    def _():
        def body(r, _):
            rows = row_block(r)
            x = x_ref[rows, :]
            xn_ref[rows, :] = _rmsnorm(x, nw_ref[...]).astype(BF16)
            o_ref[rows, :] = x
        lax.fori_loop(0, n_row_blocks, body, None)

    xn = xn_ref[...]
    g = _dot(xn, wg_ref[...].astype(BF16))
    u = _dot(xn, wu_ref[...].astype(BF16))
    o_ref[...] += _dot((_silu(g) * u).astype(BF16), (0.5 * wd_ref[...]).astype(BF16))

    if tail != "none":
        @pl.when(j == pl.num_programs(1) - 1)
        def _():
            def body(r, _):
                rows = row_block(r)
                hn = _rmsnorm(o_ref[rows, :], tw_ref[...])
                if tail == "final":
                    o_ref[rows, :] = hn
                else:
                    rest[0][rows, :] = hn.astype(BF16)
            lax.fori_loop(0, n_row_blocks, body, None)


def _ffn(x, norm_w, w_gu, w_down, tail_w, *, tail, tm=1024, tf=256):
    m, d = x.shape
    f = w_down.shape[0]
    nf = f // tf
    row_tile = pl.BlockSpec((tm, d), lambda i, j: (i, 0))
    out_specs, out_shape = [row_tile], [jax.ShapeDtypeStruct((m, d), F32)]
    if tail == "next":
        out_specs.append(row_tile)
        out_shape.append(jax.ShapeDtypeStruct((m, d), BF16))
    return pl.pallas_call(
        functools.partial(_ffn_kernel, tail=tail),
        grid=(m // tm, nf),
        in_specs=[
            row_tile,
            pl.BlockSpec((1, d), lambda i, j: (0, 0)),
            pl.BlockSpec((d, tf), lambda i, j: (0, j)),
            pl.BlockSpec((d, tf), lambda i, j: (0, j + nf)),
            pl.BlockSpec((tf, d), lambda i, j: (j, 0)),
            pl.BlockSpec((1, d), lambda i, j: (0, 0)),
        ],
        out_specs=out_specs,
        out_shape=out_shape,
        scratch_shapes=[pltpu.VMEM((tm, d), BF16)],
        compiler_params=_params(2),
        name="ffn",
    )(x, norm_w.reshape(1, d), w_gu, w_gu, w_down, tail_w.reshape(1, d))


def _in_proj_kernel(xn_ref, wt_ref, wdt_ref, cw_ref, cb_ref, *rest, gelu_tiles, silu_tiles, n_side):
    side_in, (o_ref, dt_ref) = rest[:n_side], rest[n_side:n_side + 2]
    side_out = rest[n_side + 2:2 * n_side + 2]
    raw_even_ref, raw_odd_ref, halo_ref = rest[2 * n_side + 2:]
    for src_ref, dst_ref in zip(side_in, side_out):
        dst_ref[...] = src_ref[...].astype(BF16)

    i = pl.program_id(0)
    j = pl.program_id(1)
    conv_first = gelu_tiles + silu_tiles
    raw_refs = (raw_even_ref, raw_odd_ref)

    @pl.when(j == 0)
    def _():
        dt_ref[...] = _dot_nt(xn_ref[...], wdt_ref[...].astype(BF16))

    tm = o_ref.shape[0]

    def project(consume):
        w = wt_ref[...].astype(BF16)
        for r0 in range(0, tm, IN_PROJ_ROW_BLOCK):
            consume(r0, _dot_nt(xn_ref[r0:r0 + IN_PROJ_ROW_BLOCK, :], w))

    def store_with(activation):
        def consume(r0, acc):
            o_ref[r0:r0 + IN_PROJ_ROW_BLOCK, :] = activation(acc)
        return consume

    @pl.when(j < gelu_tiles)
    def _():
        project(store_with(_gelu_tanh))

    @pl.when((j >= gelu_tiles) & (j < gelu_tiles + silu_tiles))
    def _():
        project(store_with(_silu))

    @pl.when(j >= conv_first)
    def _():
        c = j - conv_first
        tail = slice(IN_PROJ_ROW_BLOCK, IN_PROJ_ROW_BLOCK + HALO_ROWS)
        n_blocks = tm // IN_PROJ_ROW_BLOCK

        @pl.when(i == 0)
        def _():
            raw_refs[0][0:HALO_ROWS, :] = jnp.zeros((HALO_ROWS, o_ref.shape[1]), F32)

        @pl.when(i > 0)
        def _():
            raw_refs[0][0:HALO_ROWS, :] = halo_ref[c]

        def conv_silu(r0, acc):
            block = r0 // IN_PROJ_ROW_BLOCK
            raw, previous = raw_refs[block % 2], raw_refs[(block - 1) % 2]
            if block > 0:
                raw[0:HALO_ROWS, :] = previous[tail, :]
            raw[HALO_ROWS:, :] = acc
            out = cb_ref[...]
            for tap in range(SSM_CONV):
                off = HALO_ROWS - (SSM_CONV - 1) + tap
                out = out + raw[off:off + IN_PROJ_ROW_BLOCK, :] * cw_ref[tap:tap + 1, :]
            o_ref[r0:r0 + IN_PROJ_ROW_BLOCK, :] = _silu(out)

        project(conv_silu)
        halo_ref[c] = raw_refs[(n_blocks - 1) % 2][tail, :]


def _in_proj(xn, w_in_t, conv_w, conv_b, side_weights, *, tm=2048, tn=512):
    m, k = xn.shape
    n_col = PROJ_MAIN // tn
    gelu_tiles, silu_tiles = 2 * GM_WIDTH // tn, SSM_WIDTH // tn
    conv_tiles = (SSM_WIDTH + 2 * SSM_BC_WIDTH) // tn
    kernel_fn = functools.partial(_in_proj_kernel, gelu_tiles=gelu_tiles, silu_tiles=silu_tiles,
                                  n_side=len(side_weights))
    assert SIDE_CAST_STEPS <= (m // tm) * n_col
    side_specs = [pl.BlockSpec((w.shape[0] // SIDE_CAST_STEPS, w.shape[1]),
                               lambda i, j: (jnp.minimum(i * n_col + j, SIDE_CAST_STEPS - 1), 0))
                  for w in side_weights]
    conv_tile = lambda i, j: (0, jnp.maximum(j - (gelu_tiles + silu_tiles), 0))
    w_dt_t = jnp.pad(w_in_t[PROJ_MAIN:], ((0, LANES - (w_in_t.shape[0] - PROJ_MAIN)), (0, 0)))
    return pl.pallas_call(
        kernel_fn,
        grid=(m // tm, n_col),
        in_specs=[
            pl.BlockSpec((tm, k), lambda i, j: (i, 0)),
            pl.BlockSpec((tn, k), lambda i, j: (j, 0)),
            pl.BlockSpec((LANES, k), lambda i, j: (0, 0)),
            pl.BlockSpec((SSM_CONV, tn), conv_tile),
            pl.BlockSpec((1, tn), conv_tile),
            *side_specs,
        ],
        out_specs=[pl.BlockSpec((tm, tn), lambda i, j: (i, j)),
                   pl.BlockSpec((tm, LANES), lambda i, j: (i, 0)),
                   *side_specs],
        out_shape=[jax.ShapeDtypeStruct((m, PROJ_MAIN), F32), jax.ShapeDtypeStruct((m, LANES), F32),
                   *[jax.ShapeDtypeStruct(w.shape, BF16) for w in side_weights]],
        scratch_shapes=[pltpu.VMEM((HALO_ROWS + IN_PROJ_ROW_BLOCK, tn), F32),
                        pltpu.VMEM((HALO_ROWS + IN_PROJ_ROW_BLOCK, tn), F32),
                        pltpu.VMEM((conv_tiles, HALO_ROWS, tn), F32)],
        compiler_params=_params(2),
        name="in_proj",
    )(xn, w_in_t, w_dt_t, conv_w, conv_b.reshape(1, -1), *side_weights)


def _sgu_init(ws_ref, wsc_ref):
    t_idx = lax.broadcasted_iota(jnp.int32, (CHUNK, CHUNK), 0)
    s_idx = lax.broadcasted_iota(jnp.int32, (CHUNK, CHUNK), 1)
    for g in range(GM_GROUPS):
        wsc_ref[g] = jnp.where(s_idx <= t_idx, ws_ref[g], 0.0).astype(BF16)


def _sgu_mix(gu_ref, gv_ref, vw_ref, wsc_ref, bs_ref, o_ref):
    vn = _rmsnorm(gv_ref[...], vw_ref[...]).astype(BF16)
    bias = bs_ref[...]
    for g in range(GM_GROUPS):
        cols = slice(g * GM_GROUP_WIDTH, (g + 1) * GM_GROUP_WIDTH)
        mixed = _dot(wsc_ref[g], vn[:, cols]) + bias[:, g:g + 1]
        o_ref[:, cols] = (gu_ref[:, cols] * mixed).astype(o_ref.dtype)


def _split3(v):
    hi = v.astype(BF16).astype(F32)
    rest = v - hi
    mid = rest.astype(BF16).astype(F32)
    return hi, mid, rest - mid


def _pack3(v):
    hi, mid, lo = _split3(v)
    lane = lax.broadcasted_iota(jnp.int32, v.shape, 1)
    packed = jnp.where(lane < SSM_HEADS, hi,
                       jnp.where(lane < 2 * SSM_HEADS, pltpu.roll(mid, SSM_HEADS, 1),
                                 jnp.where(lane < 3 * SSM_HEADS, pltpu.roll(lo, 2 * SSM_HEADS, 1), 0.0)))
    return packed.astype(BF16)


def _ssd_init(state_ref, expand_ref):
    state_ref[...] = jnp.zeros_like(state_ref)
    k_idx = lax.broadcasted_iota(jnp.int32, (LANES, SSM_WIDTH), 0)
    j_idx = lax.broadcasted_iota(jnp.int32, (LANES, SSM_WIDTH), 1)
    hit = (k_idx < 3 * SSM_HEADS) & (j_idx // SSM_HEADDIM == k_idx % SSM_HEADS)
    expand_ref[...] = hit.astype(F32).astype(BF16)


def _ssd_scan(gate_ref, xs_ref, bc_ref, dt_ref, dtb_ref, alog_ref, dskip_ref, nw_ref, o_ref,
              state_ref, y_ref, expand_ref):
    pre = dt_ref[...] + dtb_ref[...]
    dt = jnp.maximum(pre, 0.0) + jnp.log1p(jnp.exp(-jnp.abs(pre)))
    da = dt * (-jnp.exp(alog_ref[...]))
    row = lax.broadcasted_iota(jnp.int32, (CHUNK, CHUNK), 0)
    col = lax.broadcasted_iota(jnp.int32, (CHUNK, CHUNK), 1)
    causal = col <= row
    tri = causal.astype(F32).astype(BF16)
    da_hi, da_mid, da_lo = _split3(da)
    parts = _dot(tri, jnp.concatenate([da_hi, da_mid, da_lo], axis=1).astype(BF16))
    cum = parts[:, :LANES] + parts[:, LANES:2 * LANES] + parts[:, 2 * LANES:]
    cum_t = cum.T
    cum_last = cum[CHUNK - 1:CHUNK, :]

    packed = jnp.concatenate([_pack3(dt), _pack3(dt * jnp.exp(cum_last - cum)), _pack3(jnp.exp(cum))], axis=0)
    expanded = _dot(packed, expand_ref[...])
    dt_e = expanded[0:CHUNK]
    dt_decay_end_e = expanded[CHUNK:2 * CHUNK]
    exp_cum_e = expanded[2 * CHUNK:]
    chunk_decay_e = exp_cum_e[CHUNK - 1:CHUNK, :]

    xs = xs_ref[...]
    xdt = (xs * dt_e).astype(BF16)
    xdt_end = (xs * dt_decay_end_e).astype(BF16)
    lane_head = lax.broadcasted_iota(jnp.int32, (1, SSM_GROUP_WIDTH), 1) // SSM_HEADDIM
    head_lanes = [(lane_head == r).astype(F32).astype(BF16) for r in range(SSM_HEADS_PER_GROUP)]

    b_of = lambda g: bc_ref[:, g * SSM_STATE:(g + 1) * SSM_STATE].astype(BF16)
    c_of = lambda g: bc_ref[:, SSM_BC_WIDTH + g * SSM_STATE:SSM_BC_WIDTH + (g + 1) * SSM_STATE].astype(BF16)
    for g in range(SSM_GROUPS):
        gcols = slice(g * SSM_GROUP_WIDTH, (g + 1) * SSM_GROUP_WIDTH)
        b_g, c_g = b_of(g), c_of(g)
        if g % 2 == 0:
            pair = _dot_nt(jnp.concatenate([c_g, c_of(g + 1)], axis=0), jnp.concatenate([b_g, b_of(g + 1)], axis=0))
        half = (g % 2) * CHUNK
        scores = pair[half:half + CHUNK, half:half + CHUNK]
        xdt_g = xdt[:, gcols]
        y = _dot(c_g, state_ref[g].astype(BF16)) * exp_cum_e[:, gcols]
        decayed, stacked = [], []
        for r in range(SSM_HEADS_PER_GROUP):
            h = g * SSM_HEADS_PER_GROUP + r
            seg = cum[:, h:h + 1] - cum_t[h:h + 1, :]
            decay = jnp.exp(jnp.where(causal, seg, -jnp.inf))
            decayed.append((scores * decay).astype(BF16))
            stacked.append(xdt_g * head_lanes[r])
        y = y + _dot(jnp.concatenate(decayed, axis=1), jnp.concatenate(stacked, axis=0))
        state_ref[g] = state_ref[g] * chunk_decay_e[:, gcols] + _dot_tn(b_g, xdt_end[:, gcols])
        y_ref[:, gcols] = y

    y = y_ref[...] + dskip_ref[...] * xs
    y = y * gate_ref[...]
    o_ref[...] = _rmsnorm(y, nw_ref[...]).astype(o_ref.dtype)


def _mixer_kernel(gu_ref, gv_ref, vw_ref, ws_ref, bs_ref,
                  gate_ref, xs_ref, bc_ref, dt_ref, dtb_ref, alog_ref, dskip_ref, nw_ref,
                  o_ref,
                  state_ref, y_ref, expand_ref, wsc_ref):
    @pl.when(pl.program_id(0) == 0)
    def _():
        _ssd_init(state_ref, expand_ref)
        _sgu_init(ws_ref, wsc_ref)

    for c in range(MIXER_CHUNKS):
        rows = lambda ref: ref.at[c * CHUNK:(c + 1) * CHUNK, :]
        out = rows(o_ref)
        _sgu_mix(rows(gu_ref), rows(gv_ref), vw_ref, wsc_ref, bs_ref, out.at[:, 0:GM_WIDTH])
        _ssd_scan(rows(gate_ref), rows(xs_ref), rows(bc_ref), rows(dt_ref), dtb_ref, alog_ref, dskip_ref, nw_ref,
                  out.at[:, GM_WIDTH:], state_ref, y_ref, expand_ref)


def _mixer(proj, dt_raw, v_norm_w, w_s, b_s_t, dt_bias, a_log, d_skip_e, norm_w):
    m = proj.shape[0]
    row = lambda a: a.reshape(1, -1)
    c0 = 2 * GM_WIDTH // SSM_WIDTH
    full = lambda shape: pl.BlockSpec(shape, lambda i: (0,) * len(shape))
    chunk = lambda width, col: pl.BlockSpec((MIXER_CHUNKS * CHUNK, width), lambda i: (i, col))
    return pl.pallas_call(
        _mixer_kernel,
        grid=(m // (MIXER_CHUNKS * CHUNK),),
        in_specs=[
            chunk(GM_WIDTH, 0),
            chunk(GM_WIDTH, 1),
            full((1, GM_WIDTH)),
            full((GM_GROUPS, CHUNK, CHUNK)),
            full((CHUNK, LANES)),
            chunk(SSM_WIDTH, c0),
            chunk(SSM_WIDTH, c0 + 1),
            chunk(2 * SSM_BC_WIDTH, c0 + 2),
            chunk(LANES, 0),
            full((1, LANES)),
            full((1, LANES)),
            full((1, SSM_WIDTH)),
            full((1, SSM_WIDTH)),
        ],
        out_specs=chunk(GM_WIDTH + SSM_WIDTH, 0),
        out_shape=jax.ShapeDtypeStruct((m, GM_WIDTH + SSM_WIDTH), BF16),
        scratch_shapes=[pltpu.VMEM((SSM_GROUPS, SSM_STATE, SSM_GROUP_WIDTH), F32),
                        pltpu.VMEM((CHUNK, SSM_WIDTH), F32),
                        pltpu.VMEM((LANES, SSM_WIDTH), BF16),
                        pltpu.VMEM((GM_GROUPS, CHUNK, CHUNK), BF16)],
        compiler_params=_params(1),
        name="mixer",
    )(proj, proj, row(v_norm_w), w_s, b_s_t,
      proj, proj, proj, dt_raw, row(dt_bias), row(a_log), row(d_skip_e), row(norm_w))


WEIGHT_STAGE_ROWS = 256


def _for_each_row_block(w_hbm, cols, stage_ref, sems, use):
    n_blocks = w_hbm.shape[0] // WEIGHT_STAGE_ROWS

    def block_rows(r):
        return pl.ds(pl.multiple_of(r * WEIGHT_STAGE_ROWS, WEIGHT_STAGE_ROWS), WEIGHT_STAGE_ROWS)

    def copy(r, slot):
        return pltpu.make_async_copy(w_hbm.at[block_rows(r), cols], stage_ref.at[slot], sems.at[slot])

    copy(0, 0).start()

    def body(r, _):
        slot = r % 2

        @pl.when(r + 1 < n_blocks)
        def _():
            copy(r + 1, 1 - slot).start()

        copy(r, slot).wait()
        use(block_rows(r), stage_ref.at[slot])

    lax.fori_loop(0, n_blocks, body, None)


def _project_memory(memn_ref, w_hbm, cols, out_ref, acc_ref, stage_ref, sems):
    acc_ref[...] = jnp.zeros_like(acc_ref)

    def use(rows, block_ref):
        acc_ref[...] += _dot(memn_ref[:, rows], block_ref[...].astype(BF16))

    _for_each_row_block(w_hbm, cols, stage_ref, sems, use)
    out_ref[...] = acc_ref[...].astype(BF16)


def _xattn_kernel(h_ref, mixed_ref, nw_ref, mem_ref, memw_ref, wout_ref, wq_ref, wo_ref, wkv_hbm,
                  o_ref,
                  k_ref, v_ref, stage_ref, sems, h2_ref, q_ref, att_ref):
    d = h_ref.shape[1]

    @pl.when(pl.program_id(0) == 0)
    def _():
        q_ref[...] = _rmsnorm(mem_ref[...], memw_ref[...]).astype(BF16)
        _project_memory(q_ref, wkv_hbm, pl.ds(0, d), k_ref, h2_ref, stage_ref, sems)
        _project_memory(q_ref, wkv_hbm, pl.ds(d, d), v_ref, h2_ref, stage_ref, sems)

    h2_ref[...] = h_ref[...] + _dot(mixed_ref[...], wout_ref[...])
    q_ref[...] = _dot(_rmsnorm(h2_ref[...], nw_ref[...]).astype(BF16), wq_ref[...]).astype(BF16)
    scale = XA_HEADDIM ** -0.5
    for h in range(XA_HEADS):
        cols = slice(h * XA_HEADDIM, (h + 1) * XA_HEADDIM)
        logits = _dot_nt(q_ref[:, cols], k_ref[:, cols]) * scale
        p = jnp.exp(logits - jnp.max(logits, axis=-1, keepdims=True))
        p = p / jnp.sum(p, axis=-1, keepdims=True)
        att_ref[:, cols] = _dot(p.astype(BF16), v_ref[:, cols]).astype(BF16)
    o_ref[...] = h2_ref[...] + _dot(att_ref[...], wo_ref[...])


def _xattn(h, mixed, norm_w, mem, mem_norm_w, w_out, w_q, w_o, w_kv):
    m, d = h.shape
    tm = MEM_LEN
    once = pl.Buffered(1)
    resident = lambda w: pl.BlockSpec(w.shape, lambda i: (0, 0), pipeline_mode=once)
    return pl.pallas_call(
        _xattn_kernel,
        grid=(m // tm,),
        in_specs=[
            pl.BlockSpec((tm, d), lambda i: (i, 0)),
            pl.BlockSpec((tm, mixed.shape[1]), lambda i: (i, 0)),
            pl.BlockSpec((1, d), lambda i: (0, 0)),
            pl.BlockSpec((MEM_LEN, d), lambda i: (0, 0), pipeline_mode=once),
            pl.BlockSpec((1, d), lambda i: (0, 0)),
            resident(w_out), resident(w_q), resident(w_o),
            pl.BlockSpec(memory_space=pl.ANY),
        ],
        out_specs=pl.BlockSpec((tm, d), lambda i: (i, 0)),
        out_shape=jax.ShapeDtypeStruct((m, d), F32),
        scratch_shapes=[pltpu.VMEM((MEM_LEN, d), BF16),
                        pltpu.VMEM((MEM_LEN, d), BF16),
                        pltpu.VMEM((2, WEIGHT_STAGE_ROWS, d), F32),
                        pltpu.SemaphoreType.DMA((2,)),
                        pltpu.VMEM((tm, d), F32),
                        pltpu.VMEM((tm, d), BF16),
                        pltpu.VMEM((tm, d), BF16)],
        compiler_params=_params(1),
        name="xattn",
    )(h, mixed, norm_w.reshape(1, d), mem, mem_norm_w.reshape(1, d), w_out, w_q, w_o, w_kv)


def kernel(x, mem, ffn1_norm, ffn1_w_gu, ffn1_w_down, mix_norm, w_in, gm_v_norm, gm_w_s, gm_b_s, ssm_conv_w, ssm_conv_b, ssm_dt_bias, ssm_a_log, ssm_d, ssm_norm, w_out, xa_norm, mem_norm, xa_w_q, xa_w_kv, xa_w_o, ffn2_norm, ffn2_w_gu, ffn2_w_down, final_norm):
    pad_lanes = lambda a: jnp.pad(a, ((0, 0), (0, LANES - a.shape[1])))
    h = x[0]
    for i in range(ffn1_norm.shape[0]):
        h, xn = _ffn(h, ffn1_norm[i], ffn1_w_gu[i], ffn1_w_down[i], mix_norm[i], tail="next")

        proj, dt_raw, w_out_b, w_q_b, w_o_b = _in_proj(xn, jnp.swapaxes(w_in[i], 0, 1), ssm_conv_w[i], ssm_conv_b[i],
                                                       [w_out[i], xa_w_q[i], xa_w_o[i]])
        mixed = _mixer(proj, dt_raw, gm_v_norm[i], gm_w_s[i], pad_lanes(gm_b_s[i].T),
                       pad_lanes(ssm_dt_bias[i][None])[0], pad_lanes(ssm_a_log[i][None])[0],
                       jnp.repeat(ssm_d[i], SSM_HEADDIM), ssm_norm[i])
        h = _xattn(h, mixed, xa_norm[i], mem[0], mem_norm[i], w_out_b, w_q_b, w_o_b, xa_w_kv[i])

        last = i == ffn1_norm.shape[0] - 1
        h = _ffn(h, ffn2_norm[i], ffn2_w_gu[i], ffn2_w_down[i], final_norm, tail="final" if last else "none")[0]
    return h[None]
```

```python
import functools

import jax
import jax.numpy as jnp
from jax import lax
from jax.experimental import pallas as pl
from jax.experimental.pallas import tpu as pltpu

D_MODEL = 2048
MEM_LEN = 256
GM_WIDTH = 2048
GM_GROUPS = 4
GM_GROUP_WIDTH = GM_WIDTH // GM_GROUPS
CHUNK = 128
SSM_WIDTH = 2048
SSM_HEADDIM = 64
SSM_HEADS = SSM_WIDTH // SSM_HEADDIM
SSM_GROUPS = 8
SSM_HEADS_PER_GROUP = SSM_HEADS // SSM_GROUPS
SSM_GROUP_WIDTH = SSM_HEADS_PER_GROUP * SSM_HEADDIM
SSM_STATE = 128
SSM_CONV = 4
SSM_BC_WIDTH = SSM_GROUPS * SSM_STATE
PROJ_MAIN = 2 * GM_WIDTH + SSM_WIDTH + SSM_WIDTH + 2 * SSM_BC_WIDTH
XA_HEADS = 4
XA_HEADDIM = D_MODEL // XA_HEADS
EPS = 1e-6

LANES = 128
HALO_ROWS = 8
ROW_BLOCK = 128
IN_PROJ_ROW_BLOCK = 256
SIDE_CAST_STEPS = 64
MIXER_CHUNKS = 2
VMEM_LIMIT = 60 * 1024 * 1024

BF16 = jnp.bfloat16
F32 = jnp.float32


def _params(n_axes):
    return pltpu.CompilerParams(dimension_semantics=("arbitrary",) * n_axes,
                                vmem_limit_bytes=VMEM_LIMIT)


def _rmsnorm(x, w):
    return x * lax.rsqrt(jnp.mean(x * x, axis=-1, keepdims=True) + EPS) * w


def _silu(x):
    half = 0.5 * x
    return half + half * jnp.tanh(half)


def _gelu_tanh(x):
    c = 0.7978845608028654
    return 0.5 * x * (1.0 + jnp.tanh(x * (c + (0.044715 * c) * (x * x))))


def _dot(a, b):
    return jnp.dot(a, b, preferred_element_type=F32)


def _dot_nt(a, b):
    return lax.dot_general(a, b, (((1,), (1,)), ((), ())), preferred_element_type=F32)


def _dot_tn(a, b):
    return lax.dot_general(a, b, (((0,), (0,)), ((), ())), preferred_element_type=F32)


def _ffn_kernel(x_ref, nw_ref, wg_ref, wu_ref, wd_ref, tw_ref, o_ref, *rest, tail, prefetch_x):
    i = pl.program_id(0)
    j = pl.program_id(1)
    tm = o_ref.shape[0]
    n_row_blocks = tm // ROW_BLOCK
    if prefetch_x:
        xn_ref, x_buf, sem = rest[-3:]

        def copy(tile):
            rows = pl.ds(pl.multiple_of(tile * tm, tm), tm)
            return pltpu.make_async_copy(x_ref.at[rows, :], x_buf, sem)

        pl.when((j == 0) & (i == 0))(lambda: copy(0).start())
        pl.when((j == 1) & (i + 1 < pl.num_programs(0)))(lambda: copy(i + 1).start())
    else:
        xn_ref, x_buf = rest[-1], x_ref

    def row_block(r):
        return pl.ds(pl.multiple_of(r * ROW_BLOCK, ROW_BLOCK), ROW_BLOCK)

    @pl.when(j == 0)
    def _():
        if prefetch_x:
            copy(i).wait()

        def body(r, _):
            rows = row_block(r)
            x = x_buf[rows, :]
            xn_ref[rows, :] = _rmsnorm(x, nw_ref[...]).astype(BF16)
            o_ref[rows, :] = x
        lax.fori_loop(0, n_row_blocks, body, None)

    xn = xn_ref[...]
    g = _dot(xn, wg_ref[...].astype(BF16))
    u = _dot(xn, wu_ref[...].astype(BF16))
    o_ref[...] += _dot((_silu(g) * u).astype(BF16), (0.5 * wd_ref[...]).astype(BF16))

    if tail != "none":
        @pl.when(j == pl.num_programs(1) - 1)
        def _():
            def body(r, _):
                rows = row_block(r)
                hn = _rmsnorm(o_ref[rows, :], tw_ref[...])
                if tail == "final":
                    o_ref[rows, :] = hn
                else:
                    rest[0][rows, :] = hn.astype(BF16)
            lax.fori_loop(0, n_row_blocks, body, None)


def _ffn(x, norm_w, w_gu, w_down, tail_w, *, tail, tm=1024):
    m, d = x.shape
    f = w_down.shape[0]
    row_tile = pl.BlockSpec((tm, d), lambda i, j: (i, 0))
    out_specs, out_shape = [row_tile], [jax.ShapeDtypeStruct((m, d), F32)]
    scratch = [pltpu.VMEM((tm, d), BF16)]
    if tail == "next":
        out_specs.append(row_tile)
        out_shape.append(jax.ShapeDtypeStruct((m, d), BF16))
    prefetch_x = tail != "next"
    tf = 512 if prefetch_x else 256
    nf = f // tf
    if prefetch_x:
        scratch += [pltpu.VMEM((tm, d), F32), pltpu.SemaphoreType.DMA(())]
    return pl.pallas_call(
        functools.partial(_ffn_kernel, tail=tail, prefetch_x=prefetch_x),
        grid=(m // tm, nf),
        in_specs=[
            pl.BlockSpec(memory_space=pl.ANY) if prefetch_x else row_tile,
            pl.BlockSpec((1, d), lambda i, j: (0, 0)),
            pl.BlockSpec((d, tf), lambda i, j: (0, j)),
            pl.BlockSpec((d, tf), lambda i, j: (0, j + nf)),
            pl.BlockSpec((tf, d), lambda i, j: (j, 0)),
            pl.BlockSpec((1, d), lambda i, j: (0, 0)),
        ],
        out_specs=out_specs,
        out_shape=out_shape,
        scratch_shapes=scratch,
        compiler_params=_params(2),
        name="ffn",
    )(x, norm_w.reshape(1, d), w_gu, w_gu, w_down, tail_w.reshape(1, d))


def _in_proj_kernel(xn_ref, wt_ref, wdt_ref, cw_ref, cb_ref, *rest, gelu_tiles, silu_tiles, n_side):
    side_in, (o_ref, dt_ref) = rest[:n_side], rest[n_side:n_side + 2]
    side_out = rest[n_side + 2:2 * n_side + 2]
    raw_even_ref, raw_odd_ref, halo_ref = rest[2 * n_side + 2:]
    for src_ref, dst_ref in zip(side_in, side_out):
        dst_ref[...] = src_ref[...].astype(BF16)

    i = pl.program_id(0)
    j = pl.program_id(1)
    conv_first = gelu_tiles + silu_tiles
    raw_refs = (raw_even_ref, raw_odd_ref)

    @pl.when(j == 0)
    def _():
        dt_ref[...] = _dot_nt(xn_ref[...], wdt_ref[...].astype(BF16))

    tm = o_ref.shape[0]

    def project(consume):
        w = wt_ref[...].astype(BF16)
        for r0 in range(0, tm, IN_PROJ_ROW_BLOCK):
            consume(r0, _dot_nt(xn_ref[r0:r0 + IN_PROJ_ROW_BLOCK, :], w))

    def store_with(activation):
        def consume(r0, acc):
            o_ref[r0:r0 + IN_PROJ_ROW_BLOCK, :] = activation(acc)
        return consume

    @pl.when(j < gelu_tiles)
    def _():
        project(store_with(_gelu_tanh))

    @pl.when((j >= gelu_tiles) & (j < gelu_tiles + silu_tiles))
    def _():
        project(store_with(_silu))

    @pl.when(j >= conv_first)
    def _():
        c = j - conv_first
        tail = slice(IN_PROJ_ROW_BLOCK, IN_PROJ_ROW_BLOCK + HALO_ROWS)
        n_blocks = tm // IN_PROJ_ROW_BLOCK

        @pl.when(i == 0)
        def _():
            raw_refs[0][0:HALO_ROWS, :] = jnp.zeros((HALO_ROWS, o_ref.shape[1]), F32)

        @pl.when(i > 0)
        def _():
            raw_refs[0][0:HALO_ROWS, :] = halo_ref[c]

        def conv_silu(r0, acc):
            block = r0 // IN_PROJ_ROW_BLOCK
            raw, previous = raw_refs[block % 2], raw_refs[(block - 1) % 2]
            if block > 0:
                raw[0:HALO_ROWS, :] = previous[tail, :]
            raw[HALO_ROWS:, :] = acc
            out = cb_ref[...]
            for tap in range(SSM_CONV):
                off = HALO_ROWS - (SSM_CONV - 1) + tap
                out = out + raw[off:off + IN_PROJ_ROW_BLOCK, :] * cw_ref[tap:tap + 1, :]
            o_ref[r0:r0 + IN_PROJ_ROW_BLOCK, :] = _silu(out)

        project(conv_silu)
        halo_ref[c] = raw_refs[(n_blocks - 1) % 2][tail, :]


def _in_proj(xn, w_in_t, conv_w, conv_b, side_weights, *, tm=2048, tn=512):
    m, k = xn.shape
    n_col = PROJ_MAIN // tn
    gelu_tiles, silu_tiles = 2 * GM_WIDTH // tn, SSM_WIDTH // tn
    conv_tiles = (SSM_WIDTH + 2 * SSM_BC_WIDTH) // tn
    kernel_fn = functools.partial(_in_proj_kernel, gelu_tiles=gelu_tiles, silu_tiles=silu_tiles,
                                  n_side=len(side_weights))
    assert SIDE_CAST_STEPS <= (m // tm) * n_col
    side_specs = [pl.BlockSpec((w.shape[0] // SIDE_CAST_STEPS, w.shape[1]),
                               lambda i, j: (jnp.minimum(i * n_col + j, SIDE_CAST_STEPS - 1), 0))
                  for w in side_weights]
    conv_tile = lambda i, j: (0, jnp.maximum(j - (gelu_tiles + silu_tiles), 0))
    w_dt_t = jnp.pad(w_in_t[PROJ_MAIN:], ((0, LANES - (w_in_t.shape[0] - PROJ_MAIN)), (0, 0)))
    return pl.pallas_call(
        kernel_fn,
        grid=(m // tm, n_col),
        in_specs=[
            pl.BlockSpec((tm, k), lambda i, j: (i, 0)),
            pl.BlockSpec((tn, k), lambda i, j: (j, 0)),
            pl.BlockSpec((LANES, k), lambda i, j: (0, 0)),
            pl.BlockSpec((SSM_CONV, tn), conv_tile),
            pl.BlockSpec((1, tn), conv_tile),
            *side_specs,
        ],
        out_specs=[pl.BlockSpec((tm, tn), lambda i, j: (i, j)),
                   pl.BlockSpec((tm, LANES), lambda i, j: (i, 0)),
                   *side_specs],
        out_shape=[jax.ShapeDtypeStruct((m, PROJ_MAIN), F32), jax.ShapeDtypeStruct((m, LANES), F32),
                   *[jax.ShapeDtypeStruct(w.shape, BF16) for w in side_weights]],
        scratch_shapes=[pltpu.VMEM((HALO_ROWS + IN_PROJ_ROW_BLOCK, tn), F32),
                        pltpu.VMEM((HALO_ROWS + IN_PROJ_ROW_BLOCK, tn), F32),
                        pltpu.VMEM((conv_tiles, HALO_ROWS, tn), F32)],
        compiler_params=_params(2),
        name="in_proj",
    )(xn, w_in_t, w_dt_t, conv_w, conv_b.reshape(1, -1), *side_weights)


def _sgu_init(ws_ref, wsc_ref):
    t_idx = lax.broadcasted_iota(jnp.int32, (CHUNK, CHUNK), 0)
    s_idx = lax.broadcasted_iota(jnp.int32, (CHUNK, CHUNK), 1)
    for g in range(GM_GROUPS):
        wsc_ref[g] = jnp.where(s_idx <= t_idx, ws_ref[g], 0.0).astype(BF16)


def _sgu_mix(gu_ref, gv_ref, vw_ref, wsc_ref, bs_ref, o_ref):
    vn = _rmsnorm(gv_ref[...], vw_ref[...]).astype(BF16)
    bias = bs_ref[...]
    for g in range(GM_GROUPS):
        cols = slice(g * GM_GROUP_WIDTH, (g + 1) * GM_GROUP_WIDTH)
        mixed = _dot(wsc_ref[g], vn[:, cols]) + bias[:, g:g + 1]
        o_ref[:, cols] = (gu_ref[:, cols] * mixed).astype(o_ref.dtype)


def _split3(v):
    hi = v.astype(BF16).astype(F32)
    rest = v - hi
    mid = rest.astype(BF16).astype(F32)
    return hi, mid, rest - mid


def _pack3(v):
    hi, mid, lo = _split3(v)
    lane = lax.broadcasted_iota(jnp.int32, v.shape, 1)
    packed = jnp.where(lane < SSM_HEADS, hi,
                       jnp.where(lane < 2 * SSM_HEADS, pltpu.roll(mid, SSM_HEADS, 1),
                                 jnp.where(lane < 3 * SSM_HEADS, pltpu.roll(lo, 2 * SSM_HEADS, 1), 0.0)))
    return packed.astype(BF16)


def _ssd_init(state_ref, expand_ref):
    state_ref[...] = jnp.zeros_like(state_ref)
    k_idx = lax.broadcasted_iota(jnp.int32, (LANES, SSM_WIDTH), 0)
    j_idx = lax.broadcasted_iota(jnp.int32, (LANES, SSM_WIDTH), 1)
    hit = (k_idx < 3 * SSM_HEADS) & (j_idx // SSM_HEADDIM == k_idx % SSM_HEADS)
    expand_ref[...] = hit.astype(F32).astype(BF16)


def _ssd_scan(gate_ref, xs_ref, bc_ref, dt_ref, dtb_ref, alog_ref, dskip_ref, nw_ref, o_ref,
              state_ref, y_ref, expand_ref):
    pre = dt_ref[...] + dtb_ref[...]
    dt = jnp.maximum(pre, 0.0) + jnp.log1p(jnp.exp(-jnp.abs(pre)))
    da = dt * (-jnp.exp(alog_ref[...]))
    row = lax.broadcasted_iota(jnp.int32, (CHUNK, CHUNK), 0)
    col = lax.broadcasted_iota(jnp.int32, (CHUNK, CHUNK), 1)
    causal = col <= row
    tri = causal.astype(F32).astype(BF16)
    da_hi, da_mid, da_lo = _split3(da)
    parts = _dot(tri, jnp.concatenate([da_hi, da_mid, da_lo], axis=1).astype(BF16))
    cum = parts[:, :LANES] + parts[:, LANES:2 * LANES] + parts[:, 2 * LANES:]
    cum_t = cum.T
    cum_last = cum[CHUNK - 1:CHUNK, :]

    packed = jnp.concatenate([_pack3(dt), _pack3(dt * jnp.exp(cum_last - cum)), _pack3(jnp.exp(cum))], axis=0)
    expanded = _dot(packed, expand_ref[...])
    dt_e = expanded[0:CHUNK]
    dt_decay_end_e = expanded[CHUNK:2 * CHUNK]
    exp_cum_e = expanded[2 * CHUNK:]
    chunk_decay_e = exp_cum_e[CHUNK - 1:CHUNK, :]

    xs = xs_ref[...]
    xdt = (xs * dt_e).astype(BF16)
    xdt_end = (xs * dt_decay_end_e).astype(BF16)
    lane_head = lax.broadcasted_iota(jnp.int32, (1, SSM_GROUP_WIDTH), 1) // SSM_HEADDIM
    head_lanes = [(lane_head == r).astype(F32).astype(BF16) for r in range(SSM_HEADS_PER_GROUP)]

    b_of = lambda g: bc_ref[:, g * SSM_STATE:(g + 1) * SSM_STATE].astype(BF16)
    c_of = lambda g: bc_ref[:, SSM_BC_WIDTH + g * SSM_STATE:SSM_BC_WIDTH + (g + 1) * SSM_STATE].astype(BF16)
    for g in range(SSM_GROUPS):
        gcols = slice(g * SSM_GROUP_WIDTH, (g + 1) * SSM_GROUP_WIDTH)
        b_g, c_g = b_of(g), c_of(g)
        if g % 2 == 0:
            pair = _dot_nt(jnp.concatenate([c_g, c_of(g + 1)], axis=0), jnp.concatenate([b_g, b_of(g + 1)], axis=0))
        half = (g % 2) * CHUNK
        scores = pair[half:half + CHUNK, half:half + CHUNK]
        xdt_g = xdt[:, gcols]
        y = _dot(c_g, state_ref[g].astype(BF16)) * exp_cum_e[:, gcols]
        decayed, stacked = [], []
        for r in range(SSM_HEADS_PER_GROUP):
            h = g * SSM_HEADS_PER_GROUP + r
            seg = cum[:, h:h + 1] - cum_t[h:h + 1, :]
            decay = jnp.exp(jnp.where(causal, seg, -jnp.inf))
            decayed.append((scores * decay).astype(BF16))
            stacked.append(xdt_g * head_lanes[r])
        y = y + _dot(jnp.concatenate(decayed, axis=1), jnp.concatenate(stacked, axis=0))
        state_ref[g] = state_ref[g] * chunk_decay_e[:, gcols] + _dot_tn(b_g, xdt_end[:, gcols])
        y_ref[:, gcols] = y

    y = y_ref[...] + dskip_ref[...] * xs
    y = y * gate_ref[...]
    o_ref[...] = _rmsnorm(y, nw_ref[...]).astype(o_ref.dtype)


def _mixer_kernel(gu_ref, gv_ref, vw_ref, ws_ref, bs_ref,
                  gate_ref, xs_ref, bc_ref, dt_ref, dtb_ref, alog_ref, dskip_ref, nw_ref,
                  o_ref,
                  state_ref, y_ref, expand_ref, wsc_ref):
    @pl.when(pl.program_id(0) == 0)
    def _():
        _ssd_init(state_ref, expand_ref)
        _sgu_init(ws_ref, wsc_ref)

    for c in range(MIXER_CHUNKS):
        rows = lambda ref: ref.at[c * CHUNK:(c + 1) * CHUNK, :]
        out = rows(o_ref)
        _sgu_mix(rows(gu_ref), rows(gv_ref), vw_ref, wsc_ref, bs_ref, out.at[:, 0:GM_WIDTH])
        _ssd_scan(rows(gate_ref), rows(xs_ref), rows(bc_ref), rows(dt_ref), dtb_ref, alog_ref, dskip_ref, nw_ref,
                  out.at[:, GM_WIDTH:], state_ref, y_ref, expand_ref)


def _mixer(proj, dt_raw, v_norm_w, w_s, b_s_t, dt_bias, a_log, d_skip_e, norm_w):
    m = proj.shape[0]
    row = lambda a: a.reshape(1, -1)
    c0 = 2 * GM_WIDTH // SSM_WIDTH
    full = lambda shape: pl.BlockSpec(shape, lambda i: (0,) * len(shape))
    chunk = lambda width, col: pl.BlockSpec((MIXER_CHUNKS * CHUNK, width), lambda i: (i, col))
    return pl.pallas_call(
        _mixer_kernel,
        grid=(m // (MIXER_CHUNKS * CHUNK),),
        in_specs=[
            chunk(GM_WIDTH, 0),
            chunk(GM_WIDTH, 1),
            full((1, GM_WIDTH)),
            full((GM_GROUPS, CHUNK, CHUNK)),
            full((CHUNK, LANES)),
            chunk(SSM_WIDTH, c0),
            chunk(SSM_WIDTH, c0 + 1),
            chunk(2 * SSM_BC_WIDTH, c0 + 2),
            chunk(LANES, 0),
            full((1, LANES)),
            full((1, LANES)),
            full((1, SSM_WIDTH)),
            full((1, SSM_WIDTH)),
        ],
        out_specs=chunk(GM_WIDTH + SSM_WIDTH, 0),
        out_shape=jax.ShapeDtypeStruct((m, GM_WIDTH + SSM_WIDTH), BF16),
        scratch_shapes=[pltpu.VMEM((SSM_GROUPS, SSM_STATE, SSM_GROUP_WIDTH), F32),
                        pltpu.VMEM((CHUNK, SSM_WIDTH), F32),
                        pltpu.VMEM((LANES, SSM_WIDTH), BF16),
                        pltpu.VMEM((GM_GROUPS, CHUNK, CHUNK), BF16)],
        compiler_params=_params(1),
        name="mixer",
    )(proj, proj, row(v_norm_w), w_s, b_s_t,
      proj, proj, proj, dt_raw, row(dt_bias), row(a_log), row(d_skip_e), row(norm_w))


WEIGHT_STAGE_ROWS = 256


def _for_each_row_block(w_hbm, cols, stage_ref, sems, use):
    n_blocks = w_hbm.shape[0] // WEIGHT_STAGE_ROWS

    def block_rows(r):
        return pl.ds(pl.multiple_of(r * WEIGHT_STAGE_ROWS, WEIGHT_STAGE_ROWS), WEIGHT_STAGE_ROWS)

    def copy(r, slot):
        return pltpu.make_async_copy(w_hbm.at[block_rows(r), cols], stage_ref.at[slot], sems.at[slot])

    copy(0, 0).start()

    def body(r, _):
        slot = r % 2

        @pl.when(r + 1 < n_blocks)
        def _():
            copy(r + 1, 1 - slot).start()

        copy(r, slot).wait()
        use(block_rows(r), stage_ref.at[slot])

    lax.fori_loop(0, n_blocks, body, None)


def _project_memory(memn_ref, w_hbm, cols, out_ref, acc_ref, stage_ref, sems):
    acc_ref[...] = jnp.zeros_like(acc_ref)

    def use(rows, block_ref):
        acc_ref[...] += _dot(memn_ref[:, rows], block_ref[...].astype(BF16))

    _for_each_row_block(w_hbm, cols, stage_ref, sems, use)
    out_ref[...] = acc_ref[...].astype(BF16)


def _xattn_kernel(h_ref, mixed_ref, nw_ref, mem_ref, memw_ref, wout_ref, wq_ref, wo_ref, wkv_hbm,
                  o_ref,
                  k_ref, v_ref, stage_ref, sems, h2_ref, q_ref, att_ref):
    d = h_ref.shape[1]

    @pl.when(pl.program_id(0) == 0)
    def _():
        q_ref[...] = _rmsnorm(mem_ref[...], memw_ref[...]).astype(BF16)
        _project_memory(q_ref, wkv_hbm, pl.ds(0, d), k_ref, h2_ref, stage_ref, sems)
        _project_memory(q_ref, wkv_hbm, pl.ds(d, d), v_ref, h2_ref, stage_ref, sems)

    h2_ref[...] = h_ref[...] + _dot(mixed_ref[...], wout_ref[...])
    q_ref[...] = _dot(_rmsnorm(h2_ref[...], nw_ref[...]).astype(BF16), wq_ref[...]).astype(BF16)
    scale = XA_HEADDIM ** -0.5
    for h in range(XA_HEADS):
        cols = slice(h * XA_HEADDIM, (h + 1) * XA_HEADDIM)
        logits = _dot_nt(q_ref[:, cols], k_ref[:, cols]) * scale
        p = jnp.exp(logits - jnp.max(logits, axis=-1, keepdims=True))
        p = p / jnp.sum(p, axis=-1, keepdims=True)
        att_ref[:, cols] = _dot(p.astype(BF16), v_ref[:, cols]).astype(BF16)
    o_ref[...] = h2_ref[...] + _dot(att_ref[...], wo_ref[...])


def _xattn(h, mixed, norm_w, mem, mem_norm_w, w_out, w_q, w_o, w_kv):
    m, d = h.shape
    tm = MEM_LEN
    once = pl.Buffered(1)
    resident = lambda w: pl.BlockSpec(w.shape, lambda i: (0, 0), pipeline_mode=once)
    return pl.pallas_call(
        _xattn_kernel,
        grid=(m // tm,),
        in_specs=[
            pl.BlockSpec((tm, d), lambda i: (i, 0)),
            pl.BlockSpec((tm, mixed.shape[1]), lambda i: (i, 0)),
            pl.BlockSpec((1, d), lambda i: (0, 0)),
            pl.BlockSpec((MEM_LEN, d), lambda i: (0, 0), pipeline_mode=once),
            pl.BlockSpec((1, d), lambda i: (0, 0)),
            resident(w_out), resident(w_q), resident(w_o),
            pl.BlockSpec(memory_space=pl.ANY),
        ],
        out_specs=pl.BlockSpec((tm, d), lambda i: (i, 0)),
        out_shape=jax.ShapeDtypeStruct((m, d), F32),
        scratch_shapes=[pltpu.VMEM((MEM_LEN, d), BF16),
                        pltpu.VMEM((MEM_LEN, d), BF16),
                        pltpu.VMEM((2, WEIGHT_STAGE_ROWS, d), F32),
                        pltpu.SemaphoreType.DMA((2,)),
                        pltpu.VMEM((tm, d), F32),
                        pltpu.VMEM((tm, d), BF16),
                        pltpu.VMEM((tm, d), BF16)],
        compiler_params=_params(1),
        name="xattn",
    )(h, mixed, norm_w.reshape(1, d), mem, mem_norm_w.reshape(1, d), w_out, w_q, w_o, w_kv)


def kernel(x, mem, ffn1_norm, ffn1_w_gu, ffn1_w_down, mix_norm, w_in, gm_v_norm, gm_w_s, gm_b_s, ssm_conv_w, ssm_conv_b, ssm_dt_bias, ssm_a_log, ssm_d, ssm_norm, w_out, xa_norm, mem_norm, xa_w_q, xa_w_kv, xa_w_o, ffn2_norm, ffn2_w_gu, ffn2_w_down, final_norm):
    pad_lanes = lambda a: jnp.pad(a, ((0, 0), (0, LANES - a.shape[1])))
    h = x[0]
    for i in range(ffn1_norm.shape[0]):
        h, xn = _ffn(h, ffn1_norm[i], ffn1_w_gu[i], ffn1_w_down[i], mix_norm[i], tail="next")

        proj, dt_raw, w_out_b, w_q_b, w_o_b = _in_proj(xn, jnp.swapaxes(w_in[i], 0, 1), ssm_conv_w[i], ssm_conv_b[i],
                                                       [w_out[i], xa_w_q[i], xa_w_o[i]])
        mixed = _mixer(proj, dt_raw, gm_v_norm[i], gm_w_s[i], pad_lanes(gm_b_s[i].T),
                       pad_lanes(ssm_dt_bias[i][None])[0], pad_lanes(ssm_a_log[i][None])[0],
                       jnp.repeat(ssm_d[i], SSM_HEADDIM), ssm_norm[i])
        h = _xattn(h, mixed, xa_norm[i], mem[0], mem_norm[i], w_out_b, w_q_b, w_o_b, xa_w_kv[i])

        last = i == ffn1_norm.shape[0] - 1
        h = _ffn(h, ffn2_norm[i], ffn2_w_gu[i], ffn2_w_down[i], final_norm, tail="final" if last else "none")[0]
    return h[None]
```

```python
import functools

import jax
import jax.numpy as jnp
from jax import lax
from jax.experimental import pallas as pl
from jax.experimental.pallas import tpu as pltpu

D_MODEL = 2048
MEM_LEN = 256
GM_WIDTH = 2048
GM_GROUPS = 4
GM_GROUP_WIDTH = GM_WIDTH // GM_GROUPS
CHUNK = 128
SSM_WIDTH = 2048
SSM_HEADDIM = 64
SSM_HEADS = SSM_WIDTH // SSM_HEADDIM
SSM_GROUPS = 8
SSM_HEADS_PER_GROUP = SSM_HEADS // SSM_GROUPS
SSM_GROUP_WIDTH = SSM_HEADS_PER_GROUP * SSM_HEADDIM
SSM_STATE = 128
SSM_CONV = 4
SSM_BC_WIDTH = SSM_GROUPS * SSM_STATE
PROJ_MAIN = 2 * GM_WIDTH + SSM_WIDTH + SSM_WIDTH + 2 * SSM_BC_WIDTH
XA_HEADS = 4
XA_HEADDIM = D_MODEL // XA_HEADS
EPS = 1e-6

LANES = 128
HALO_ROWS = 8
ROW_BLOCK = 128
IN_PROJ_ROW_BLOCK = 256
SIDE_CAST_STEPS = 64
MIXER_CHUNKS = 2
VMEM_LIMIT = 63 * 1024 * 1024

BF16 = jnp.bfloat16
F32 = jnp.float32


def _params(n_axes):
    return pltpu.CompilerParams(dimension_semantics=("arbitrary",) * n_axes,
                                vmem_limit_bytes=VMEM_LIMIT)


def _rmsnorm(x, w):
    return x * lax.rsqrt(jnp.mean(x * x, axis=-1, keepdims=True) + EPS) * w


def _silu(x):
    half = 0.5 * x
    return half + half * jnp.tanh(half)


def _gelu_tanh(x):
    c = 0.7978845608028654
    return 0.5 * x * (1.0 + jnp.tanh(x * (c + (0.044715 * c) * (x * x))))


def _dot(a, b):
    return jnp.dot(a, b, preferred_element_type=F32)


def _dot_nt(a, b):
    return lax.dot_general(a, b, (((1,), (1,)), ((), ())), preferred_element_type=F32)


def _dot_tn(a, b):
    return lax.dot_general(a, b, (((0,), (0,)), ((), ())), preferred_element_type=F32)


def _ffn_kernel(x_ref, nw_ref, wg_ref, wu_ref, wd_ref, tw_ref, o_ref, *rest, tail, prefetch_x):
    i = pl.program_id(0)
    j = pl.program_id(1)
    tm = o_ref.shape[0]
    n_row_blocks = tm // ROW_BLOCK
    if prefetch_x:
        xn_ref, x_buf, sem = rest[-3:]

        def copy(tile):
            rows = pl.ds(pl.multiple_of(tile * tm, tm), tm)
            return pltpu.make_async_copy(x_ref.at[rows, :], x_buf, sem)

        pl.when((j == 0) & (i == 0))(lambda: copy(0).start())
        pl.when((j == 1) & (i + 1 < pl.num_programs(0)))(lambda: copy(i + 1).start())
    else:
        xn_ref, x_buf = rest[-1], x_ref

    def row_block(r):
        return pl.ds(pl.multiple_of(r * ROW_BLOCK, ROW_BLOCK), ROW_BLOCK)

    @pl.when(j == 0)
    def _():
        if prefetch_x:
            copy(i).wait()

        def body(r, _):
            rows = row_block(r)
            x = x_buf[rows, :]
            xn_ref[rows, :] = _rmsnorm(x, nw_ref[...]).astype(BF16)
            o_ref[rows, :] = x
        lax.fori_loop(0, n_row_blocks, body, None)

    xn = xn_ref[...]
    g = _dot(xn, wg_ref[...].astype(BF16))
    u = _dot(xn, wu_ref[...].astype(BF16))
    o_ref[...] += _dot((_silu(g) * u).astype(BF16), (0.5 * wd_ref[...]).astype(BF16))

    if tail != "none":
        @pl.when(j == pl.num_programs(1) - 1)
        def _():
            def body(r, _):
                rows = row_block(r)
                hn = _rmsnorm(o_ref[rows, :], tw_ref[...])
                if tail == "final":
                    o_ref[rows, :] = hn
                else:
                    rest[0][rows, :] = hn.astype(BF16)
            lax.fori_loop(0, n_row_blocks, body, None)


def _ffn(x, norm_w, w_gu, w_down, tail_w, *, tail, tm=1024):
    m, d = x.shape
    f = w_down.shape[0]
    row_tile = pl.BlockSpec((tm, d), lambda i, j: (i, 0))
    out_specs, out_shape = [row_tile], [jax.ShapeDtypeStruct((m, d), F32)]
    scratch = [pltpu.VMEM((tm, d), BF16)]
    if tail == "next":
        out_specs.append(pl.BlockSpec((tm, d), lambda i, j: (i, 0), pipeline_mode=pl.Buffered(1)))
        out_shape.append(jax.ShapeDtypeStruct((m, d), BF16))
    prefetch_x = True
    tf = 512 if prefetch_x else 256
    nf = f // tf
    if prefetch_x:
        scratch += [pltpu.VMEM((tm, d), F32), pltpu.SemaphoreType.DMA(())]
    return pl.pallas_call(
        functools.partial(_ffn_kernel, tail=tail, prefetch_x=prefetch_x),
        grid=(m // tm, nf),
        in_specs=[
            pl.BlockSpec(memory_space=pl.ANY) if prefetch_x else row_tile,
            pl.BlockSpec((1, d), lambda i, j: (0, 0)),
            pl.BlockSpec((d, tf), lambda i, j: (0, j)),
            pl.BlockSpec((d, tf), lambda i, j: (0, j + nf)),
            pl.BlockSpec((tf, d), lambda i, j: (j, 0)),
            pl.BlockSpec((1, d), lambda i, j: (0, 0)),
        ],
        out_specs=out_specs,
        out_shape=out_shape,
        scratch_shapes=scratch,
        compiler_params=_params(2),
        name="ffn",
    )(x, norm_w.reshape(1, d), w_gu, w_gu, w_down, tail_w.reshape(1, d))


def _in_proj_kernel(xn_ref, wt_ref, wdt_ref, cw_ref, cb_ref, *rest, gelu_tiles, silu_tiles, n_side):
    side_in, (o_ref, dt_ref) = rest[:n_side], rest[n_side:n_side + 2]
    side_out = rest[n_side + 2:2 * n_side + 2]
    raw_even_ref, raw_odd_ref, halo_ref = rest[2 * n_side + 2:]
    for src_ref, dst_ref in zip(side_in, side_out):
        dst_ref[...] = src_ref[...].astype(BF16)

    i = pl.program_id(0)
    j = pl.program_id(1)
    conv_first = gelu_tiles + silu_tiles
    raw_refs = (raw_even_ref, raw_odd_ref)

    @pl.when(j == 0)
    def _():
        dt_ref[...] = _dot_nt(xn_ref[...], wdt_ref[...].astype(BF16))

    tm = o_ref.shape[0]

    def project(consume):
        w = wt_ref[...].astype(BF16)
        for r0 in range(0, tm, IN_PROJ_ROW_BLOCK):
            consume(r0, _dot_nt(xn_ref[r0:r0 + IN_PROJ_ROW_BLOCK, :], w))

    def store_with(activation):
        def consume(r0, acc):
            o_ref[r0:r0 + IN_PROJ_ROW_BLOCK, :] = activation(acc)
        return consume

    @pl.when(j < gelu_tiles)
    def _():
        project(store_with(_gelu_tanh))

    @pl.when((j >= gelu_tiles) & (j < gelu_tiles + silu_tiles))
    def _():
        project(store_with(_silu))

    @pl.when(j >= conv_first)
    def _():
        c = j - conv_first
        tail = slice(IN_PROJ_ROW_BLOCK, IN_PROJ_ROW_BLOCK + HALO_ROWS)
        n_blocks = tm // IN_PROJ_ROW_BLOCK

        @pl.when(i == 0)
        def _():
            raw_refs[0][0:HALO_ROWS, :] = jnp.zeros((HALO_ROWS, o_ref.shape[1]), F32)

        @pl.when(i > 0)
        def _():
            raw_refs[0][0:HALO_ROWS, :] = halo_ref[c]

        def conv_silu(r0, acc):
            block = r0 // IN_PROJ_ROW_BLOCK
            raw, previous = raw_refs[block % 2], raw_refs[(block - 1) % 2]
            if block > 0:
                raw[0:HALO_ROWS, :] = previous[tail, :]
            raw[HALO_ROWS:, :] = acc
            out = cb_ref[...]
            for tap in range(SSM_CONV):
                off = HALO_ROWS - (SSM_CONV - 1) + tap
                out = out + raw[off:off + IN_PROJ_ROW_BLOCK, :] * cw_ref[tap:tap + 1, :]
            o_ref[r0:r0 + IN_PROJ_ROW_BLOCK, :] = _silu(out)

        project(conv_silu)
        halo_ref[c] = raw_refs[(n_blocks - 1) % 2][tail, :]


def _in_proj(xn, w_in_t, conv_w, conv_b, side_weights, *, tm=2048, tn=512):
    m, k = xn.shape
    n_col = PROJ_MAIN // tn
    gelu_tiles, silu_tiles = 2 * GM_WIDTH // tn, SSM_WIDTH // tn
    conv_tiles = (SSM_WIDTH + 2 * SSM_BC_WIDTH) // tn
    kernel_fn = functools.partial(_in_proj_kernel, gelu_tiles=gelu_tiles, silu_tiles=silu_tiles,
                                  n_side=len(side_weights))
    assert SIDE_CAST_STEPS <= (m // tm) * n_col
    side_specs = [pl.BlockSpec((w.shape[0] // SIDE_CAST_STEPS, w.shape[1]),
                               lambda i, j: (jnp.minimum(i * n_col + j, SIDE_CAST_STEPS - 1), 0))
                  for w in side_weights]
    conv_tile = lambda i, j: (0, jnp.maximum(j - (gelu_tiles + silu_tiles), 0))
    w_dt_t = jnp.pad(w_in_t[PROJ_MAIN:], ((0, LANES - (w_in_t.shape[0] - PROJ_MAIN)), (0, 0)))
    return pl.pallas_call(
        kernel_fn,
        grid=(m // tm, n_col),
        in_specs=[
            pl.BlockSpec((tm, k), lambda i, j: (i, 0)),
            pl.BlockSpec((tn, k), lambda i, j: (j, 0)),
            pl.BlockSpec((LANES, k), lambda i, j: (0, 0)),
            pl.BlockSpec((SSM_CONV, tn), conv_tile),
            pl.BlockSpec((1, tn), conv_tile),
            *side_specs,
        ],
        out_specs=[pl.BlockSpec((tm, tn), lambda i, j: (i, j)),
                   pl.BlockSpec((tm, LANES), lambda i, j: (i, 0)),
                   *side_specs],
        out_shape=[jax.ShapeDtypeStruct((m, PROJ_MAIN), F32), jax.ShapeDtypeStruct((m, LANES), F32),
                   *[jax.ShapeDtypeStruct(w.shape, BF16) for w in side_weights]],
        scratch_shapes=[pltpu.VMEM((HALO_ROWS + IN_PROJ_ROW_BLOCK, tn), F32),
                        pltpu.VMEM((HALO_ROWS + IN_PROJ_ROW_BLOCK, tn), F32),
                        pltpu.VMEM((conv_tiles, HALO_ROWS, tn), F32)],
        compiler_params=_params(2),
        name="in_proj",
    )(xn, w_in_t, w_dt_t, conv_w, conv_b.reshape(1, -1), *side_weights)


def _sgu_init(ws_ref, wsc_ref):
    t_idx = lax.broadcasted_iota(jnp.int32, (CHUNK, CHUNK), 0)
    s_idx = lax.broadcasted_iota(jnp.int32, (CHUNK, CHUNK), 1)
    for g in range(GM_GROUPS):
        wsc_ref[g] = jnp.where(s_idx <= t_idx, ws_ref[g], 0.0).astype(BF16)


def _sgu_mix(gu_ref, gv_ref, vw_ref, wsc_ref, bs_ref, o_ref):
    vn = _rmsnorm(gv_ref[...], vw_ref[...]).astype(BF16)
    bias = bs_ref[...]
    for g in range(GM_GROUPS):
        cols = slice(g * GM_GROUP_WIDTH, (g + 1) * GM_GROUP_WIDTH)
        mixed = _dot(wsc_ref[g], vn[:, cols]) + bias[:, g:g + 1]
        o_ref[:, cols] = (gu_ref[:, cols] * mixed).astype(o_ref.dtype)


def _split3(v):
    hi = v.astype(BF16).astype(F32)
    rest = v - hi
    mid = rest.astype(BF16).astype(F32)
    return hi, mid, rest - mid


def _pack3(v):
    hi, mid, lo = _split3(v)
    lane = lax.broadcasted_iota(jnp.int32, v.shape, 1)
    packed = jnp.where(lane < SSM_HEADS, hi,
                       jnp.where(lane < 2 * SSM_HEADS, pltpu.roll(mid, SSM_HEADS, 1),
                                 jnp.where(lane < 3 * SSM_HEADS, pltpu.roll(lo, 2 * SSM_HEADS, 1), 0.0)))
    return packed.astype(BF16)


def _ssd_init(state_ref, expand_ref):
    state_ref[...] = jnp.zeros_like(state_ref)
    k_idx = lax.broadcasted_iota(jnp.int32, (LANES, SSM_WIDTH), 0)
    j_idx = lax.broadcasted_iota(jnp.int32, (LANES, SSM_WIDTH), 1)
    hit = (k_idx < 3 * SSM_HEADS) & (j_idx // SSM_HEADDIM == k_idx % SSM_HEADS)
    expand_ref[...] = hit.astype(F32).astype(BF16)


def _ssd_scan(gate_ref, xs_ref, bc_ref, dt_ref, dtb_ref, alog_ref, dskip_ref, nw_ref, o_ref,
              state_ref, y_ref, expand_ref):
    pre = dt_ref[...] + dtb_ref[...]
    dt = jnp.maximum(pre, 0.0) + jnp.log1p(jnp.exp(-jnp.abs(pre)))
    da = dt * (-jnp.exp(alog_ref[...]))
    row = lax.broadcasted_iota(jnp.int32, (CHUNK, CHUNK), 0)
    col = lax.broadcasted_iota(jnp.int32, (CHUNK, CHUNK), 1)
    causal = col <= row
    tri = causal.astype(F32).astype(BF16)
    da_hi, da_mid, da_lo = _split3(da)
    parts = _dot(tri, jnp.concatenate([da_hi, da_mid, da_lo], axis=1).astype(BF16))
    cum = parts[:, :LANES] + parts[:, LANES:2 * LANES] + parts[:, 2 * LANES:]
    cum_t = cum.T
    cum_last = cum[CHUNK - 1:CHUNK, :]

    packed = jnp.concatenate([_pack3(dt), _pack3(dt * jnp.exp(cum_last - cum)), _pack3(jnp.exp(cum))], axis=0)
    expanded = _dot(packed, expand_ref[...])
    dt_e = expanded[0:CHUNK]
    dt_decay_end_e = expanded[CHUNK:2 * CHUNK]
    exp_cum_e = expanded[2 * CHUNK:]
    chunk_decay_e = exp_cum_e[CHUNK - 1:CHUNK, :]

    xs = xs_ref[...]
    xdt = (xs * dt_e).astype(BF16)
    xdt_end = (xs * dt_decay_end_e).astype(BF16)
    lane_head = lax.broadcasted_iota(jnp.int32, (1, SSM_GROUP_WIDTH), 1) // SSM_HEADDIM
    head_lanes = [(lane_head == r).astype(F32).astype(BF16) for r in range(SSM_HEADS_PER_GROUP)]

    b_of = lambda g: bc_ref[:, g * SSM_STATE:(g + 1) * SSM_STATE].astype(BF16)
    c_of = lambda g: bc_ref[:, SSM_BC_WIDTH + g * SSM_STATE:SSM_BC_WIDTH + (g + 1) * SSM_STATE].astype(BF16)
    for g in range(SSM_GROUPS):
        gcols = slice(g * SSM_GROUP_WIDTH, (g + 1) * SSM_GROUP_WIDTH)
        b_g, c_g = b_of(g), c_of(g)
        if g % 2 == 0:
            pair = _dot_nt(jnp.concatenate([c_g, c_of(g + 1)], axis=0), jnp.concatenate([b_g, b_of(g + 1)], axis=0))
        half = (g % 2) * CHUNK
        scores = pair[half:half + CHUNK, half:half + CHUNK]
        xdt_g = xdt[:, gcols]
        y = _dot(c_g, state_ref[g].astype(BF16)) * exp_cum_e[:, gcols]
        decayed, stacked = [], []
        for r in range(SSM_HEADS_PER_GROUP):
            h = g * SSM_HEADS_PER_GROUP + r
            seg = cum[:, h:h + 1] - cum_t[h:h + 1, :]
            decay = jnp.exp(jnp.where(causal, seg, -jnp.inf))
            decayed.append((scores * decay).astype(BF16))
            stacked.append(xdt_g * head_lanes[r])
        y = y + _dot(jnp.concatenate(decayed, axis=1), jnp.concatenate(stacked, axis=0))
        state_ref[g] = state_ref[g] * chunk_decay_e[:, gcols] + _dot_tn(b_g, xdt_end[:, gcols])
        y_ref[:, gcols] = y

    y = y_ref[...] + dskip_ref[...] * xs
    y = y * gate_ref[...]
    o_ref[...] = _rmsnorm(y, nw_ref[...]).astype(o_ref.dtype)


def _mixer_kernel(gu_ref, gv_ref, vw_ref, ws_ref, bs_ref,
                  gate_ref, xs_ref, bc_ref, dt_ref, dtb_ref, alog_ref, dskip_ref, nw_ref,
                  o_ref,
                  state_ref, y_ref, expand_ref, wsc_ref):
    @pl.when(pl.program_id(0) == 0)
    def _():
        _ssd_init(state_ref, expand_ref)
        _sgu_init(ws_ref, wsc_ref)

    for c in range(MIXER_CHUNKS):
        rows = lambda ref: ref.at[c * CHUNK:(c + 1) * CHUNK, :]
        out = rows(o_ref)
        _sgu_mix(rows(gu_ref), rows(gv_ref), vw_ref, wsc_ref, bs_ref, out.at[:, 0:GM_WIDTH])
        _ssd_scan(rows(gate_ref), rows(xs_ref), rows(bc_ref), rows(dt_ref), dtb_ref, alog_ref, dskip_ref, nw_ref,
                  out.at[:, GM_WIDTH:], state_ref, y_ref, expand_ref)


def _mixer(proj, dt_raw, v_norm_w, w_s, b_s_t, dt_bias, a_log, d_skip_e, norm_w):
    m = proj.shape[0]
    row = lambda a: a.reshape(1, -1)
    c0 = 2 * GM_WIDTH // SSM_WIDTH
    full = lambda shape: pl.BlockSpec(shape, lambda i: (0,) * len(shape))
    chunk = lambda width, col: pl.BlockSpec((MIXER_CHUNKS * CHUNK, width), lambda i: (i, col))
    return pl.pallas_call(
        _mixer_kernel,
        grid=(m // (MIXER_CHUNKS * CHUNK),),
        in_specs=[
            chunk(GM_WIDTH, 0),
            chunk(GM_WIDTH, 1),
            full((1, GM_WIDTH)),
            full((GM_GROUPS, CHUNK, CHUNK)),
            full((CHUNK, LANES)),
            chunk(SSM_WIDTH, c0),
            chunk(SSM_WIDTH, c0 + 1),
            chunk(2 * SSM_BC_WIDTH, c0 + 2),
            chunk(LANES, 0),
            full((1, LANES)),
            full((1, LANES)),
            full((1, SSM_WIDTH)),
            full((1, SSM_WIDTH)),
        ],
        out_specs=chunk(GM_WIDTH + SSM_WIDTH, 0),
        out_shape=jax.ShapeDtypeStruct((m, GM_WIDTH + SSM_WIDTH), BF16),
        scratch_shapes=[pltpu.VMEM((SSM_GROUPS, SSM_STATE, SSM_GROUP_WIDTH), F32),
                        pltpu.VMEM((CHUNK, SSM_WIDTH), F32),
                        pltpu.VMEM((LANES, SSM_WIDTH), BF16),
                        pltpu.VMEM((GM_GROUPS, CHUNK, CHUNK), BF16)],
        compiler_params=_params(1),
        name="mixer",
    )(proj, proj, row(v_norm_w), w_s, b_s_t,
      proj, proj, proj, dt_raw, row(dt_bias), row(a_log), row(d_skip_e), row(norm_w))


WEIGHT_STAGE_ROWS = 256


def _for_each_row_block(w_hbm, cols, stage_ref, sems, use):
    n_blocks = w_hbm.shape[0] // WEIGHT_STAGE_ROWS

    def block_rows(r):
        return pl.ds(pl.multiple_of(r * WEIGHT_STAGE_ROWS, WEIGHT_STAGE_ROWS), WEIGHT_STAGE_ROWS)

    def copy(r, slot):
        return pltpu.make_async_copy(w_hbm.at[block_rows(r), cols], stage_ref.at[slot], sems.at[slot])

    copy(0, 0).start()

    def body(r, _):
        slot = r % 2

        @pl.when(r + 1 < n_blocks)
        def _():
            copy(r + 1, 1 - slot).start()

        copy(r, slot).wait()
        use(block_rows(r), stage_ref.at[slot])

    lax.fori_loop(0, n_blocks, body, None)


def _project_memory(memn_ref, w_hbm, cols, out_ref, acc_ref, stage_ref, sems):
    acc_ref[...] = jnp.zeros_like(acc_ref)

    def use(rows, block_ref):
        acc_ref[...] += _dot(memn_ref[:, rows], block_ref[...].astype(BF16))

    _for_each_row_block(w_hbm, cols, stage_ref, sems, use)
    out_ref[...] = acc_ref[...].astype(BF16)


def _xattn_kernel(h_ref, mixed_ref, nw_ref, mem_ref, memw_ref, wout_ref, wq_ref, wo_ref, wkv_hbm,
                  o_ref,
                  k_ref, v_ref, stage_ref, sems, h2_ref, q_ref, att_ref):
    d = h_ref.shape[1]

    @pl.when(pl.program_id(0) == 0)
    def _():
        q_ref[...] = _rmsnorm(mem_ref[...], memw_ref[...]).astype(BF16)
        _project_memory(q_ref, wkv_hbm, pl.ds(0, d), k_ref, h2_ref, stage_ref, sems)
        _project_memory(q_ref, wkv_hbm, pl.ds(d, d), v_ref, h2_ref, stage_ref, sems)

    h2_ref[...] = h_ref[...] + _dot(mixed_ref[...], wout_ref[...])
    q_ref[...] = _dot(_rmsnorm(h2_ref[...], nw_ref[...]).astype(BF16), wq_ref[...]).astype(BF16)
    scale = XA_HEADDIM ** -0.5
    for h in range(XA_HEADS):
        cols = slice(h * XA_HEADDIM, (h + 1) * XA_HEADDIM)
        logits = _dot_nt(q_ref[:, cols], k_ref[:, cols]) * scale
        p = jnp.exp(logits - jnp.max(logits, axis=-1, keepdims=True))
        p = p / jnp.sum(p, axis=-1, keepdims=True)
        att_ref[:, cols] = _dot(p.astype(BF16), v_ref[:, cols]).astype(BF16)
    o_ref[...] = h2_ref[...] + _dot(att_ref[...], wo_ref[...])


def _xattn(h, mixed, norm_w, mem, mem_norm_w, w_out, w_q, w_o, w_kv):
    m, d = h.shape
    tm = MEM_LEN
    once = pl.Buffered(1)
    resident = lambda w: pl.BlockSpec(w.shape, lambda i: (0, 0), pipeline_mode=once)
    return pl.pallas_call(
        _xattn_kernel,
        grid=(m // tm,),
        in_specs=[
            pl.BlockSpec((tm, d), lambda i: (i, 0)),
            pl.BlockSpec((tm, mixed.shape[1]), lambda i: (i, 0)),
            pl.BlockSpec((1, d), lambda i: (0, 0)),
            pl.BlockSpec((MEM_LEN, d), lambda i: (0, 0), pipeline_mode=once),
            pl.BlockSpec((1, d), lambda i: (0, 0)),
            resident(w_out), resident(w_q), resident(w_o),
            pl.BlockSpec(memory_space=pl.ANY),
        ],
        out_specs=pl.BlockSpec((tm, d), lambda i: (i, 0)),
        out_shape=jax.ShapeDtypeStruct((m, d), F32),
        scratch_shapes=[pltpu.VMEM((MEM_LEN, d), BF16),
                        pltpu.VMEM((MEM_LEN, d), BF16),
                        pltpu.VMEM((2, WEIGHT_STAGE_ROWS, d), F32),
                        pltpu.SemaphoreType.DMA((2,)),
                        pltpu.VMEM((tm, d), F32),
                        pltpu.VMEM((tm, d), BF16),
                        pltpu.VMEM((tm, d), BF16)],
        compiler_params=_params(1),
        name="xattn",
    )(h, mixed, norm_w.reshape(1, d), mem, mem_norm_w.reshape(1, d), w_out, w_q, w_o, w_kv)


def kernel(x, mem, ffn1_norm, ffn1_w_gu, ffn1_w_down, mix_norm, w_in, gm_v_norm, gm_w_s, gm_b_s, ssm_conv_w, ssm_conv_b, ssm_dt_bias, ssm_a_log, ssm_d, ssm_norm, w_out, xa_norm, mem_norm, xa_w_q, xa_w_kv, xa_w_o, ffn2_norm, ffn2_w_gu, ffn2_w_down, final_norm):
    pad_lanes = lambda a: jnp.pad(a, ((0, 0), (0, LANES - a.shape[1])))
    h = x[0]
    for i in range(ffn1_norm.shape[0]):
        h, xn = _ffn(h, ffn1_norm[i], ffn1_w_gu[i], ffn1_w_down[i], mix_norm[i], tail="next")

        proj, dt_raw, w_out_b, w_q_b, w_o_b = _in_proj(xn, jnp.swapaxes(w_in[i], 0, 1), ssm_conv_w[i], ssm_conv_b[i],
                                                       [w_out[i], xa_w_q[i], xa_w_o[i]])
        mixed = _mixer(proj, dt_raw, gm_v_norm[i], gm_w_s[i], pad_lanes(gm_b_s[i].T),
                       pad_lanes(ssm_dt_bias[i][None])[0], pad_lanes(ssm_a_log[i][None])[0],
                       jnp.repeat(ssm_d[i], SSM_HEADDIM), ssm_norm[i])
        h = _xattn(h, mixed, xa_norm[i], mem[0], mem_norm[i], w_out_b, w_q_b, w_o_b, xa_w_kv[i])

        last = i == ffn1_norm.shape[0] - 1
        h = _ffn(h, ffn2_norm[i], ffn2_w_gu[i], ffn2_w_down[i], final_norm, tail="final" if last else "none")[0]
    return h[None]
```
